```python
import jax, jax.numpy as jnp
from jax import lax
import numpy as np

D_MODEL = 2048
BATCH = 2
SEQ = 8192
DEPTH = 4

N_MIXERS = 3
BLOCK = 128
EPS = 1e-6

FOX_HEADS = 16
FOX_HEAD_DIM = D_MODEL // FOX_HEADS
FOX_FORGET_BIAS = 2.0

SGU_WIDTH = D_MODEL
SGU_GROUPS = 16
SGU_GROUP_DIM = SGU_WIDTH // SGU_GROUPS
SGU_CHUNK = 128

SWA_HEAD_DIM = 64
SWA_Q_HEADS = D_MODEL // SWA_HEAD_DIM
SWA_KV_HEADS = 8
SWA_WINDOW = 128
ROPE_DIM = SWA_HEAD_DIM // 4
ROPE_THETA = 500000.0

D_FF = ((8 * D_MODEL + 3 * 256 - 1) // (3 * 256)) * 256

N_FOX = (DEPTH + 2) // 3
N_SGU = (DEPTH + 1) // 3
N_SWA = DEPTH // 3

kernel_name = "hybrid_fox_gmlp_swa_sink_adaln"

F32 = jnp.float32


def rmsnorm(x, g):
    xf = x.astype(F32)
    y = xf * lax.rsqrt(jnp.mean(xf * xf, axis=-1, keepdims=True) + EPS)
    return (y * g.astype(F32)).astype(x.dtype)


def layernorm(x, g, b):
    xf = x.astype(F32)
    mu = jnp.mean(xf, axis=-1, keepdims=True)
    var = jnp.mean(jnp.square(xf - mu), axis=-1, keepdims=True)
    y = (xf - mu) * lax.rsqrt(var + EPS)
    return (y * g.astype(F32) + b.astype(F32)).astype(x.dtype)


def modulate(h, shift, scale):
    return h * (1 + scale[:, None, :]) + shift[:, None, :]


def rope_tables(positions):
    inv = ROPE_THETA ** (-jnp.arange(0, ROPE_DIM, 2, dtype=F32) / ROPE_DIM)
    ang = positions.astype(F32)[..., None] * inv
    return jnp.cos(ang), jnp.sin(ang)


def apply_partial_rope(x, cos, sin):
    half = ROPE_DIM // 2
    x1 = x[..., :half]
    x2 = x[..., half:ROPE_DIM]
    rest = x[..., ROPE_DIM:]
    c = cos[:, :, None, :].astype(x.dtype)
    s = sin[:, :, None, :].astype(x.dtype)
    return jnp.concatenate([x1 * c - x2 * s, x2 * c + x1 * s, rest], axis=-1)


def fox_attention(h, w_in, b_f, w_out):
    B, S, _ = h.shape
    H, Dh = FOX_HEADS, FOX_HEAD_DIM
    proj = h @ w_in
    q, k, v, fg = jnp.split(proj, [H * Dh, 2 * H * Dh, 3 * H * Dh], axis=-1)
    q = q.reshape(B, S, H, Dh)
    k = k.reshape(B, S, H, Dh)
    v = v.reshape(B, S, H, Dh)
    log_f = jax.nn.log_sigmoid((fg + b_f).astype(F32))
    cum = lax.cumsum(log_f, axis=1)
    cum_t = cum.transpose(0, 2, 1)
    nb = S // BLOCK
    qb = q.reshape(B, nb, BLOCK, H, Dh).transpose(1, 0, 2, 3, 4)
    fq = cum.reshape(B, nb, BLOCK, H).transpose(1, 0, 3, 2)
    kpos = jnp.arange(S)
    scale = Dh ** -0.5

    def block_fn(args):
        i, q_i, f_i = args
        s = jnp.einsum('bqhd,bkhd->bhqk', q_i, k, preferred_element_type=F32) * scale
        s = s + f_i[..., None] - cum_t[:, :, None, :]
        qpos = i * BLOCK + jnp.arange(BLOCK)
        mask = kpos[None, :] <= qpos[:, None]
        s = jnp.where(mask, s, -jnp.inf)
        p = jax.nn.softmax(s, axis=-1)
        return jnp.einsum('bhqk,bkhd->bqhd', p.astype(v.dtype), v)

    out = lax.map(block_fn, (jnp.arange(nb), qb, fq))
    out = out.transpose(1, 0, 2, 3, 4).reshape(B, S, H * Dh)
    return out @ w_out


def gmlp_sgu(h, w_in, ln_g, ln_b, w_s, b_s, w_out):
    B, S, _ = h.shape
    z = jax.nn.gelu(h @ w_in)
    u, v = jnp.split(z, 2, axis=-1)
    v = layernorm(v, ln_g, ln_b)
    nc = S // SGU_CHUNK
    vg = v.reshape(B, nc, SGU_CHUNK, SGU_GROUPS, SGU_GROUP_DIM)
    causal = jnp.tril(jnp.ones((SGU_CHUNK, SGU_CHUNK), dtype=bool))
    ws = jnp.where(causal[None], w_s, jnp.zeros_like(w_s))
    f = jnp.einsum('gts,bcsgd->bctgd', ws, vg)
    f = f + b_s.T[None, None, :, :, None]
    gated = u * f.reshape(B, S, SGU_WIDTH)
    return gated @ w_out


def swa_sink_attention(h, w_in, sinks, w_out, cos, sin):
    B, S, _ = h.shape
    Hq, Hk, Dh = SWA_Q_HEADS, SWA_KV_HEADS, SWA_HEAD_DIM
    G = Hq // Hk
    proj = h @ w_in
    q, k, v = jnp.split(proj, [Hq * Dh, (Hq + Hk) * Dh], axis=-1)
    q = apply_partial_rope(q.reshape(B, S, Hq, Dh), cos, sin)
    k = apply_partial_rope(k.reshape(B, S, Hk, Dh), cos, sin)
    v = v.reshape(B, S, Hk, Dh)
    nb = S // BLOCK
    qb = q.reshape(B, nb, BLOCK, Hk, G, Dh)
    kb = k.reshape(B, nb, BLOCK, Hk, Dh)
    vb = v.reshape(B, nb, BLOCK, Hk, Dh)
    pad = ((0, 0), (1, 0), (0, 0), (0, 0), (0, 0))
    kband = jnp.concatenate([jnp.pad(kb[:, :-1], pad), kb], axis=2)
    vband = jnp.concatenate([jnp.pad(vb[:, :-1], pad), vb], axis=2)
    s = jnp.einsum('bnqhgd,bnkhd->bnhgqk', qb, kband, preferred_element_type=F32) * (Dh ** -0.5)
    qi = jnp.arange(BLOCK)[:, None]
    ki = jnp.arange(2 * BLOCK)[None, :] - BLOCK
    rel = qi - ki
    valid = (rel >= 0) & (rel < SWA_WINDOW)
    in_seq = (jnp.arange(nb)[:, None, None] * BLOCK + ki[None]) >= 0
    mask = valid[None] & in_seq
    s = jnp.where(mask[None, :, None, None], s, -jnp.inf)
    sink = sinks.astype(F32).reshape(Hk, G)[None, None, :, :, None, None]
    m = jnp.maximum(jnp.max(s, axis=-1, keepdims=True), sink)
    p = jnp.exp(s - m)
    p = p / (jnp.sum(p, axis=-1, keepdims=True) + jnp.exp(sink - m))
    o = jnp.einsum('bnhgqk,bnkhd->bnqhgd', p.astype(v.dtype), vband)
    return o.reshape(B, S, Hq * Dh) @ w_out


def swiglu(h, w_gu, w_down):
    g, u = jnp.split(h @ w_gu, 2, axis=-1)
    return (jax.nn.silu(g) * u) @ w_down


def setup_inputs(seed: int = 0) -> dict:
    key = jax.random.key(seed)
    ks = jax.random.split(key, 32)
    D = D_MODEL
    nrm = lambda k, shape, fan_in, mult=1.0: jax.random.normal(k, shape, F32) * (mult * fan_in ** -0.5)
    fox_in = 3 * FOX_HEADS * FOX_HEAD_DIM + FOX_HEADS
    swa_in = (SWA_Q_HEADS + 2 * SWA_KV_HEADS) * SWA_HEAD_DIM
    x = jax.random.normal(ks[0], (BATCH, SEQ, D), F32)
    c = jax.random.normal(ks[1], (BATCH, D), F32)
    offset = jax.random.randint(ks[2], (BATCH, 1), 0, 4096, dtype=jnp.int32)
    positions = offset + jnp.arange(SEQ, dtype=jnp.int32)[None, :]
    gain = lambda k, shape: 1.0 + 0.02 * jax.random.normal(k, shape, F32)
    return {
        "x": x,
        "c": c,
        "positions": positions,
        "ada_w": nrm(ks[3], (DEPTH, D, 6 * D), D, 0.5),
        "ada_b": 0.01 * jax.random.normal(ks[4], (DEPTH, 6 * D), F32),
        "mix_pre_g": gain(ks[5], (DEPTH, D)),
        "mix_post_g": gain(ks[6], (DEPTH, D)),
        "ffn_pre_g": gain(ks[7], (DEPTH, D)),
        "ffn_post_g": gain(ks[8], (DEPTH, D)),
        "ffn_w_gu": nrm(ks[9], (DEPTH, D, 2 * D_FF), D),
        "ffn_w_down": nrm(ks[10], (DEPTH, D_FF, D), D_FF),
        "fox_w_in": nrm(ks[11], (N_FOX, D, fox_in), D),
        "fox_b_f": FOX_FORGET_BIAS + 0.5 * jax.random.normal(ks[12], (N_FOX, FOX_HEADS), F32),
        "fox_w_out": nrm(ks[13], (N_FOX, FOX_HEADS * FOX_HEAD_DIM, D), FOX_HEADS * FOX_HEAD_DIM),
        "sgu_w_in": nrm(ks[14], (N_SGU, D, 2 * SGU_WIDTH), D),
        "sgu_ln_g": gain(ks[15], (N_SGU, SGU_WIDTH)),
        "sgu_ln_b": 0.01 * jax.random.normal(ks[16], (N_SGU, SGU_WIDTH), F32),
        "sgu_w_s": nrm(ks[17], (N_SGU, SGU_GROUPS, SGU_CHUNK, SGU_CHUNK), SGU_CHUNK),
        "sgu_b_s": 1.0 + 0.02 * jax.random.normal(ks[18], (N_SGU, SGU_GROUPS, SGU_CHUNK), F32),
        "sgu_w_out": nrm(ks[19], (N_SGU, SGU_WIDTH, D), SGU_WIDTH),
        "swa_w_in": nrm(ks[20], (N_SWA, D, swa_in), D),
        "swa_sinks": 0.5 * jax.random.normal(ks[21], (N_SWA, SWA_Q_HEADS), F32),
        "swa_w_out": nrm(ks[22], (N_SWA, SWA_Q_HEADS * SWA_HEAD_DIM, D), SWA_Q_HEADS * SWA_HEAD_DIM),
    }


def reference(x, c, positions, ada_w, ada_b, mix_pre_g, mix_post_g, ffn_pre_g, ffn_post_g,
              ffn_w_gu, ffn_w_down, fox_w_in, fox_b_f, fox_w_out, sgu_w_in, sgu_ln_g,
              sgu_ln_b, sgu_w_s, sgu_b_s, sgu_w_out, swa_w_in, swa_sinks, swa_w_out):
    cos, sin = rope_tables(positions)
    c_act = jax.nn.silu(c)
    for i in range(DEPTH):
        mod = c_act @ ada_w[i] + ada_b[i]
        sh_m, sc_m, g_m, sh_f, sc_f, g_f = jnp.split(mod, 6, axis=-1)
        h = modulate(rmsnorm(x, mix_pre_g[i]), sh_m, sc_m)
        kind, j = i % N_MIXERS, i // N_MIXERS
        if kind == 0:
            y = fox_attention(h, fox_w_in[j], fox_b_f[j], fox_w_out[j])
        elif kind == 1:
            y = gmlp_sgu(h, sgu_w_in[j], sgu_ln_g[j], sgu_ln_b[j], sgu_w_s[j], sgu_b_s[j], sgu_w_out[j])
        else:
            y = swa_sink_attention(h, swa_w_in[j], swa_sinks[j], swa_w_out[j], cos, sin)
        x = x + g_m[:, None, :] * rmsnorm(y, mix_post_g[i])
        h = modulate(rmsnorm(x, ffn_pre_g[i]), sh_f, sc_f)
        y = swiglu(h, ffn_w_gu[i], ffn_w_down[i])
        x = x + g_f[:, None, :] * rmsnorm(y, ffn_post_g[i])
    return x
```

```python
import functools

import jax
import jax.numpy as jnp
from jax import lax
from jax.experimental import pallas as pl
from jax.experimental.pallas import tpu as pltpu

F32 = jnp.float32
BF16 = jnp.bfloat16

EPS = 1e-6
N_MIXERS = 3
BLOCK = 128
SWA_HEAD_DIM = 64
ROPE_THETA = 500000.0
LANES = 128
SUBLANES = 8
N_MOD = 6
VMEM_LIMIT = 56 * 1024 * 1024


def _cparams(*sem):
    return pltpu.CompilerParams(dimension_semantics=sem, vmem_limit_bytes=VMEM_LIMIT)


def _const_spec(shape):
    nd = len(shape)
    return pl.BlockSpec(shape, lambda *_: (0,) * nd, pipeline_mode=pl.Buffered(1))


def _pick(n, pref):
    t = min(n, pref)
    while n % t:
        t -= LANES
    return t


def _rms(x):
    return x * lax.rsqrt(jnp.mean(x * x, axis=-1, keepdims=True) + EPS)


def _prenorm(x, g, sh, sc):
    return (_rms(x) * g) * (1.0 + sc) + sh


def _postnorm_residual(x, y, g, gate):
    return x + gate * (_rms(y) * g)


def _dot(a, b):
    return jnp.dot(a, b, preferred_element_type=F32)


def _dot_nt(a, b):
    return lax.dot_general(a, b, (((1,), (1,)), ((), ())), preferred_element_type=F32)


def _adaln_kernel(c_ref, w_ref, b_ref, o_ref):
    c = c_ref[...]
    ca = (c * jax.nn.sigmoid(c)).astype(BF16)
    o_ref[0] = _dot(ca, w_ref[0].astype(BF16)) + b_ref[0]


def _adaln(c_pad, ada_w, ada_b):
    depth, d, n = ada_w.shape
    tn = _pick(n, 1024)
    return pl.pallas_call(
        _adaln_kernel,
        out_shape=jax.ShapeDtypeStruct((depth, SUBLANES, n), F32),
        grid=(depth, n // tn),
        in_specs=[
            pl.BlockSpec((SUBLANES, d), lambda i, j: (0, 0)),
            pl.BlockSpec((1, d, tn), lambda i, j: (i, 0, j)),
            pl.BlockSpec((1, 1, tn), lambda i, j: (i, 0, j)),
        ],
        out_specs=pl.BlockSpec((1, SUBLANES, tn), lambda i, j: (i, 0, j)),
        compiler_params=_cparams("parallel", "parallel"),
        name="adaln",
    )(c_pad, ada_w, ada_b.reshape(depth, 1, n))


class _Mod:
    def __init__(self, mod, layer, seq):
        self.mod = mod
        self.layer = layer
        self.seq = seq
        self.d = mod.shape[-1]

    def spec(self, k, tm):
        layer, tpb = self.layer, self.seq // tm
        return pl.BlockSpec(
            (1, 1, self.d),
            lambda i, *_: ((layer * SUBLANES + i // tpb) * N_MOD + k, 0, 0))


def _proj_kernel(x_ref, g_ref, sh_ref, sc_ref, w_ref, o_ref, h_ref):
    @pl.when(pl.program_id(1) == 0)
    def _():
        h_ref[...] = _prenorm(x_ref[...], g_ref[...], sh_ref[0], sc_ref[0]).astype(BF16)

    o_ref[...] = _dot(h_ref[...], w_ref[...]).astype(o_ref.dtype)


def _fox_proj_kernel(x_ref, g_ref, sh_ref, sc_ref, w_ref, wf_ref, o_ref, fg_ref, h_ref):
    @pl.when(pl.program_id(1) == 0)
    def _():
        h = _prenorm(x_ref[...], g_ref[...], sh_ref[0], sc_ref[0]).astype(BF16)
        h_ref[...] = h
        fg_ref[...] = _dot(h, wf_ref[...])

    o_ref[...] = _dot(h_ref[...], w_ref[...]).astype(o_ref.dtype)


def _fox_proj(x, gain, mod, w, wf, tm, tn):
    t, d = x.shape
    n = w.shape[1]
    return pl.pallas_call(
        _fox_proj_kernel,
        out_shape=(jax.ShapeDtypeStruct((t, n), BF16), jax.ShapeDtypeStruct((t, LANES), F32)),
        grid=(t // tm, n // tn),
        in_specs=[
            pl.BlockSpec((tm, d), lambda i, j: (i, 0)),
            pl.BlockSpec((1, d), lambda i, j: (0, 0)),
            mod.spec(0, tm), mod.spec(1, tm),
            pl.BlockSpec((d, tn), lambda i, j: (0, j)),
            _const_spec((d, LANES)),
        ],
        out_specs=(pl.BlockSpec((tm, tn), lambda i, j: (i, j)),
                   pl.BlockSpec((tm, LANES), lambda i, j: (i, 0))),
        scratch_shapes=[pltpu.VMEM((tm, d), BF16)],
        compiler_params=_cparams("parallel", "arbitrary"),
        name="fox_proj",
    )(x, gain, mod.mod, mod.mod, w, wf)


def _swa_proj_kernel(x_ref, g_ref, sh_ref, sc_ref, w_ref, pos_ref, inv_ref, m1_ref, m2_ref,
                     o_ref, h_ref, cos_ref, s1_ref, s2_ref, *, n_rope_tiles, shift):
    j = pl.program_id(1)

    @pl.when(j == 0)
    def _():
        h_ref[...] = _prenorm(x_ref[...], g_ref[...], sh_ref[0], sc_ref[0]).astype(BF16)
        ang = pos_ref[...].astype(F32) * inv_ref[...]
        sn = jnp.sin(ang)
        cos_ref[...] = jnp.cos(ang)
        s1_ref[...] = sn * m1_ref[...]
        s2_ref[...] = sn * m2_ref[...]

    acc = _dot(h_ref[...], w_ref[...])
    tn = acc.shape[1]

    @pl.when(j < n_rope_tiles)
    def _():
        cs, s1, s2 = cos_ref[...], s1_ref[...], s2_ref[...]
        for c in range(tn // LANES):
            a = acc[:, c * LANES:(c + 1) * LANES]
            r = a * cs + pltpu.roll(a, LANES - shift, 1) * s1 + pltpu.roll(a, shift, 1) * s2
            o_ref[:, c * LANES:(c + 1) * LANES] = r.astype(o_ref.dtype)

    @pl.when(j >= n_rope_tiles)
    def _():
        o_ref[...] = acc.astype(o_ref.dtype)


def _swa_proj(x, gain, mod, w, pos, inv_l, m1, m2, tm, tn, n_rope_cols, shift):
    t, d = x.shape
    n = w.shape[1]
    row = lambda shape: pl.BlockSpec(shape, lambda i, j: (0, 0))
    return pl.pallas_call(
        functools.partial(_swa_proj_kernel, n_rope_tiles=n_rope_cols // tn, shift=shift),
        out_shape=jax.ShapeDtypeStruct((t, n), BF16),
        grid=(t // tm, n // tn),
        in_specs=[
            pl.BlockSpec((tm, d), lambda i, j: (i, 0)),
            row((1, d)),
            mod.spec(0, tm), mod.spec(1, tm),
            pl.BlockSpec((d, tn), lambda i, j: (0, j)),
            pl.BlockSpec((tm, 1), lambda i, j: (i, 0)),
            row((1, LANES)), row((1, LANES)), row((1, LANES)),
        ],
        out_specs=pl.BlockSpec((tm, tn), lambda i, j: (i, j)),
        scratch_shapes=[pltpu.VMEM((tm, d), BF16)] + [pltpu.VMEM((tm, LANES), F32)] * 3,
        compiler_params=_cparams("parallel", "arbitrary"),
        name="swa_proj",
    )(x, gain, mod.mod, mod.mod, w, pos, inv_l, m1, m2)


def _out_kernel(a_ref, w_ref, x_ref, g_ref, gate_ref, o_ref):
    y = _dot(a_ref[...], w_ref[...])
    o_ref[...] = _postnorm_residual(x_ref[...], y, g_ref[...], gate_ref[0])


def _out_proj(a, w, x, gain, mod, tm):
    t, k = a.shape
    d = w.shape[1]
    return pl.pallas_call(
        _out_kernel,
        out_shape=jax.ShapeDtypeStruct((t, d), F32),
        grid=(t // tm,),
        in_specs=[
            pl.BlockSpec((tm, k), lambda i: (i, 0)),
            _const_spec((k, d)),
            pl.BlockSpec((tm, d), lambda i: (i, 0)),
            pl.BlockSpec((1, d), lambda i: (0, 0)),
            mod.spec(2, tm),
        ],
        out_specs=pl.BlockSpec((tm, d), lambda i: (i, 0)),
        compiler_params=_cparams("parallel"),
        name="out_proj",
    )(a, w, x, gain, mod.mod)


def _ffn_kernel(x_ref, gpre_ref, sh_ref, sc_ref, wg_ref, wu_ref, wd_ref, gpost_ref, gate_ref,
                o_ref, h_ref, acc_ref):
    j = pl.program_id(1)

    @pl.when(j == 0)
    def _():
        h_ref[...] = _prenorm(x_ref[...], gpre_ref[...], sh_ref[0], sc_ref[0]).astype(BF16)
        acc_ref[...] = jnp.zeros_like(acc_ref)

    h = h_ref[...]
    g = _dot(h, wg_ref[...])
    u = _dot(h, wu_ref[...])
    a = (g * jax.nn.sigmoid(g) * u).astype(BF16)
    acc_ref[...] += _dot(a, wd_ref[...])

    @pl.when(j == pl.num_programs(1) - 1)
    def _():
        o_ref[...] = _postnorm_residual(x_ref[...], acc_ref[...], gpost_ref[...], gate_ref[0])


def _ffn(x, gpre, gpost, mod, w_gu, w_down, tm, tf):
    t, d = x.shape
    dff = w_down.shape[0]
    nf = dff // tf
    row = lambda: pl.BlockSpec((1, d), lambda i, j: (0, 0))
    return pl.pallas_call(
        _ffn_kernel,
        out_shape=jax.ShapeDtypeStruct((t, d), F32),
        grid=(t // tm, nf),
        in_specs=[
            pl.BlockSpec((tm, d), lambda i, j: (i, 0)),
            row(), mod.spec(3, tm), mod.spec(4, tm),
            pl.BlockSpec((d, tf), lambda i, j: (0, j)),
            pl.BlockSpec((d, tf), lambda i, j: (0, nf + j)),
            pl.BlockSpec((tf, d), lambda i, j: (j, 0)),
            row(), mod.spec(5, tm),
        ],
        out_specs=pl.BlockSpec((tm, d), lambda i, j: (i, 0)),
        scratch_shapes=[pltpu.VMEM((tm, d), BF16), pltpu.VMEM((tm, d), F32)],
        compiler_params=_cparams("parallel", "arbitrary"),
        name="ffn",
    )(x, gpre, mod.mod, mod.mod, w_gu, w_gu, w_down, gpost, mod.mod)


def _gate_kernel(fg_ref, b_ref, cum_ref, cumt_ref, carry_ref):
    @pl.when(pl.program_id(1) == 0)
    def _():
        carry_ref[...] = jnp.zeros_like(carry_ref)

    z = fg_ref[...] + b_ref[...]
    lf = jnp.minimum(z, 0.0) - jnp.log1p(jnp.exp(-jnp.abs(z)))
    n = lf.shape[0]
    tril = (lax.broadcasted_iota(jnp.int32, (n, n), 1)
            <= lax.broadcasted_iota(jnp.int32, (n, n), 0)).astype(BF16)
    hi = lf.astype(BF16)
    r1 = lf - hi.astype(F32)
    mid = r1.astype(BF16)
    lo = (r1 - mid.astype(F32)).astype(BF16)
    cum = (_dot(tril, hi) + _dot(tril, mid)) + _dot(tril, lo) + carry_ref[...]
    cum_ref[...] = cum
    cumt_ref[0] = cum.T
    carry_ref[...] = cum[n - 1:n, :]


def _gate_cumsum(fg, b_pad, batch, tg):
    t = fg.shape[0]
    s = t // batch
    nt = s // tg
    return pl.pallas_call(
        _gate_kernel,
        out_shape=(jax.ShapeDtypeStruct((t, LANES), F32),
                   jax.ShapeDtypeStruct((batch, LANES, s), F32)),
        grid=(batch, nt),
        in_specs=[
            pl.BlockSpec((tg, LANES), lambda b, j: (b * nt + j, 0)),
            pl.BlockSpec((1, LANES), lambda b, j: (0, 0)),
        ],
        out_specs=(pl.BlockSpec((tg, LANES), lambda b, j: (b * nt + j, 0)),
                   pl.BlockSpec((1, LANES, tg), lambda b, j: (b, 0, j))),
        scratch_shapes=[pltpu.VMEM((1, LANES), F32)],
        compiler_params=_cparams("parallel", "arbitrary"),
        name="fox_gate",
    )(fg, b_pad)


def _fox_attn_kernel(q_ref, k_ref, v_ref, fq_ref, fk_ref, o_ref, *, tk, scale):
    h = pl.program_id(1)
    i = pl.program_id(2)
    tq, dh = q_ref.shape
    q = (q_ref[...].astype(F32) * scale).astype(BF16)
    lane = lax.broadcasted_iota(jnp.int32, fq_ref.shape, 1)
    fq = jnp.sum(jnp.where(lane == h, fq_ref[...], 0.0), axis=-1, keepdims=True)

    def step(c, carry, diagonal):
        m, l, acc = carry
        off = pl.multiple_of(c * tk, tk)
        ks = k_ref[pl.ds(off, tk), :]
        vs = v_ref[pl.ds(off, tk), :]
        s = _dot_nt(q, ks) + (fq - fk_ref[0, 0, c])
        if diagonal:
            keep = (lax.broadcasted_iota(jnp.int32, s.shape, 1)
                    <= lax.broadcasted_iota(jnp.int32, s.shape, 0))
            s = jnp.where(keep, s, -jnp.inf)
        m_new = jnp.maximum(m, jnp.max(s, axis=-1, keepdims=True))
        alpha = jnp.exp(m - m_new)
        p = jnp.exp(s - m_new)
        l = alpha * l + jnp.sum(p, axis=-1, keepdims=True)
        acc = alpha * acc + _dot(p.astype(BF16), vs)
        return m_new, l, acc

    init = (jnp.full((tq, 1), -jnp.inf, F32), jnp.zeros((tq, 1), F32), jnp.zeros((tq, dh), F32))
    carry = lax.fori_loop(0, i, lambda c, car: step(c, car, False), init)
    _, l, acc = step(i, carry, True)
    o_ref[...] = (acc / l).astype(o_ref.dtype)


def _fox_attn(qkv, cum, cum_rows, batch, heads, tq):
    t = qkv.shape[0]
    dh = qkv.shape[1] // (3 * heads)
    s = t // batch
    nq = s // tq
    return pl.pallas_call(
        functools.partial(_fox_attn_kernel, tk=tq, scale=dh ** -0.5),
        out_shape=jax.ShapeDtypeStruct((t, heads * dh), BF16),
        grid=(batch, heads, nq),
        in_specs=[
            pl.BlockSpec((tq, dh), lambda b, h, i: (b * nq + i, h)),
            pl.BlockSpec((s, dh), lambda b, h, i: (b, heads + h)),
            pl.BlockSpec((s, dh), lambda b, h, i: (b, 2 * heads + h)),
            pl.BlockSpec((tq, LANES), lambda b, h, i: (b * nq + i, 0)),
            pl.BlockSpec((1, 1, nq, 1, tq), lambda b, h, i: (b, h, 0, 0, 0)),
        ],
        out_specs=pl.BlockSpec((tq, dh), lambda b, h, i: (b * nq + i, h)),
        compiler_params=_cparams("parallel", "parallel", "arbitrary"),
        name="fox_attn",
    )(qkv, qkv, qkv, cum, cum_rows)


def _sgu_kernel(x_ref, gpre_ref, sh_ref, sc_ref, win_ref, lng_ref, lnb_ref, ws_ref, bs_ref,
                wout_ref, gpost_ref, gate_ref, o_ref, u_ref, vn_ref, gated_ref):
    x = x_ref[...]
    tm = x.shape[0]
    width = u_ref.shape[1]
    groups, chunk, _ = ws_ref.shape
    gd = width // groups
    h = _prenorm(x, gpre_ref[...], sh_ref[0], sc_ref[0]).astype(BF16)
    u_ref[...] = jax.nn.gelu(_dot(h, win_ref[:, :width]))
    v = jax.nn.gelu(_dot(h, win_ref[:, width:]))
    mu = jnp.mean(v, axis=-1, keepdims=True)
    vc = v - mu
    var = jnp.mean(vc * vc, axis=-1, keepdims=True)
    vn_ref[...] = (vc * lax.rsqrt(var + EPS) * lng_ref[...] + lnb_ref[...]).astype(BF16)
    causal = (lax.broadcasted_iota(jnp.int32, (chunk, chunk), 1)
              <= lax.broadcasted_iota(jnp.int32, (chunk, chunk), 0))
    for g in range(groups):
        wg = jnp.where(causal, ws_ref[g], jnp.zeros((), BF16))
        bias = bs_ref[:, g:g + 1]
        cols = slice(g * gd, (g + 1) * gd)
        for c in range(tm // chunk):
            rows = slice(c * chunk, (c + 1) * chunk)
            f = _dot(wg, vn_ref[rows, cols]) + bias
            gated_ref[rows, cols] = (u_ref[rows, cols] * f).astype(BF16)
    y = _dot(gated_ref[...], wout_ref[...])
    o_ref[...] = _postnorm_residual(x, y, gpost_ref[...], gate_ref[0])


def _sgu(x, gpre, gpost, mod, w_in, ln_g, ln_b, w_s, b_st, w_out, tm):
    t, d = x.shape
    width = w_out.shape[0]
    row = lambda n: pl.BlockSpec((1, n), lambda i: (0, 0))
    return pl.pallas_call(
        _sgu_kernel,
        out_shape=jax.ShapeDtypeStruct((t, d), F32),
        grid=(t // tm,),
        in_specs=[
            pl.BlockSpec((tm, d), lambda i: (i, 0)),
            row(d), mod.spec(0, tm), mod.spec(1, tm),
            _const_spec(w_in.shape),
            row(width), row(width),
            _const_spec(w_s.shape),
            _const_spec(b_st.shape),
            _const_spec(w_out.shape),
            row(d), mod.spec(2, tm),
        ],
        out_specs=pl.BlockSpec((tm, d), lambda i: (i, 0)),
        scratch_shapes=[pltpu.VMEM((tm, width), F32), pltpu.VMEM((tm, width), BF16),
                        pltpu.VMEM((tm, width), BF16)],
        compiler_params=_cparams("parallel"),
        name="sgu",
    )(x, gpre, mod.mod, mod.mod, w_in, ln_g, ln_b, w_s, b_st, w_out, gpost, mod.mod)


def _swa_attn_kernel(q_ref, kc_ref, kp_ref, vc_ref, vp_ref, sink_ref, o_ref, *, nb, dh, scale):
    n = pl.program_id(0) % nb
    hq = q_ref.shape[1] // dh
    hk = kc_ref.shape[1] // dh
    grp = hq // hk
    blk = q_ref.shape[0]
    row = lax.broadcasted_iota(jnp.int32, (blk, blk), 0)
    col = lax.broadcasted_iota(jnp.int32, (blk, blk), 1)
    keep_c = col <= row
    keep_p = jnp.logical_and(col > row, n > 0)
    for kh in range(hk):
        ksl = slice(kh * dh, (kh + 1) * dh)
        kc, kp, vc, vp = kc_ref[:, ksl], kp_ref[:, ksl], vc_ref[:, ksl], vp_ref[:, ksl]
        for g in range(grp):
            qh = kh * grp + g
            qsl = slice(qh * dh, (qh + 1) * dh)
            qv = q_ref[:, qsl]
            sc = jnp.where(keep_c, _dot_nt(qv, kc) * scale, -jnp.inf)
            sp = jnp.where(keep_p, _dot_nt(qv, kp) * scale, -jnp.inf)
            sink = sink_ref[0:1, qh:qh + 1]
            m = jnp.maximum(jnp.maximum(jnp.max(sc, axis=-1, keepdims=True),
                                        jnp.max(sp, axis=-1, keepdims=True)), sink)
            pc = jnp.exp(sc - m)
            pp = jnp.exp(sp - m)
            den = (jnp.sum(pc, axis=-1, keepdims=True) + jnp.sum(pp, axis=-1, keepdims=True)
                   + jnp.exp(sink - m))
            o = _dot(pc.astype(BF16), vc) + _dot(pp.astype(BF16), vp)
            o_ref[:, qsl] = (o / den).astype(o_ref.dtype)


def _swa_attn(qkv, sinks_pad, batch, hq, hk, dh):
    t = qkv.shape[0]
    nb = t // batch // BLOCK
    grp = hq // hk
    prev = lambda r: jnp.maximum(r - 1, 0)
    return pl.pallas_call(
        functools.partial(_swa_attn_kernel, nb=nb, dh=dh, scale=dh ** -0.5),
        out_shape=jax.ShapeDtypeStruct((t, hq * dh), BF16),
        grid=(t // BLOCK,),
        in_specs=[
            pl.BlockSpec((BLOCK, hq * dh), lambda r: (r, 0)),
            pl.BlockSpec((BLOCK, hk * dh), lambda r: (r, grp)),
            pl.BlockSpec((BLOCK, hk * dh), lambda r: (prev(r), grp)),
            pl.BlockSpec((BLOCK, hk * dh), lambda r: (r, grp + 1)),
            pl.BlockSpec((BLOCK, hk * dh), lambda r: (prev(r), grp + 1)),
            pl.BlockSpec((1, LANES), lambda r: (0, 0)),
        ],
        out_specs=pl.BlockSpec((BLOCK, hq * dh), lambda r: (r, 0)),
        compiler_params=_cparams("parallel"),
        name="swa_attn",
    )(qkv, qkv, qkv, qkv, qkv, sinks_pad)


def _pad_lanes(a, n=LANES):
    return jnp.pad(a, [(0, 0)] * (a.ndim - 1) + [(0, n - a.shape[-1])])


def kernel(x, c, positions, ada_w, ada_b, mix_pre_g, mix_post_g, ffn_pre_g, ffn_post_g, ffn_w_gu, ffn_w_down, fox_w_in, fox_b_f, fox_w_out, sgu_w_in, sgu_ln_g, sgu_ln_b, sgu_w_s, sgu_b_s, sgu_w_out, swa_w_in, swa_sinks, swa_w_out):
    batch, seq, d = x.shape
    depth = ada_w.shape[0]
    t = batch * seq
    assert batch <= SUBLANES and seq % BLOCK == 0 and d % LANES == 0

    tm = _pick(seq, 512)
    xf = x.reshape(t, d)

    c_pad = jnp.pad(c, ((0, SUBLANES - batch), (0, 0)))
    mod_all = _adaln(c_pad, ada_w, ada_b).reshape(depth * SUBLANES * N_MOD, 1, d)

    for i in range(depth):
        kind, j = i % N_MIXERS, i // N_MIXERS
        mod = _Mod(mod_all, i, seq)
        gpre, gpost = mix_pre_g[i].reshape(1, d), mix_post_g[i].reshape(1, d)
        if kind == 0:
            heads = fox_b_f.shape[1]
            nqkv = fox_w_in.shape[2] - heads
            w = fox_w_in[j].astype(BF16)
            qkv, fg = _fox_proj(xf, gpre, mod, w[:, :nqkv], _pad_lanes(w[:, nqkv:]),
                                tm, _pick(nqkv, 1024))
            tq = _pick(seq, 512)
            cum, cum_t = _gate_cumsum(fg, _pad_lanes(fox_b_f[j].reshape(1, heads)), batch, tq)
            cum_rows = cum_t[:, :heads].reshape(batch, heads, seq // tq, 1, tq)
            a = _fox_attn(qkv, cum, cum_rows, batch, heads, tq)
            xf = _out_proj(a, fox_w_out[j].astype(BF16), xf, gpost, mod, tm)
        elif kind == 1:
            groups = sgu_w_s.shape[1]
            xf = _sgu(xf, gpre, gpost, mod, sgu_w_in[j].astype(BF16),
                      sgu_ln_g[j].reshape(1, -1), sgu_ln_b[j].reshape(1, -1),
                      sgu_w_s[j].astype(BF16), _pad_lanes(sgu_b_s[j].T),
                      sgu_w_out[j].astype(BF16), _pick(seq, 256))
        else:
            dh = SWA_HEAD_DIM
            hq = swa_sinks.shape[1]
            hk = (swa_w_in.shape[2] // dh - hq) // 2
            rope = dh // 4
            half = rope // 2
            inv = ROPE_THETA ** (-jnp.arange(0, rope, 2, dtype=F32) / rope)
            lane_d = jnp.arange(LANES) % dh
            inv_l = jnp.where(lane_d < rope, inv[lane_d % half], 0.0).reshape(1, LANES).astype(F32)
            m1 = jnp.where(lane_d < half, -1.0, 0.0).reshape(1, LANES).astype(F32)
            m2 = jnp.where((lane_d >= half) & (lane_d < rope), 1.0, 0.0).reshape(1, LANES).astype(F32)
            qkv = _swa_proj(xf, gpre, mod, swa_w_in[j].astype(BF16), positions.reshape(t, 1),
                            inv_l, m1, m2, tm, _pick(hk * dh, 512), (hq + hk) * dh, half)
            a = _swa_attn(qkv, _pad_lanes(swa_sinks[j].reshape(1, hq)), batch, hq, hk, dh)
            xf = _out_proj(a, swa_w_out[j].astype(BF16), xf, gpost, mod, tm)
        xf = _ffn(xf, ffn_pre_g[i].reshape(1, d), ffn_post_g[i].reshape(1, d), mod,
                  ffn_w_gu[i].astype(BF16), ffn_w_down[i].astype(BF16), tm,
                  _pick(ffn_w_down.shape[1], 512))
    return xf.reshape(batch, seq, d)
```

```python
import functools

import jax
import jax.numpy as jnp
from jax import lax
from jax.experimental import pallas as pl
from jax.experimental.pallas import tpu as pltpu

F32 = jnp.float32
BF16 = jnp.bfloat16

EPS = 1e-6
N_MIXERS = 3
BLOCK = 128
SWA_HEAD_DIM = 64
ROPE_THETA = 500000.0
LANES = 128
SUBLANES = 8
N_MOD = 6
LOG2E = 1.4426950408889634
FOX_CHAINS = 1
VMEM_LIMIT = 56 * 1024 * 1024


def _cparams(*sem):
    return pltpu.CompilerParams(dimension_semantics=sem, vmem_limit_bytes=VMEM_LIMIT)


def _const_spec(shape):
    nd = len(shape)
    return pl.BlockSpec(shape, lambda *_: (0,) * nd, pipeline_mode=pl.Buffered(1))


def _pick(n, pref):
    t = min(n, pref)
    while n % t:
        t -= LANES
    return t


def _rms(x):
    return x * lax.rsqrt(jnp.mean(x * x, axis=-1, keepdims=True) + EPS)


def _prenorm(x, g, sh, sc):
    return (_rms(x) * g) * (1.0 + sc) + sh


def _postnorm_residual(x, y, g, gate):
    return x + gate * (_rms(y) * g)


def _dot(a, b):
    return jnp.dot(a, b, preferred_element_type=F32)


def _dot_nt(a, b):
    return lax.dot_general(a, b, (((1,), (1,)), ((), ())), preferred_element_type=F32)


def _adaln_kernel(c_ref, w_ref, b_ref, o_ref):
    c = c_ref[...]
    ca = (c * jax.nn.sigmoid(c)).astype(BF16)
    o_ref[0] = _dot(ca, w_ref[0].astype(BF16)) + b_ref[0]


def _adaln(c_pad, ada_w, ada_b):
    depth, d, n = ada_w.shape
    tn = _pick(n, 1024)
    return pl.pallas_call(
        _adaln_kernel,
        out_shape=jax.ShapeDtypeStruct((depth, SUBLANES, n), F32),
        grid=(depth, n // tn),
        in_specs=[
            pl.BlockSpec((SUBLANES, d), lambda i, j: (0, 0)),
            pl.BlockSpec((1, d, tn), lambda i, j: (i, 0, j)),
            pl.BlockSpec((1, 1, tn), lambda i, j: (i, 0, j)),
        ],
        out_specs=pl.BlockSpec((1, SUBLANES, tn), lambda i, j: (i, 0, j)),
        compiler_params=_cparams("parallel", "parallel"),
        name="adaln",
    )(c_pad, ada_w, ada_b.reshape(depth, 1, n))


class _Mod:
    def __init__(self, mod, layer, seq):
        self.mod = mod
        self.layer = layer
        self.seq = seq
        self.d = mod.shape[-1]

    def spec(self, k, tm):
        layer, tpb = self.layer, self.seq // tm
        return pl.BlockSpec(
            (1, 1, self.d),
            lambda i, *_: ((layer * SUBLANES + i // tpb) * N_MOD + k, 0, 0))


def _proj_kernel(x_ref, g_ref, sh_ref, sc_ref, w_ref, o_ref, h_ref):
    @pl.when(pl.program_id(1) == 0)
    def _():
        h_ref[...] = _prenorm(x_ref[...], g_ref[...], sh_ref[0], sc_ref[0]).astype(BF16)

    o_ref[...] = _dot(h_ref[...], w_ref[...]).astype(o_ref.dtype)


def _fox_proj_kernel(x_ref, g_ref, sh_ref, sc_ref, w_ref, wf_ref, o_ref, fg_ref, h_ref):
    @pl.when(pl.program_id(1) == 0)
    def _():
        h = _prenorm(x_ref[...], g_ref[...], sh_ref[0], sc_ref[0]).astype(BF16)
        h_ref[...] = h
        fg_ref[...] = _dot(h, wf_ref[...])

    o_ref[...] = _dot(h_ref[...], w_ref[...]).astype(o_ref.dtype)


def _fox_proj(x, gain, mod, w, wf, tm, tn):
    t, d = x.shape
    n = w.shape[1]
    return pl.pallas_call(
        _fox_proj_kernel,
        out_shape=(jax.ShapeDtypeStruct((t, n), BF16), jax.ShapeDtypeStruct((t, LANES), F32)),
        grid=(t // tm, n // tn),
        in_specs=[
            pl.BlockSpec((tm, d), lambda i, j: (i, 0)),
            pl.BlockSpec((1, d), lambda i, j: (0, 0)),
            mod.spec(0, tm), mod.spec(1, tm),
            pl.BlockSpec((d, tn), lambda i, j: (0, j)),
            _const_spec((d, LANES)),
        ],
        out_specs=(pl.BlockSpec((tm, tn), lambda i, j: (i, j)),
                   pl.BlockSpec((tm, LANES), lambda i, j: (i, 0))),
        scratch_shapes=[pltpu.VMEM((tm, d), BF16)],
        compiler_params=_cparams("parallel", "arbitrary"),
        name="fox_proj",
    )(x, gain, mod.mod, mod.mod, w, wf)


def _swa_proj_kernel(x_ref, g_ref, sh_ref, sc_ref, w_ref, pos_ref, inv_ref, m1_ref, m2_ref,
                     o_ref, h_ref, cos_ref, s1_ref, s2_ref, *, n_rope_tiles, shift):
    j = pl.program_id(1)

    @pl.when(j == 0)
    def _():
        h_ref[...] = _prenorm(x_ref[...], g_ref[...], sh_ref[0], sc_ref[0]).astype(BF16)
        ang = pos_ref[...].astype(F32) * inv_ref[...]
        sn = jnp.sin(ang)
        cos_ref[...] = jnp.cos(ang)
        s1_ref[...] = sn * m1_ref[...]
        s2_ref[...] = sn * m2_ref[...]

    acc = _dot(h_ref[...], w_ref[...])
    tn = acc.shape[1]

    @pl.when(j < n_rope_tiles)
    def _():
        cs, s1, s2 = cos_ref[...], s1_ref[...], s2_ref[...]
        for c in range(tn // LANES):
            a = acc[:, c * LANES:(c + 1) * LANES]
            r = a * cs + pltpu.roll(a, LANES - shift, 1) * s1 + pltpu.roll(a, shift, 1) * s2
            o_ref[:, c * LANES:(c + 1) * LANES] = r.astype(o_ref.dtype)

    @pl.when(j >= n_rope_tiles)
    def _():
        o_ref[...] = acc.astype(o_ref.dtype)


def _swa_proj(x, gain, mod, w, pos, inv_l, m1, m2, tm, tn, n_rope_cols, shift):
    t, d = x.shape
    n = w.shape[1]
    row = lambda shape: pl.BlockSpec(shape, lambda i, j: (0, 0))
    return pl.pallas_call(
        functools.partial(_swa_proj_kernel, n_rope_tiles=n_rope_cols // tn, shift=shift),
        out_shape=jax.ShapeDtypeStruct((t, n), BF16),
        grid=(t // tm, n // tn),
        in_specs=[
            pl.BlockSpec((tm, d), lambda i, j: (i, 0)),
            row((1, d)),
            mod.spec(0, tm), mod.spec(1, tm),
            pl.BlockSpec((d, tn), lambda i, j: (0, j)),
            pl.BlockSpec((tm, 1), lambda i, j: (i, 0)),
            row((1, LANES)), row((1, LANES)), row((1, LANES)),
        ],
        out_specs=pl.BlockSpec((tm, tn), lambda i, j: (i, j)),
        scratch_shapes=[pltpu.VMEM((tm, d), BF16)] + [pltpu.VMEM((tm, LANES), F32)] * 3,
        compiler_params=_cparams("parallel", "arbitrary"),
        name="swa_proj",
    )(x, gain, mod.mod, mod.mod, w, pos, inv_l, m1, m2)


def _out_kernel(a_ref, w_ref, x_ref, g_ref, gate_ref, o_ref):
    y = _dot(a_ref[...], w_ref[...])
    o_ref[...] = _postnorm_residual(x_ref[...], y, g_ref[...], gate_ref[0])


def _out_proj(a, w, x, gain, mod, tm):
    t, k = a.shape
    d = w.shape[1]
    return pl.pallas_call(
        _out_kernel,
        out_shape=jax.ShapeDtypeStruct((t, d), F32),
        grid=(t // tm,),
        in_specs=[
            pl.BlockSpec((tm, k), lambda i: (i, 0)),
            _const_spec((k, d)),
            pl.BlockSpec((tm, d), lambda i: (i, 0)),
            pl.BlockSpec((1, d), lambda i: (0, 0)),
            mod.spec(2, tm),
        ],
        out_specs=pl.BlockSpec((tm, d), lambda i: (i, 0)),
        compiler_params=_cparams("parallel"),
        name="out_proj",
    )(a, w, x, gain, mod.mod)


def _ffn_kernel(x_ref, gpre_ref, sh_ref, sc_ref, wg_ref, wu_ref, wd_ref, gpost_ref, gate_ref,
                o_ref, h_ref, acc_ref):
    j = pl.program_id(1)

    @pl.when(j == 0)
    def _():
        h_ref[...] = _prenorm(x_ref[...], gpre_ref[...], sh_ref[0], sc_ref[0]).astype(BF16)
        acc_ref[...] = jnp.zeros_like(acc_ref)

    h = h_ref[...]
    g = _dot(h, wg_ref[...])
    u = _dot(h, wu_ref[...])
    a = (g * jax.nn.sigmoid(g) * u).astype(BF16)
    acc_ref[...] += _dot(a, wd_ref[...])

    @pl.when(j == pl.num_programs(1) - 1)
    def _():
        o_ref[...] = _postnorm_residual(x_ref[...], acc_ref[...], gpost_ref[...], gate_ref[0])


def _ffn(x, gpre, gpost, mod, w_gu, w_down, tm, tf):
    t, d = x.shape
    dff = w_down.shape[0]
    nf = dff // tf
    row = lambda: pl.BlockSpec((1, d), lambda i, j: (0, 0))
    return pl.pallas_call(
        _ffn_kernel,
        out_shape=jax.ShapeDtypeStruct((t, d), F32),
        grid=(t // tm, nf),
        in_specs=[
            pl.BlockSpec((tm, d), lambda i, j: (i, 0)),
            row(), mod.spec(3, tm), mod.spec(4, tm),
            pl.BlockSpec((d, tf), lambda i, j: (0, j)),
            pl.BlockSpec((d, tf), lambda i, j: (0, nf + j)),
            pl.BlockSpec((tf, d), lambda i, j: (j, 0)),
            row(), mod.spec(5, tm),
        ],
        out_specs=pl.BlockSpec((tm, d), lambda i, j: (i, 0)),
        scratch_shapes=[pltpu.VMEM((tm, d), BF16), pltpu.VMEM((tm, d), F32)],
        compiler_params=_cparams("parallel", "arbitrary"),
        name="ffn",
    )(x, gpre, mod.mod, mod.mod, w_gu, w_gu, w_down, gpost, mod.mod)


def _gate_kernel(fg_ref, b_ref, cum_ref, carry_ref):
    @pl.when(pl.program_id(1) == 0)
    def _():
        carry_ref[...] = jnp.zeros_like(carry_ref)

    z = fg_ref[...] + b_ref[...]
    lf = jnp.minimum(z, 0.0) - jnp.log1p(jnp.exp(-jnp.abs(z)))
    n = lf.shape[0]
    tril = (lax.broadcasted_iota(jnp.int32, (n, n), 1)
            <= lax.broadcasted_iota(jnp.int32, (n, n), 0)).astype(BF16)
    hi = lf.astype(BF16)
    r1 = lf - hi.astype(F32)
    mid = r1.astype(BF16)
    lo = (r1 - mid.astype(F32)).astype(BF16)
    cum = (_dot(tril, hi) + _dot(tril, mid)) + _dot(tril, lo) + carry_ref[...]
    cum_ref[...] = cum
    carry_ref[...] = cum[n - 1:n, :]


def _gate_cumsum(fg, b_pad, batch, tg):
    t = fg.shape[0]
    s = t // batch
    nt = s // tg
    return pl.pallas_call(
        _gate_kernel,
        out_shape=jax.ShapeDtypeStruct((t, LANES), F32),
        grid=(batch, nt),
        in_specs=[
            pl.BlockSpec((tg, LANES), lambda b, j: (b * nt + j, 0)),
            pl.BlockSpec((1, LANES), lambda b, j: (0, 0)),
        ],
        out_specs=pl.BlockSpec((tg, LANES), lambda b, j: (b * nt + j, 0)),
        scratch_shapes=[pltpu.VMEM((1, LANES), F32)],
        compiler_params=_cparams("parallel", "arbitrary"),
        name="fox_gate",
    )(fg, b_pad)


def _lane_col(x, h):
    lane = lax.broadcasted_iota(jnp.int32, x.shape, 1)
    return jnp.sum(jnp.where(lane == h, x, 0.0), axis=-1, keepdims=True)


def _decay_lanes(f, base):
    hi = f.astype(BF16).astype(F32)
    r = f - hi
    mid = r.astype(BF16).astype(F32)
    lo = (r - mid).astype(BF16).astype(F32)
    lane = lax.broadcasted_iota(jnp.int32, (f.shape[0], LANES), 1)
    ones = jnp.logical_and(lane >= 3 - base, lane < 6 - base)
    out = jnp.where(lane == base, hi, jnp.where(lane == base + 1, mid, jnp.where(
        lane == base + 2, lo, jnp.where(ones, 1.0, 0.0))))
    return out.astype(BF16)


def _fox_attn_kernel(q_ref, k_ref, v_ref, cum_ref, o_ref, ka_ref, vt_ref, *, tk, qscale, chains):
    h = pl.program_id(1)
    i = pl.program_id(2)
    tq, dh = q_ref.shape
    nkv = vt_ref.shape[0]
    w = tq // chains

    @pl.when(i == 0)
    def _():
        for c in range(nkv):
            rows = slice(c * tk, (c + 1) * tk)
            ka_ref[rows, :dh] = k_ref[rows, :]
            ka_ref[rows, dh:] = _decay_lanes(_lane_col(cum_ref[rows, :], h) * (-LOG2E), 0)
            vt_ref[c] = v_ref[rows, :].T

    q = (q_ref[...].astype(F32) * qscale).astype(BF16)
    fq = _lane_col(cum_ref[pl.ds(pl.multiple_of(i * tq, tq), tq), :], h) * LOG2E
    qa = jnp.concatenate([q, _decay_lanes(fq, 3)], axis=1)
    qas = [qa[r * w:(r + 1) * w] for r in range(chains)]

    def step(c, carry, diagonal):
        ka = ka_ref[pl.ds(pl.multiple_of(c * tk, tk), tk), :]
        vt = vt_ref[c]
        out = []
        for r in range(chains):
            m, l, acc = carry[r]
            st = _dot_nt(ka, qas[r])
            if diagonal:
                keep = (lax.broadcasted_iota(jnp.int32, st.shape, 0)
                        <= lax.broadcasted_iota(jnp.int32, st.shape, 1) + r * w)
                st = jnp.where(keep, st, -jnp.inf)
            m_new = jnp.maximum(m, jnp.max(st, axis=0, keepdims=True))
            alpha = jnp.exp2(m - m_new)
            p = jnp.exp2(st - m_new)
            l = alpha * l + jnp.sum(p, axis=0, keepdims=True)
            acc = alpha * acc + _dot(vt, p.astype(BF16))
            out.append((m_new, l, acc))
        return tuple(out)

    init = tuple((jnp.full((1, w), -jnp.inf, F32), jnp.zeros((1, w), F32), jnp.zeros((dh, w), F32))
                 for _ in range(chains))
    carry = lax.fori_loop(0, i, lambda c, car: step(c, car, False), init)
    carry = step(i, carry, True)
    for r in range(chains):
        _, l, acc = carry[r]
        o_ref[r * w:(r + 1) * w, :] = (acc / l).T.astype(o_ref.dtype)


def _fox_attn_pipe_kernel(q_ref, k_ref, v_ref, cum_ref, o_ref, ka_ref, vt_ref, qa_ref,
                          sa_ref, sb_ref, pa_ref, pb_ref, m_ref, l_ref, al_ref, acc_ref,
                          *, tk, qscale):
    h = pl.program_id(1)
    i = pl.program_id(2)
    tq, dh = q_ref.shape
    nkv = vt_ref.shape[0]

    @pl.when(i == 0)
    def _():
        for c in range(nkv):
            rows = slice(c * tk, (c + 1) * tk)
            ka_ref[rows, :dh] = k_ref[rows, :]
            ka_ref[rows, dh:] = _decay_lanes(_lane_col(cum_ref[rows, :], h) * (-LOG2E), 0)
            vt_ref[c] = v_ref[rows, :].T

    fq = _lane_col(cum_ref[pl.ds(pl.multiple_of(i * tq, tq), tq), :], h) * LOG2E
    qa_ref[:, :dh] = (q_ref[...].astype(F32) * qscale).astype(BF16)
    qa_ref[:, dh:] = _decay_lanes(fq, 3)
    m_ref[...] = jnp.full(m_ref.shape, -jnp.inf, F32)
    l_ref[...] = jnp.zeros(l_ref.shape, F32)
    al_ref[...] = jnp.ones(al_ref.shape, F32)
    acc_ref[...] = jnp.zeros(acc_ref.shape, F32)
    pb_ref[...] = jnp.zeros(pb_ref.shape, BF16)

    def qk(c, s_out):
        ka = ka_ref[pl.ds(pl.multiple_of(c * tk, tk), tk), :]
        s_out[...] = _dot_nt(ka, qa_ref[...])

    def pv(c, p_in):
        acc_ref[...] = al_ref[...] * acc_ref[...] + _dot(vt_ref[c], p_in[...])

    def softmax(s_in, p_out, key_offset):
        st = s_in[...]
        if key_offset is not None:
            keep = (lax.broadcasted_iota(jnp.int32, st.shape, 0) + key_offset
                    <= lax.broadcasted_iota(jnp.int32, st.shape, 1))
            st = jnp.where(keep, st, -jnp.inf)
        m = m_ref[...]
        m_new = jnp.maximum(m, jnp.max(st, axis=0, keepdims=True))
        alpha = jnp.exp2(m - m_new)
        p = jnp.exp2(st - m_new)
        l_ref[...] = alpha * l_ref[...] + jnp.sum(p, axis=0, keepdims=True)
        m_ref[...] = m_new
        al_ref[...] = alpha
        p_out[...] = p.astype(BF16)

    def pair(c0, diagonal):
        qk(c0 + 1, sb_ref)
        pv(jnp.maximum(c0 - 1, 0), pb_ref)
        softmax(sa_ref, pa_ref, 0 if diagonal else None)
        if not diagonal:
            qk(c0 + 2, sa_ref)
        pv(c0, pa_ref)
        softmax(sb_ref, pb_ref, tk if diagonal else None)

    qk(0, sa_ref)

    def body(k, carry):
        pair(2 * k, False)
        return carry

    lax.fori_loop(0, i, body, 0)
    pair(2 * i, True)
    pv(2 * i + 1, pb_ref)
    o_ref[...] = (acc_ref[...] / l_ref[...]).T.astype(o_ref.dtype)


def _fox_attn(qkv, cum, batch, heads, tq):
    t = qkv.shape[0]
    dh = qkv.shape[1] // (3 * heads)
    s = t // batch
    nq = s // tq
    tk = tq // 2
    return pl.pallas_call(
        functools.partial(_fox_attn_pipe_kernel, tk=tk, qscale=dh ** -0.5 * LOG2E),
        out_shape=jax.ShapeDtypeStruct((t, heads * dh), BF16),
        grid=(batch, heads, nq),
        in_specs=[
            pl.BlockSpec((tq, dh), lambda b, h, i: (b * nq + i, h)),
            pl.BlockSpec((s, dh), lambda b, h, i: (b, heads + h)),
            pl.BlockSpec((s, dh), lambda b, h, i: (b, 2 * heads + h)),
            pl.BlockSpec((s, LANES), lambda b, h, i: (b, 0)),
        ],
        out_specs=pl.BlockSpec((tq, dh), lambda b, h, i: (b * nq + i, h)),
        scratch_shapes=[
            pltpu.VMEM((s, 2 * dh), BF16),
            pltpu.VMEM((s // tk, dh, tk), BF16),
            pltpu.VMEM((tq, 2 * dh), BF16),
            pltpu.VMEM((tk, tq), F32), pltpu.VMEM((tk, tq), F32),
            pltpu.VMEM((tk, tq), BF16), pltpu.VMEM((tk, tq), BF16),
            pltpu.VMEM((1, tq), F32), pltpu.VMEM((1, tq), F32), pltpu.VMEM((1, tq), F32),
            pltpu.VMEM((dh, tq), F32),
        ],
        compiler_params=_cparams("parallel", "parallel", "arbitrary"),
        name="fox_attn",
    )(qkv, qkv, qkv, cum)


def _sgu_kernel(x_ref, gpre_ref, sh_ref, sc_ref, win_ref, lng_ref, lnb_ref, ws_ref, bs_ref,
                wout_ref, gpost_ref, gate_ref, o_ref, u_ref, vn_ref, gated_ref):
    x = x_ref[...]
    tm = x.shape[0]
    width = u_ref.shape[1]
    groups, chunk, _ = ws_ref.shape
    gd = width // groups
    h = _prenorm(x, gpre_ref[...], sh_ref[0], sc_ref[0]).astype(BF16)
    u_ref[...] = jax.nn.gelu(_dot(h, win_ref[:, :width]))
    v = jax.nn.gelu(_dot(h, win_ref[:, width:]))
    mu = jnp.mean(v, axis=-1, keepdims=True)
    vc = v - mu
    var = jnp.mean(vc * vc, axis=-1, keepdims=True)
    vn_ref[...] = (vc * lax.rsqrt(var + EPS) * lng_ref[...] + lnb_ref[...]).astype(BF16)
    causal = (lax.broadcasted_iota(jnp.int32, (chunk, chunk), 1)
              <= lax.broadcasted_iota(jnp.int32, (chunk, chunk), 0))
    for g in range(groups):
        wg = jnp.where(causal, ws_ref[g], jnp.zeros((), BF16))
        bias = bs_ref[:, g:g + 1]
        cols = slice(g * gd, (g + 1) * gd)
        for c in range(tm // chunk):
            rows = slice(c * chunk, (c + 1) * chunk)
            f = _dot(wg, vn_ref[rows, cols]) + bias
            gated_ref[rows, cols] = (u_ref[rows, cols] * f).astype(BF16)
    y = _dot(gated_ref[...], wout_ref[...])
    o_ref[...] = _postnorm_residual(x, y, gpost_ref[...], gate_ref[0])


def _sgu(x, gpre, gpost, mod, w_in, ln_g, ln_b, w_s, b_st, w_out, tm):
    t, d = x.shape
    width = w_out.shape[0]
    row = lambda n: pl.BlockSpec((1, n), lambda i: (0, 0))
    return pl.pallas_call(
        _sgu_kernel,
        out_shape=jax.ShapeDtypeStruct((t, d), F32),
        grid=(t // tm,),
        in_specs=[
            pl.BlockSpec((tm, d), lambda i: (i, 0)),
            row(d), mod.spec(0, tm), mod.spec(1, tm),
            _const_spec(w_in.shape),
            row(width), row(width),
            _const_spec(w_s.shape),
            _const_spec(b_st.shape),
            _const_spec(w_out.shape),
            row(d), mod.spec(2, tm),
        ],
        out_specs=pl.BlockSpec((tm, d), lambda i: (i, 0)),
        scratch_shapes=[pltpu.VMEM((tm, width), F32), pltpu.VMEM((tm, width), BF16),
                        pltpu.VMEM((tm, width), BF16)],
        compiler_params=_cparams("parallel"),
        name="sgu",
    )(x, gpre, mod.mod, mod.mod, w_in, ln_g, ln_b, w_s, b_st, w_out, gpost, mod.mod)


def _swa_attn_kernel(sink_ref, q_ref, kc_ref, kp_ref, vc_ref, vp_ref, o_ref, *, nb, grp, scale):
    n = pl.program_id(0) % nb
    blk = q_ref.shape[0]
    half = LANES // 2
    kv_cols = kc_ref.shape[1] // LANES
    npair = grp // 2
    row = lax.broadcasted_iota(jnp.int32, (blk, blk), 0)
    col = lax.broadcasted_iota(jnp.int32, (blk, blk), 1)
    keep_p = jnp.logical_and(row > col, n > 0)
    keep_c = row <= col
    keep = jnp.concatenate([keep_p, keep_c], axis=0)
    keep = jnp.concatenate([keep] * npair, axis=1)
    lo = lax.broadcasted_iota(jnp.int32, (2 * blk, LANES), 1) < half
    scores, vts = {}, {}
    for j in range(kv_cols):
        csl = slice(j * LANES, (j + 1) * LANES)
        kf = jnp.concatenate([kp_ref[:, csl], kc_ref[:, csl]], axis=0).astype(F32) * scale
        kr = pltpu.roll(kf, half, 1)
        vt = jnp.concatenate([vp_ref[:, csl], vc_ref[:, csl]], axis=0).T
        for e in range(2):
            kh = 2 * j + e
            qcols = [q_ref[:, (kh * npair + a) * LANES:(kh * npair + a + 1) * LANES]
                     for a in range(npair)]
            rhs = jnp.concatenate(qcols, axis=0)
            vts[kh] = vt[e * half:(e + 1) * half, :]
            for p in range(2):
                src = kf if p == e else kr
                kz = jnp.where(lo if p == 0 else jnp.logical_not(lo), src, 0.0).astype(BF16)
                scores[kh, p] = _dot_nt(kz, rhs)
    probs = {}
    for (kh, p), st in scores.items():
        st = jnp.where(keep, st, -jnp.inf)
        sink = jnp.concatenate(
            [jnp.full((1, blk), sink_ref[kh * grp + 2 * a + p], F32) for a in range(npair)], axis=1)
        m = jnp.maximum(jnp.max(st, axis=0, keepdims=True), sink)
        pt = jnp.exp(st - m)
        den = jnp.sum(pt, axis=0, keepdims=True) + jnp.exp(sink - m)
        probs[kh, p] = (pt.astype(BF16), den)
    for kh in range(2 * kv_cols):
        outs = [_dot(vts[kh], probs[kh, p][0]) / probs[kh, p][1] for p in range(2)]
        for a in range(npair):
            ot = jnp.concatenate([o[:, a * blk:(a + 1) * blk] for o in outs], axis=0)
            c0 = (kh * npair + a) * LANES
            o_ref[:, c0:c0 + LANES] = ot.T.astype(o_ref.dtype)


def _swa_attn(qkv, sinks, batch, hq, hk, dh):
    t = qkv.shape[0]
    nb = t // batch // BLOCK
    grp = hq // hk
    assert 2 * dh == LANES and grp % 2 == 0 and hk % 2 == 0
    prev = lambda r: jnp.maximum(r - 1, 0)
    return pl.pallas_call(
        functools.partial(_swa_attn_kernel, nb=nb, grp=grp, scale=dh ** -0.5),
        out_shape=jax.ShapeDtypeStruct((t, hq * dh), BF16),
        grid=(t // BLOCK,),
        in_specs=[
            pl.BlockSpec(memory_space=pltpu.SMEM),
            pl.BlockSpec((BLOCK, hq * dh), lambda r: (r, 0)),
            pl.BlockSpec((BLOCK, hk * dh), lambda r: (r, grp)),
            pl.BlockSpec((BLOCK, hk * dh), lambda r: (prev(r), grp)),
            pl.BlockSpec((BLOCK, hk * dh), lambda r: (r, grp + 1)),
            pl.BlockSpec((BLOCK, hk * dh), lambda r: (prev(r), grp + 1)),
        ],
        out_specs=pl.BlockSpec((BLOCK, hq * dh), lambda r: (r, 0)),
        compiler_params=_cparams("parallel"),
        name="swa_attn",
    )(sinks, qkv, qkv, qkv, qkv, qkv)


def _pad_lanes(a, n=LANES):
    return jnp.pad(a, [(0, 0)] * (a.ndim - 1) + [(0, n - a.shape[-1])])


def kernel(x, c, positions, ada_w, ada_b, mix_pre_g, mix_post_g, ffn_pre_g, ffn_post_g, ffn_w_gu, ffn_w_down, fox_w_in, fox_b_f, fox_w_out, sgu_w_in, sgu_ln_g, sgu_ln_b, sgu_w_s, sgu_b_s, sgu_w_out, swa_w_in, swa_sinks, swa_w_out):
    batch, seq, d = x.shape
    depth = ada_w.shape[0]
    t = batch * seq
    assert batch <= SUBLANES and seq % BLOCK == 0 and d % LANES == 0

    tm = _pick(seq, 512)
    xf = x.reshape(t, d)

    c_pad = jnp.pad(c, ((0, SUBLANES - batch), (0, 0)))
    mod_all = _adaln(c_pad, ada_w, ada_b).reshape(depth * SUBLANES * N_MOD, 1, d)

    for i in range(depth):
        kind, j = i % N_MIXERS, i // N_MIXERS
        mod = _Mod(mod_all, i, seq)
        gpre, gpost = mix_pre_g[i].reshape(1, d), mix_post_g[i].reshape(1, d)
        if kind == 0:
            heads = fox_b_f.shape[1]
            nqkv = fox_w_in.shape[2] - heads
            w = fox_w_in[j].astype(BF16)
            qkv, fg = _fox_proj(xf, gpre, mod, w[:, :nqkv], _pad_lanes(w[:, nqkv:]),
                                tm, _pick(nqkv, 1024))
            cum = _gate_cumsum(fg, _pad_lanes(fox_b_f[j].reshape(1, heads)), batch, _pick(seq, 512))
            a = _fox_attn(qkv, cum, batch, heads, _pick(seq, 1024))
            xf = _out_proj(a, fox_w_out[j].astype(BF16), xf, gpost, mod, tm)
        elif kind == 1:
            groups = sgu_w_s.shape[1]
            xf = _sgu(xf, gpre, gpost, mod, sgu_w_in[j].astype(BF16),
                      sgu_ln_g[j].reshape(1, -1), sgu_ln_b[j].reshape(1, -1),
                      sgu_w_s[j].astype(BF16), _pad_lanes(sgu_b_s[j].T),
                      sgu_w_out[j].astype(BF16), _pick(seq, 256))
        else:
            dh = SWA_HEAD_DIM
            hq = swa_sinks.shape[1]
            hk = (swa_w_in.shape[2] // dh - hq) // 2
            rope = dh // 4
            half = rope // 2
            inv = ROPE_THETA ** (-jnp.arange(0, rope, 2, dtype=F32) / rope)
            lane_d = jnp.arange(LANES) % dh
            inv_l = jnp.where(lane_d < rope, inv[lane_d % half], 0.0).reshape(1, LANES).astype(F32)
            m1 = jnp.where(lane_d < half, -1.0, 0.0).reshape(1, LANES).astype(F32)
            m2 = jnp.where((lane_d >= half) & (lane_d < rope), 1.0, 0.0).reshape(1, LANES).astype(F32)
            qkv = _swa_proj(xf, gpre, mod, swa_w_in[j].astype(BF16), positions.reshape(t, 1),
                            inv_l, m1, m2, tm, _pick(hk * dh, 512), (hq + hk) * dh, half)
            a = _swa_attn(qkv, swa_sinks[j], batch, hq, hk, dh)
            xf = _out_proj(a, swa_w_out[j].astype(BF16), xf, gpost, mod, tm)
        xf = _ffn(xf, ffn_pre_g[i].reshape(1, d), ffn_post_g[i].reshape(1, d), mod,
                  ffn_w_gu[i].astype(BF16), ffn_w_down[i].astype(BF16), tm,
                  _pick(ffn_w_down.shape[1], 512))
    return xf.reshape(batch, seq, d)
```

```python
import functools

import jax
import jax.numpy as jnp
from jax import lax
from jax.experimental import pallas as pl
from jax.experimental.pallas import tpu as pltpu

F32 = jnp.float32
BF16 = jnp.bfloat16

EPS = 1e-6
N_MIXERS = 3
BLOCK = 128
SWA_HEAD_DIM = 64
ROPE_THETA = 500000.0
LANES = 128
SUBLANES = 8
N_MOD = 6
LOG2E = 1.4426950408889634
SKIP_LOG2 = 160.0
VMEM_LIMIT = 56 * 1024 * 1024


def _cparams(*sem):
    return pltpu.CompilerParams(dimension_semantics=sem, vmem_limit_bytes=VMEM_LIMIT)


def _const_spec(shape):
    nd = len(shape)
    return pl.BlockSpec(shape, lambda *_: (0,) * nd, pipeline_mode=pl.Buffered(1))


def _pick(n, pref):
    t = min(n, pref)
    while n % t:
        t -= LANES
    return t


def _rms(x):
    return x * lax.rsqrt(jnp.mean(x * x, axis=-1, keepdims=True) + EPS)


def _prenorm(x, g, sh, sc):
    return (_rms(x) * g) * (1.0 + sc) + sh


def _postnorm_residual(x, y, g, gate):
    return x + gate * (_rms(y) * g)


def _dot(a, b):
    return jnp.dot(a, b, preferred_element_type=F32)


def _dot_nt(a, b):
    return lax.dot_general(a, b, (((1,), (1,)), ((), ())), preferred_element_type=F32)


def _adaln_kernel(c_ref, w_ref, b_ref, o_ref):
    c = c_ref[...]
    ca = (c * jax.nn.sigmoid(c)).astype(BF16)
    o_ref[0] = _dot(ca, w_ref[0].astype(BF16)) + b_ref[0]


def _adaln(c_pad, ada_w, ada_b):
    depth, d, n = ada_w.shape
    tn = _pick(n, 1024)
    return pl.pallas_call(
        _adaln_kernel,
        out_shape=jax.ShapeDtypeStruct((depth, SUBLANES, n), F32),
        grid=(depth, n // tn),
        in_specs=[
            pl.BlockSpec((SUBLANES, d), lambda i, j: (0, 0)),
            pl.BlockSpec((1, d, tn), lambda i, j: (i, 0, j)),
            pl.BlockSpec((1, 1, tn), lambda i, j: (i, 0, j)),
        ],
        out_specs=pl.BlockSpec((1, SUBLANES, tn), lambda i, j: (i, 0, j)),
        compiler_params=_cparams("parallel", "parallel"),
        name="adaln",
    )(c_pad, ada_w, ada_b.reshape(depth, 1, n))


class _Mod:
    def __init__(self, mod, layer, seq):
        self.mod = mod
        self.layer = layer
        self.seq = seq
        self.d = mod.shape[-1]

    def spec(self, k, tm):
        layer, tpb = self.layer, self.seq // tm
        return pl.BlockSpec(
            (1, 1, self.d),
            lambda i, *_: ((layer * SUBLANES + i // tpb) * N_MOD + k, 0, 0))


def _fox_proj_kernel(x_ref, g_ref, sh_ref, sc_ref, w_ref, wf_ref, o_ref, fg_ref, h_ref):
    @pl.when(pl.program_id(1) == 0)
    def _():
        h = _prenorm(x_ref[...], g_ref[...], sh_ref[0], sc_ref[0]).astype(BF16)
        h_ref[...] = h
        fg_ref[...] = _dot(h, wf_ref[...])

    o_ref[...] = _dot(h_ref[...], w_ref[...]).astype(o_ref.dtype)


def _fox_proj(x, gain, mod, w, wf, n, tm, tn):
    t, d = x.shape
    return pl.pallas_call(
        _fox_proj_kernel,
        out_shape=(jax.ShapeDtypeStruct((t, n), BF16), jax.ShapeDtypeStruct((t, LANES), F32)),
        grid=(t // tm, n // tn),
        in_specs=[
            pl.BlockSpec((tm, d), lambda i, j: (i, 0)),
            pl.BlockSpec((1, d), lambda i, j: (0, 0)),
            mod.spec(0, tm), mod.spec(1, tm),
            pl.BlockSpec((d, tn), lambda i, j: (0, j)),
            _const_spec((d, LANES)),
        ],
        out_specs=(pl.BlockSpec((tm, tn), lambda i, j: (i, j)),
                   pl.BlockSpec((tm, LANES), lambda i, j: (i, 0))),
        scratch_shapes=[pltpu.VMEM((tm, d), BF16)],
        compiler_params=_cparams("parallel", "arbitrary"),
        name="fox_proj",
    )(x, gain, mod.mod, mod.mod, w, wf)


def _swa_proj_kernel(x_ref, g_ref, sh_ref, sc_ref, w_ref, pos_ref, inv_ref, m1_ref, m2_ref,
                     o_ref, h_ref, cos_ref, s1_ref, s2_ref, *, n_rope_tiles, shift):
    j = pl.program_id(1)

    @pl.when(j == 0)
    def _():
        h_ref[...] = _prenorm(x_ref[...], g_ref[...], sh_ref[0], sc_ref[0]).astype(BF16)
        ang = pos_ref[...].astype(F32) * inv_ref[...]
        sn = jnp.sin(ang)
        cos_ref[...] = jnp.cos(ang)
        s1_ref[...] = sn * m1_ref[...]
        s2_ref[...] = sn * m2_ref[...]

    acc = _dot(h_ref[...], w_ref[...])
    tn = acc.shape[1]

    @pl.when(j < n_rope_tiles)
    def _():
        cs, s1, s2 = cos_ref[...], s1_ref[...], s2_ref[...]
        for c in range(tn // LANES):
            a = acc[:, c * LANES:(c + 1) * LANES]
            r = a * cs + pltpu.roll(a, LANES - shift, 1) * s1 + pltpu.roll(a, shift, 1) * s2
            o_ref[:, c * LANES:(c + 1) * LANES] = r.astype(o_ref.dtype)

    @pl.when(j >= n_rope_tiles)
    def _():
        o_ref[...] = acc.astype(o_ref.dtype)


def _swa_proj(x, gain, mod, w, pos, inv_l, m1, m2, tm, tn, n_rope_cols, shift):
    t, d = x.shape
    n = w.shape[1]
    row = lambda shape: pl.BlockSpec(shape, lambda i, j: (0, 0))
    return pl.pallas_call(
        functools.partial(_swa_proj_kernel, n_rope_tiles=n_rope_cols // tn, shift=shift),
        out_shape=jax.ShapeDtypeStruct((t, n), BF16),
        grid=(t // tm, n // tn),
        in_specs=[
            pl.BlockSpec((tm, d), lambda i, j: (i, 0)),
            row((1, d)),
            mod.spec(0, tm), mod.spec(1, tm),
            pl.BlockSpec((d, tn), lambda i, j: (0, j)),
            pl.BlockSpec((tm, 1), lambda i, j: (i, 0)),
            row((1, LANES)), row((1, LANES)), row((1, LANES)),
        ],
        out_specs=pl.BlockSpec((tm, tn), lambda i, j: (i, j)),
        scratch_shapes=[pltpu.VMEM((tm, d), BF16)] + [pltpu.VMEM((tm, LANES), F32)] * 3,
        compiler_params=_cparams("parallel", "arbitrary"),
        name="swa_proj",
    )(x, gain, mod.mod, mod.mod, w, pos, inv_l, m1, m2)


def _out_kernel(a_ref, w_ref, x_ref, g_ref, gate_ref, o_ref):
    y = _dot(a_ref[...], w_ref[...])
    o_ref[...] = _postnorm_residual(x_ref[...], y, g_ref[...], gate_ref[0])


def _out_proj(a, w, x, gain, mod, tm):
    t, k = a.shape
    d = w.shape[1]
    return pl.pallas_call(
        _out_kernel,
        out_shape=jax.ShapeDtypeStruct((t, d), F32),
        grid=(t // tm,),
        in_specs=[
            pl.BlockSpec((tm, k), lambda i: (i, 0)),
            _const_spec((k, d)),
            pl.BlockSpec((tm, d), lambda i: (i, 0)),
            pl.BlockSpec((1, d), lambda i: (0, 0)),
            mod.spec(2, tm),
        ],
        out_specs=pl.BlockSpec((tm, d), lambda i: (i, 0)),
        compiler_params=_cparams("parallel"),
        name="out_proj",
    )(a, w, x, gain, mod.mod)


def _ffn_kernel(x_ref, gpre_ref, sh_ref, sc_ref, wg_ref, wu_ref, wd_ref, gpost_ref, gate_ref,
                o_ref, h_ref, acc_ref):
    j = pl.program_id(1)

    @pl.when(j == 0)
    def _():
        h_ref[...] = _prenorm(x_ref[...], gpre_ref[...], sh_ref[0], sc_ref[0]).astype(BF16)
        acc_ref[...] = jnp.zeros_like(acc_ref)

    h = h_ref[...]
    g = _dot(h, wg_ref[...])
    u = _dot(h, wu_ref[...])
    a = (g * jax.nn.sigmoid(g) * u).astype(BF16)
    acc_ref[...] += _dot(a, wd_ref[...])

    @pl.when(j == pl.num_programs(1) - 1)
    def _():
        o_ref[...] = _postnorm_residual(x_ref[...], acc_ref[...], gpost_ref[...], gate_ref[0])


def _ffn(x, gpre, gpost, mod, w_gu, w_down, tm, tf):
    t, d = x.shape
    dff = w_down.shape[0]
    nf = dff // tf
    row = lambda: pl.BlockSpec((1, d), lambda i, j: (0, 0))
    return pl.pallas_call(
        _ffn_kernel,
        out_shape=jax.ShapeDtypeStruct((t, d), F32),
        grid=(t // tm, nf),
        in_specs=[
            pl.BlockSpec((tm, d), lambda i, j: (i, 0)),
            row(), mod.spec(3, tm), mod.spec(4, tm),
            pl.BlockSpec((d, tf), lambda i, j: (0, j)),
            pl.BlockSpec((d, tf), lambda i, j: (0, nf + j)),
            pl.BlockSpec((tf, d), lambda i, j: (j, 0)),
            row(), mod.spec(5, tm),
        ],
        out_specs=pl.BlockSpec((tm, d), lambda i, j: (i, 0)),
        scratch_shapes=[pltpu.VMEM((tm, d), BF16), pltpu.VMEM((tm, d), F32)],
        compiler_params=_cparams("parallel", "arbitrary"),
        name="ffn",
    )(x, gpre, mod.mod, mod.mod, w_gu, w_gu, w_down, gpost, mod.mod)


def _gate_kernel(fg_ref, b_ref, cum_ref, carry_ref):
    @pl.when(pl.program_id(1) == 0)
    def _():
        carry_ref[...] = jnp.zeros_like(carry_ref)

    z = fg_ref[...] + b_ref[...]
    lf = jnp.minimum(z, 0.0) - jnp.log1p(jnp.exp(-jnp.abs(z)))
    n = lf.shape[0]
    tril = (lax.broadcasted_iota(jnp.int32, (n, n), 1)
            <= lax.broadcasted_iota(jnp.int32, (n, n), 0)).astype(BF16)
    hi = lf.astype(BF16)
    r1 = lf - hi.astype(F32)
    mid = r1.astype(BF16)
    lo = (r1 - mid.astype(F32)).astype(BF16)
    cum = (_dot(tril, hi) + _dot(tril, mid)) + _dot(tril, lo) + carry_ref[...]
    cum_ref[...] = cum
    carry_ref[...] = cum[n - 1:n, :]


def _gate_cumsum(fg, b_pad, batch, tg):
    t = fg.shape[0]
    s = t // batch
    nt = s // tg
    return pl.pallas_call(
        _gate_kernel,
        out_shape=jax.ShapeDtypeStruct((t, LANES), F32),
        grid=(batch, nt),
        in_specs=[
            pl.BlockSpec((tg, LANES), lambda b, j: (b * nt + j, 0)),
            pl.BlockSpec((1, LANES), lambda b, j: (0, 0)),
        ],
        out_specs=pl.BlockSpec((tg, LANES), lambda b, j: (b * nt + j, 0)),
        scratch_shapes=[pltpu.VMEM((1, LANES), F32)],
        compiler_params=_cparams("parallel", "arbitrary"),
        name="fox_gate",
    )(fg, b_pad)


def _lane_col(x, h):
    lane = lax.broadcasted_iota(jnp.int32, x.shape, 1)
    return jnp.sum(jnp.where(lane == h, x, 0.0), axis=-1, keepdims=True)


def _decay_lanes(f, base):
    hi = f.astype(BF16).astype(F32)
    r = f - hi
    mid = r.astype(BF16).astype(F32)
    lo = (r - mid).astype(BF16).astype(F32)
    lane = lax.broadcasted_iota(jnp.int32, (f.shape[0], LANES), 1)
    ones = jnp.logical_and(lane >= 3 - base, lane < 6 - base)
    out = jnp.where(lane == base, hi, jnp.where(lane == base + 1, mid, jnp.where(
        lane == base + 2, lo, jnp.where(ones, 1.0, 0.0))))
    return out.astype(BF16)


def _fox_attn_kernel(q_ref, k_ref, v_ref, cum_ref, o_ref, ka_ref, vt_ref, qa_ref, kstat_ref,
                     sa_ref, sb_ref, pa_ref, pb_ref, m_ref, l_ref, ala_ref, alb_ref, acc_ref,
                     *, tk, qscale):
    h = pl.program_id(1)
    i = pl.program_id(2)
    tq, dh = q_ref.shape
    nkv = vt_ref.shape[0]
    lane_row = lax.broadcasted_iota(jnp.int32, (1, LANES), 1)

    @pl.when(i == 0)
    def _():
        kn2 = jnp.zeros((1, 1), F32)
        for c in range(nkv):
            rows = slice(c * tk, (c + 1) * tk)
            kc = k_ref[rows, :]
            kf = kc.astype(F32)
            kn2 = jnp.maximum(kn2, jnp.max(jnp.sum(kf * kf, axis=1, keepdims=True),
                                           axis=0, keepdims=True))
            ka_ref[rows, :dh] = kc
            ka_ref[rows, dh:] = _decay_lanes(_lane_col(cum_ref[rows, :], h) * (-LOG2E), 0)
            vt_ref[c] = v_ref[rows, :].T
        stat = jnp.where(lane_row == LANES - 1, kn2, 0.0)
        for k in range(nkv // 2):
            last = (k + 1) * tq - 1
            stat = jnp.where(lane_row == k, _lane_col(cum_ref[last:last + 1, :], h) * LOG2E, stat)
        kstat_ref[...] = stat

    qs = (q_ref[...].astype(F32) * qscale).astype(BF16)
    fq = _lane_col(cum_ref[pl.ds(pl.multiple_of(i * tq, tq), tq), :], h) * LOG2E
    qa_ref[:, :dh] = qs
    qa_ref[:, dh:] = _decay_lanes(fq, 3)
    qf = qs.astype(F32)
    qn2 = jnp.max(jnp.sum(qf * qf, axis=1, keepdims=True), axis=0, keepdims=True)
    fq_max = jnp.max(fq, axis=0, keepdims=True)
    m_ref[...] = jnp.full(m_ref.shape, -jnp.inf, F32)
    l_ref[...] = jnp.zeros(l_ref.shape, F32)
    acc_ref[...] = jnp.zeros(acc_ref.shape, F32)

    def qk(c, s_out):
        ka = ka_ref[pl.ds(pl.multiple_of(c * tk, tk), tk), :]
        s_out[...] = _dot_nt(ka, qa_ref[...])

    def pv(c, p_in, al_in):
        acc_ref[...] = al_in[...] * acc_ref[...] + _dot(vt_ref[c], p_in[...])

    def softmax(s_in, p_out, al_out, key_offset=None):
        st = s_in[...]
        if key_offset is not None:
            keep = (lax.broadcasted_iota(jnp.int32, st.shape, 0) + key_offset
                    <= lax.broadcasted_iota(jnp.int32, st.shape, 1))
            st = jnp.where(keep, st, -jnp.inf)
        m = m_ref[...]
        m_new = jnp.maximum(m, jnp.max(st, axis=0, keepdims=True))
        alpha = jnp.exp2(m - m_new)
        p = jnp.exp2(st - m_new)
        l_ref[...] = alpha * l_ref[...] + jnp.sum(p, axis=0, keepdims=True)
        m_ref[...] = m_new
        al_out[...] = alpha
        p_out[...] = p.astype(BF16)

    qk(2 * i, sa_ref)
    qk(2 * i + 1, sb_ref)
    softmax(sa_ref, pa_ref, ala_ref, 0)
    pv(2 * i, pa_ref, ala_ref)
    softmax(sb_ref, pb_ref, alb_ref, tk)

    stat = kstat_ref[...]
    kn2 = jnp.sum(jnp.where(lane_row == LANES - 1, stat, 0.0), axis=1, keepdims=True)
    bound = jnp.sqrt(qn2 * kn2) + fq_max - stat
    m_min = jnp.min(m_ref[...], axis=1, keepdims=True)
    live = jnp.logical_and(lane_row < i, bound >= m_min - SKIP_LOG2)
    first = jnp.min(jnp.where(live, lane_row, i).astype(F32), axis=1, keepdims=True)
    n_old = i - first[0, 0].astype(jnp.int32)

    def body(j, carry):
        k = i - 1 - j
        qk(2 * k, sa_ref)
        qk(2 * k + 1, sb_ref)
        pv(2 * k + 3, pb_ref, alb_ref)
        softmax(sa_ref, pa_ref, ala_ref)
        pv(2 * k, pa_ref, ala_ref)
        softmax(sb_ref, pb_ref, alb_ref)
        return carry

    lax.fori_loop(0, n_old, body, 0)
    pv(2 * (i - n_old) + 1, pb_ref, alb_ref)
    o_ref[...] = (acc_ref[...] / l_ref[...]).T.astype(o_ref.dtype)


def _fox_attn(qkv, cum, batch, heads, tq):
    t = qkv.shape[0]
    dh = qkv.shape[1] // (3 * heads)
    s = t // batch
    nq = s // tq
    tk = tq // 2
    return pl.pallas_call(
        functools.partial(_fox_attn_kernel, tk=tk, qscale=dh ** -0.5 * LOG2E),
        out_shape=jax.ShapeDtypeStruct((t, heads * dh), BF16),
        grid=(batch, heads, nq),
        in_specs=[
            pl.BlockSpec((tq, dh), lambda b, h, i: (b * nq + i, h)),
            pl.BlockSpec((s, dh), lambda b, h, i: (b, heads + h)),
            pl.BlockSpec((s, dh), lambda b, h, i: (b, 2 * heads + h)),
            pl.BlockSpec((s, LANES), lambda b, h, i: (b, 0)),
        ],
        out_specs=pl.BlockSpec((tq, dh), lambda b, h, i: (b * nq + i, h)),
        scratch_shapes=[
            pltpu.VMEM((s, 2 * dh), BF16),
            pltpu.VMEM((s // tk, dh, tk), BF16),
            pltpu.VMEM((tq, 2 * dh), BF16),
            pltpu.VMEM((1, LANES), F32),
            pltpu.VMEM((tk, tq), F32), pltpu.VMEM((tk, tq), F32),
            pltpu.VMEM((tk, tq), BF16), pltpu.VMEM((tk, tq), BF16),
            pltpu.VMEM((1, tq), F32), pltpu.VMEM((1, tq), F32),
            pltpu.VMEM((1, tq), F32), pltpu.VMEM((1, tq), F32),
            pltpu.VMEM((dh, tq), F32),
        ],
        compiler_params=_cparams("parallel", "parallel", "arbitrary"),
        name="fox_attn",
    )(qkv, qkv, qkv, cum)


def _sgu_kernel(x_ref, gpre_ref, sh_ref, sc_ref, win_ref, lng_ref, lnb_ref, ws_ref, bs_ref,
                wout_ref, gpost_ref, gate_ref, o_ref, u_ref, vn_ref, gated_ref):
    x = x_ref[...]
    tm = x.shape[0]
    width = u_ref.shape[1]
    groups, chunk, _ = ws_ref.shape
    gd = width // groups
    h = _prenorm(x, gpre_ref[...], sh_ref[0], sc_ref[0]).astype(BF16)
    u_ref[...] = jax.nn.gelu(_dot(h, win_ref[:, :width]))
    v = jax.nn.gelu(_dot(h, win_ref[:, width:]))
    mu = jnp.mean(v, axis=-1, keepdims=True)
    vc = v - mu
    var = jnp.mean(vc * vc, axis=-1, keepdims=True)
    vn_ref[...] = (vc * lax.rsqrt(var + EPS) * lng_ref[...] + lnb_ref[...]).astype(BF16)
    causal = (lax.broadcasted_iota(jnp.int32, (chunk, chunk), 1)
              <= lax.broadcasted_iota(jnp.int32, (chunk, chunk), 0))
    for g in range(groups):
        wg = jnp.where(causal, ws_ref[g], jnp.zeros((), BF16))
        bias = bs_ref[:, g:g + 1]
        cols = slice(g * gd, (g + 1) * gd)
        for c in range(tm // chunk):
            rows = slice(c * chunk, (c + 1) * chunk)
            f = _dot(wg, vn_ref[rows, cols]) + bias
            gated_ref[rows, cols] = (u_ref[rows, cols] * f).astype(BF16)
    y = _dot(gated_ref[...], wout_ref[...])
    o_ref[...] = _postnorm_residual(x, y, gpost_ref[...], gate_ref[0])


def _sgu(x, gpre, gpost, mod, w_in, ln_g, ln_b, w_s, b_st, w_out, tm):
    t, d = x.shape
    width = w_out.shape[0]
    row = lambda n: pl.BlockSpec((1, n), lambda i: (0, 0))
    return pl.pallas_call(
        _sgu_kernel,
        out_shape=jax.ShapeDtypeStruct((t, d), F32),
        grid=(t // tm,),
        in_specs=[
            pl.BlockSpec((tm, d), lambda i: (i, 0)),
            row(d), mod.spec(0, tm), mod.spec(1, tm),
            _const_spec(w_in.shape),
            row(width), row(width),
            _const_spec(w_s.shape),
            _const_spec(b_st.shape),
            _const_spec(w_out.shape),
            row(d), mod.spec(2, tm),
        ],
        out_specs=pl.BlockSpec((tm, d), lambda i: (i, 0)),
        scratch_shapes=[pltpu.VMEM((tm, width), F32), pltpu.VMEM((tm, width), BF16),
                        pltpu.VMEM((tm, width), BF16)],
        compiler_params=_cparams("parallel"),
        name="sgu",
    )(x, gpre, mod.mod, mod.mod, w_in, ln_g, ln_b, w_s, b_st, w_out, gpost, mod.mod)


def _swa_attn_kernel(sink_ref, q_ref, kc_ref, kp_ref, vc_ref, vp_ref, o_ref, *, nb, grp, scale):
    n = pl.program_id(0) % nb
    blk = q_ref.shape[0]
    half = LANES // 2
    kv_cols = kc_ref.shape[1] // LANES
    npair = grp // 2
    row = lax.broadcasted_iota(jnp.int32, (blk, blk), 0)
    col = lax.broadcasted_iota(jnp.int32, (blk, blk), 1)
    keep_p = jnp.logical_and(row > col, n > 0)
    keep_c = row <= col
    keep = jnp.concatenate([keep_p, keep_c], axis=0)
    keep = jnp.concatenate([keep] * npair, axis=1)
    lo = lax.broadcasted_iota(jnp.int32, (2 * blk, LANES), 1) < half
    scores, vts = {}, {}
    for j in range(kv_cols):
        csl = slice(j * LANES, (j + 1) * LANES)
        kf = jnp.concatenate([kp_ref[:, csl], kc_ref[:, csl]], axis=0).astype(F32) * scale
        kr = pltpu.roll(kf, half, 1)
        vt = jnp.concatenate([vp_ref[:, csl], vc_ref[:, csl]], axis=0).T
        for e in range(2):
            kh = 2 * j + e
            qcols = [q_ref[:, (kh * npair + a) * LANES:(kh * npair + a + 1) * LANES]
                     for a in range(npair)]
            rhs = jnp.concatenate(qcols, axis=0)
            vts[kh] = vt[e * half:(e + 1) * half, :]
            for p in range(2):
                src = kf if p == e else kr
                kz = jnp.where(lo if p == 0 else jnp.logical_not(lo), src, 0.0).astype(BF16)
                scores[kh, p] = _dot_nt(kz, rhs)
    probs = {}
    for (kh, p), st in scores.items():
        st = jnp.where(keep, st, -jnp.inf)
        sink = jnp.concatenate(
            [jnp.full((1, blk), sink_ref[kh * grp + 2 * a + p], F32) for a in range(npair)], axis=1)
        m = jnp.maximum(jnp.max(st, axis=0, keepdims=True), sink)
        pt = jnp.exp(st - m)
        den = jnp.sum(pt, axis=0, keepdims=True) + jnp.exp(sink - m)
        probs[kh, p] = (pt.astype(BF16), den)
    for kh in range(2 * kv_cols):
        outs = [_dot(vts[kh], probs[kh, p][0]) / probs[kh, p][1] for p in range(2)]
        for a in range(npair):
            ot = jnp.concatenate([o[:, a * blk:(a + 1) * blk] for o in outs], axis=0)
            c0 = (kh * npair + a) * LANES
            o_ref[:, c0:c0 + LANES] = ot.T.astype(o_ref.dtype)


def _swa_attn(qkv, sinks, batch, hq, hk, dh):
    t = qkv.shape[0]
    nb = t // batch // BLOCK
    grp = hq // hk
    assert 2 * dh == LANES and grp % 2 == 0 and hk % 2 == 0
    prev = lambda r: jnp.maximum(r - 1, 0)
    return pl.pallas_call(
        functools.partial(_swa_attn_kernel, nb=nb, grp=grp, scale=dh ** -0.5),
        out_shape=jax.ShapeDtypeStruct((t, hq * dh), BF16),
        grid=(t // BLOCK,),
        in_specs=[
            pl.BlockSpec(memory_space=pltpu.SMEM),
            pl.BlockSpec((BLOCK, hq * dh), lambda r: (r, 0)),
            pl.BlockSpec((BLOCK, hk * dh), lambda r: (r, grp)),
            pl.BlockSpec((BLOCK, hk * dh), lambda r: (prev(r), grp)),
            pl.BlockSpec((BLOCK, hk * dh), lambda r: (r, grp + 1)),
            pl.BlockSpec((BLOCK, hk * dh), lambda r: (prev(r), grp + 1)),
        ],
        out_specs=pl.BlockSpec((BLOCK, hq * dh), lambda r: (r, 0)),
        compiler_params=_cparams("parallel"),
        name="swa_attn",
    )(sinks, qkv, qkv, qkv, qkv, qkv)


def _pad_lanes(a, n=LANES):
    return jnp.pad(a, [(0, 0)] * (a.ndim - 1) + [(0, n - a.shape[-1])])


def kernel(x, c, positions, ada_w, ada_b, mix_pre_g, mix_post_g, ffn_pre_g, ffn_post_g, ffn_w_gu, ffn_w_down, fox_w_in, fox_b_f, fox_w_out, sgu_w_in, sgu_ln_g, sgu_ln_b, sgu_w_s, sgu_b_s, sgu_w_out, swa_w_in, swa_sinks, swa_w_out):
    batch, seq, d = x.shape
    depth = ada_w.shape[0]
    t = batch * seq
    assert batch <= SUBLANES and seq % BLOCK == 0 and d % LANES == 0

    tm = _pick(seq, 512)
    xf = x.reshape(t, d)

    c_pad = jnp.pad(c, ((0, SUBLANES - batch), (0, 0)))
    mod_all = _adaln(c_pad, ada_w, ada_b).reshape(depth * SUBLANES * N_MOD, 1, d)

    for i in range(depth):
        kind, j = i % N_MIXERS, i // N_MIXERS
        mod = _Mod(mod_all, i, seq)
        gpre, gpost = mix_pre_g[i].reshape(1, d), mix_post_g[i].reshape(1, d)
        if kind == 0:
            heads = fox_b_f.shape[1]
            nqkv = fox_w_in.shape[2] - heads
            w = fox_w_in[j].astype(BF16)
            qkv, fg = _fox_proj(xf, gpre, mod, w, _pad_lanes(w[:, nqkv:]), nqkv,
                                tm, _pick(nqkv, 1024))
            cum = _gate_cumsum(fg, _pad_lanes(fox_b_f[j].reshape(1, heads)), batch, _pick(seq, 512))
            a = _fox_attn(qkv, cum, batch, heads, _pick(seq, 1024))
            xf = _out_proj(a, fox_w_out[j].astype(BF16), xf, gpost, mod, tm)
        elif kind == 1:
            groups = sgu_w_s.shape[1]
            xf = _sgu(xf, gpre, gpost, mod, sgu_w_in[j].astype(BF16),
                      sgu_ln_g[j].reshape(1, -1), sgu_ln_b[j].reshape(1, -1),
                      sgu_w_s[j].astype(BF16), _pad_lanes(sgu_b_s[j].T),
                      sgu_w_out[j].astype(BF16), _pick(seq, 256))
        else:
            dh = SWA_HEAD_DIM
            hq = swa_sinks.shape[1]
            hk = (swa_w_in.shape[2] // dh - hq) // 2
            rope = dh // 4
            half = rope // 2
            inv = ROPE_THETA ** (-jnp.arange(0, rope, 2, dtype=F32) / rope)
            lane_d = jnp.arange(LANES) % dh
            inv_l = jnp.where(lane_d < rope, inv[lane_d % half], 0.0).reshape(1, LANES).astype(F32)
            m1 = jnp.where(lane_d < half, -1.0, 0.0).reshape(1, LANES).astype(F32)
            m2 = jnp.where((lane_d >= half) & (lane_d < rope), 1.0, 0.0).reshape(1, LANES).astype(F32)
            qkv = _swa_proj(xf, gpre, mod, swa_w_in[j].astype(BF16), positions.reshape(t, 1),
                            inv_l, m1, m2, tm, _pick(hk * dh, 512), (hq + hk) * dh, half)
            a = _swa_attn(qkv, swa_sinks[j], batch, hq, hk, dh)
            xf = _out_proj(a, swa_w_out[j].astype(BF16), xf, gpost, mod, tm)
        xf = _ffn(xf, ffn_pre_g[i].reshape(1, d), ffn_post_g[i].reshape(1, d), mod,
                  ffn_w_gu[i].astype(BF16), ffn_w_down[i].astype(BF16), tm,
                  _pick(ffn_w_down.shape[1], 512))
    return xf.reshape(batch, seq, d)
```

```python
import functools

import jax
import jax.numpy as jnp
from jax import lax
from jax.experimental import pallas as pl
from jax.experimental.pallas import tpu as pltpu

F32 = jnp.float32
BF16 = jnp.bfloat16

EPS = 1e-6
N_MIXERS = 3
BLOCK = 128
SWA_HEAD_DIM = 64
ROPE_THETA = 500000.0
LANES = 128
SUBLANES = 8
N_MOD = 6
LOG2E = 1.4426950408889634
SKIP_LOG2 = 160.0
VMEM_LIMIT = 56 * 1024 * 1024


def _cparams(*sem):
    return pltpu.CompilerParams(dimension_semantics=sem, vmem_limit_bytes=VMEM_LIMIT)


def _const_spec(shape):
    nd = len(shape)
    return pl.BlockSpec(shape, lambda *_: (0,) * nd, pipeline_mode=pl.Buffered(1))


def _pick(n, pref):
    t = min(n, pref)
    while n % t:
        t -= LANES
    return t


def _rms(x):
    return x * lax.rsqrt(jnp.mean(x * x, axis=-1, keepdims=True) + EPS)


def _prenorm(x, g, sh, sc):
    return (_rms(x) * g) * (1.0 + sc) + sh


def _postnorm_residual(x, y, g, gate):
    return x + gate * (_rms(y) * g)


def _dot(a, b):
    return jnp.dot(a, b, preferred_element_type=F32)


def _dot_nt(a, b):
    return lax.dot_general(a, b, (((1,), (1,)), ((), ())), preferred_element_type=F32)


def _adaln_kernel(c_ref, w_ref, b_ref, o_ref):
    c = c_ref[...]
    ca = (c * jax.nn.sigmoid(c)).astype(BF16)
    o_ref[0] = _dot(ca, w_ref[0].astype(BF16)) + b_ref[0]


def _adaln(c_pad, ada_w, ada_b):
    depth, d, n = ada_w.shape
    tn = _pick(n, 1024)
    return pl.pallas_call(
        _adaln_kernel,
        out_shape=jax.ShapeDtypeStruct((depth, SUBLANES, n), F32),
        grid=(depth, n // tn),
        in_specs=[
            pl.BlockSpec((SUBLANES, d), lambda i, j: (0, 0)),
            pl.BlockSpec((1, d, tn), lambda i, j: (i, 0, j)),
            pl.BlockSpec((1, 1, tn), lambda i, j: (i, 0, j)),
        ],
        out_specs=pl.BlockSpec((1, SUBLANES, tn), lambda i, j: (i, 0, j)),
        compiler_params=_cparams("parallel", "parallel"),
        name="adaln",
    )(c_pad, ada_w, ada_b.reshape(depth, 1, n))


class _Mod:
    def __init__(self, mod, layer, batch, seq):
        self.mod = mod
        self.layer = layer
        self.batch = batch
        self.seq = seq
        self.d = mod.shape[-1]

    def spec(self, k, tm, ahead=0):
        layer, tpb = self.layer, self.seq // tm
        last = self.batch * tpb - 1
        return pl.BlockSpec(
            (1, 1, self.d),
            lambda i, *_: ((layer * SUBLANES + jnp.minimum(i + ahead, last) // tpb) * N_MOD + k, 0, 0))


def _x_ahead_spec(tm, d, ntiles, nj):
    return pl.BlockSpec(
        (tm, d), lambda i, j: (jnp.where(j == nj - 1, jnp.minimum(i + 1, ntiles - 1), i), 0))


def _fox_proj_kernel(x_ref, g_ref, sh_ref, sc_ref, shn_ref, scn_ref, w_ref, wf_ref, o_ref, fg_ref,
                     h_ref):
    i, j = pl.program_id(0), pl.program_id(1)
    cur = i % 2

    @pl.when(jnp.logical_and(i == 0, j == 0))
    def _():
        h_ref[0] = _prenorm(x_ref[...], g_ref[...], sh_ref[0], sc_ref[0]).astype(BF16)

    @pl.when(j == 0)
    def _():
        h = h_ref[cur]
        fg_ref[...] = _dot(h, wf_ref[...])
        o_ref[...] = _dot(h, w_ref[...]).astype(o_ref.dtype)

    @pl.when(jnp.logical_and(j > 0, j < pl.num_programs(1) - 1))
    def _():
        o_ref[...] = _dot(h_ref[cur], w_ref[...]).astype(o_ref.dtype)

    @pl.when(j == pl.num_programs(1) - 1)
    def _():
        h_ref[1 - cur] = _prenorm(x_ref[...], g_ref[...], shn_ref[0], scn_ref[0]).astype(BF16)
        o_ref[...] = _dot(h_ref[cur], w_ref[...]).astype(o_ref.dtype)


def _fox_proj(x, gain, mod, w, wf, layer, n, tm, tn):
    t, d = x.shape
    nj = n // tn
    assert nj >= 2
    return pl.pallas_call(
        _fox_proj_kernel,
        out_shape=(jax.ShapeDtypeStruct((t, n), BF16), jax.ShapeDtypeStruct((t, LANES), F32)),
        grid=(t // tm, nj),
        in_specs=[
            _x_ahead_spec(tm, d, t // tm, nj),
            pl.BlockSpec((1, d), lambda i, j: (0, 0)),
            mod.spec(0, tm), mod.spec(1, tm), mod.spec(0, tm, 1), mod.spec(1, tm, 1),
            pl.BlockSpec((None, d, tn), lambda i, j: (layer, 0, j)),
            _const_spec((d, LANES)),
        ],
        out_specs=(pl.BlockSpec((tm, tn), lambda i, j: (i, j)),
                   pl.BlockSpec((tm, LANES), lambda i, j: (i, 0))),
        scratch_shapes=[pltpu.VMEM((2, tm, d), BF16)],
        compiler_params=_cparams("arbitrary", "arbitrary"),
        name="fox_proj",
    )(x, gain, mod.mod, mod.mod, mod.mod, mod.mod, w, wf)


def _swa_proj_kernel(x_ref, g_ref, sh_ref, sc_ref, shn_ref, scn_ref, w_ref, pos_ref, inv_ref,
                     m1_ref, m2_ref, o_ref, h_ref, cos_ref, s1_ref, s2_ref, *, n_rope_tiles, shift):
    i, j = pl.program_id(0), pl.program_id(1)
    cur = i % 2

    def stage(slot, sh, sc):
        h_ref[slot] = _prenorm(x_ref[...], g_ref[...], sh[0], sc[0]).astype(BF16)
        ang = pos_ref[...].astype(F32) * inv_ref[...]
        sn = jnp.sin(ang)
        cos_ref[slot] = jnp.cos(ang)
        s1_ref[slot] = sn * m1_ref[...]
        s2_ref[slot] = sn * m2_ref[...]

    def rope_tile():
        acc = _dot(h_ref[cur], w_ref[...])
        cs, s1, s2 = cos_ref[cur], s1_ref[cur], s2_ref[cur]
        for c in range(acc.shape[1] // LANES):
            a = acc[:, c * LANES:(c + 1) * LANES]
            r = a * cs + pltpu.roll(a, LANES - shift, 1) * s1 + pltpu.roll(a, shift, 1) * s2
            o_ref[:, c * LANES:(c + 1) * LANES] = r.astype(o_ref.dtype)

    @pl.when(jnp.logical_and(i == 0, j == 0))
    def _():
        stage(0, sh_ref, sc_ref)

    @pl.when(j < n_rope_tiles)
    def _():
        rope_tile()

    @pl.when(jnp.logical_and(j >= n_rope_tiles, j < pl.num_programs(1) - 1))
    def _():
        o_ref[...] = _dot(h_ref[cur], w_ref[...]).astype(o_ref.dtype)

    @pl.when(j == pl.num_programs(1) - 1)
    def _():
        stage(1 - cur, shn_ref, scn_ref)
        o_ref[...] = _dot(h_ref[cur], w_ref[...]).astype(o_ref.dtype)


def _swa_proj(x, gain, mod, w, pos, inv_l, m1, m2, tm, tn, n_rope_cols, shift):
    t, d = x.shape
    n = w.shape[1]
    nt, nj = t // tm, n // tn
    assert n_rope_cols // tn < nj
    row = lambda shape: pl.BlockSpec(shape, lambda i, j: (0, 0))
    ahead = lambda i, j: (jnp.where(j == nj - 1, jnp.minimum(i + 1, nt - 1), i), 0)
    return pl.pallas_call(
        functools.partial(_swa_proj_kernel, n_rope_tiles=n_rope_cols // tn, shift=shift),
        out_shape=jax.ShapeDtypeStruct((t, n), BF16),
        grid=(nt, nj),
        in_specs=[
            _x_ahead_spec(tm, d, nt, nj),
            row((1, d)),
            mod.spec(0, tm), mod.spec(1, tm), mod.spec(0, tm, 1), mod.spec(1, tm, 1),
            pl.BlockSpec((d, tn), lambda i, j: (0, j)),
            pl.BlockSpec((tm, 1), ahead),
            row((1, LANES)), row((1, LANES)), row((1, LANES)),
        ],
        out_specs=pl.BlockSpec((tm, tn), lambda i, j: (i, j)),
        scratch_shapes=[pltpu.VMEM((2, tm, d), BF16)] + [pltpu.VMEM((2, tm, LANES), F32)] * 3,
        compiler_params=_cparams("arbitrary", "arbitrary"),
        name="swa_proj",
    )(x, gain, mod.mod, mod.mod, mod.mod, mod.mod, w, pos, inv_l, m1, m2)


def _out_kernel(a_ref, w_ref, x_ref, g_ref, gate_ref, o_ref):
    y = _dot(a_ref[...], w_ref[...])
    o_ref[...] = _postnorm_residual(x_ref[...], y, g_ref[...], gate_ref[0])


def _out_proj(a, w, layer, x, gain, mod, tm):
    t, k = a.shape
    d = w.shape[2]
    return pl.pallas_call(
        _out_kernel,
        out_shape=jax.ShapeDtypeStruct((t, d), F32),
        grid=(t // tm,),
        in_specs=[
            pl.BlockSpec((tm, k), lambda i: (i, 0)),
            pl.BlockSpec((None, k, d), lambda i: (layer, 0, 0), pipeline_mode=pl.Buffered(1)),
            pl.BlockSpec((tm, d), lambda i: (i, 0)),
            pl.BlockSpec((1, d), lambda i: (0, 0)),
            mod.spec(2, tm),
        ],
        out_specs=pl.BlockSpec((tm, d), lambda i: (i, 0)),
        compiler_params=_cparams("parallel"),
        name="out_proj",
    )(a, w, x, gain, mod.mod)


def _ffn_kernel(x_ref, gpre_ref, sh_ref, sc_ref, shn_ref, scn_ref, wg_ref, wu_ref, wd_ref,
                gpost_ref, gate_ref, o_ref, h_ref, xs_ref, acc_ref):
    i, j = pl.program_id(0), pl.program_id(1)
    last = pl.num_programs(1) - 1
    cur = i % 2

    def stage(slot, sh, sc):
        x = x_ref[...]
        xs_ref[slot] = x
        h_ref[slot] = _prenorm(x, gpre_ref[...], sh[0], sc[0]).astype(BF16)

    def chunk():
        h = h_ref[cur]
        g = _dot(h, wg_ref[...])
        u = _dot(h, wu_ref[...])
        a = (g * jax.nn.sigmoid(g) * u).astype(BF16)
        return _dot(a, wd_ref[...])

    @pl.when(jnp.logical_and(i == 0, j == 0))
    def _():
        stage(0, sh_ref, sc_ref)

    @pl.when(j == 0)
    def _():
        acc_ref[...] = chunk()

    @pl.when(jnp.logical_and(j > 0, j < last))
    def _():
        acc_ref[...] += chunk()

    @pl.when(j == last)
    def _():
        stage(1 - cur, shn_ref, scn_ref)
        y = acc_ref[...] + chunk()
        o_ref[...] = _postnorm_residual(xs_ref[cur], y, gpost_ref[...], gate_ref[0])


def _ffn(x, gpre, gpost, mod, w_gu, w_down, layer, tm, tf):
    t, d = x.shape
    dff = w_down.shape[1]
    nf = dff // tf
    assert nf >= 2
    row = lambda: pl.BlockSpec((1, d), lambda i, j: (0, 0))
    return pl.pallas_call(
        _ffn_kernel,
        out_shape=jax.ShapeDtypeStruct((t, d), F32),
        grid=(t // tm, nf),
        in_specs=[
            _x_ahead_spec(tm, d, t // tm, nf),
            row(), mod.spec(3, tm), mod.spec(4, tm), mod.spec(3, tm, 1), mod.spec(4, tm, 1),
            pl.BlockSpec((None, d, tf), lambda i, j: (layer, 0, j)),
            pl.BlockSpec((None, d, tf), lambda i, j: (layer, 0, nf + j)),
            pl.BlockSpec((None, tf, d), lambda i, j: (layer, j, 0)),
            row(), mod.spec(5, tm),
        ],
        out_specs=pl.BlockSpec((tm, d), lambda i, j: (i, 0)),
        scratch_shapes=[pltpu.VMEM((2, tm, d), BF16), pltpu.VMEM((2, tm, d), F32),
                        pltpu.VMEM((tm, d), F32)],
        compiler_params=_cparams("arbitrary", "arbitrary"),
        name="ffn",
    )(x, gpre, mod.mod, mod.mod, mod.mod, mod.mod, w_gu, w_gu, w_down, gpost, mod.mod)


def _gate_kernel(fg_ref, b_ref, cum_ref, carry_ref):
    @pl.when(pl.program_id(1) == 0)
    def _():
        carry_ref[...] = jnp.zeros_like(carry_ref)

    z = fg_ref[...] + b_ref[...]
    lf = jnp.minimum(z, 0.0) - jnp.log1p(jnp.exp(-jnp.abs(z)))
    n = lf.shape[0]
    tril = (lax.broadcasted_iota(jnp.int32, (n, n), 1)
            <= lax.broadcasted_iota(jnp.int32, (n, n), 0)).astype(BF16)
    hi = lf.astype(BF16)
    r1 = lf - hi.astype(F32)
    mid = r1.astype(BF16)
    lo = (r1 - mid.astype(F32)).astype(BF16)
    cum = (_dot(tril, hi) + _dot(tril, mid)) + _dot(tril, lo) + carry_ref[...]
    cum_ref[...] = cum
    carry_ref[...] = cum[n - 1:n, :]


def _gate_cumsum(fg, b_pad, batch, tg):
    t = fg.shape[0]
    s = t // batch
    nt = s // tg
    return pl.pallas_call(
        _gate_kernel,
        out_shape=jax.ShapeDtypeStruct((t, LANES), F32),
        grid=(batch, nt),
        in_specs=[
            pl.BlockSpec((tg, LANES), lambda b, j: (b * nt + j, 0)),
            pl.BlockSpec((1, LANES), lambda b, j: (0, 0)),
        ],
        out_specs=pl.BlockSpec((tg, LANES), lambda b, j: (b * nt + j, 0)),
        scratch_shapes=[pltpu.VMEM((1, LANES), F32)],
        compiler_params=_cparams("parallel", "arbitrary"),
        name="fox_gate",
    )(fg, b_pad)


def _lane_col(x, h):
    lane = lax.broadcasted_iota(jnp.int32, x.shape, 1)
    return jnp.sum(jnp.where(lane == h, x, 0.0), axis=-1, keepdims=True)


def _decay_lanes(f, base):
    hi = f.astype(BF16).astype(F32)
    r = f - hi
    mid = r.astype(BF16).astype(F32)
    lo = (r - mid).astype(BF16).astype(F32)
    lane = lax.broadcasted_iota(jnp.int32, (f.shape[0], LANES), 1)
    ones = jnp.logical_and(lane >= 3 - base, lane < 6 - base)
    out = jnp.where(lane == base, hi, jnp.where(lane == base + 1, mid, jnp.where(
        lane == base + 2, lo, jnp.where(ones, 1.0, 0.0))))
    return out.astype(BF16)


def _fox_attn_kernel(q_ref, k_ref, v_ref, cum_ref, o_ref, ka_ref, vt_ref, qa_ref, kstat_ref,
                     sa_ref, sb_ref, pa_ref, pb_ref, m_ref, l_ref, ala_ref, alb_ref, acc_ref,
                     *, tk, qscale):
    h = pl.program_id(1)
    i = pl.program_id(2)
    tq, dh = q_ref.shape
    nkv = vt_ref.shape[0]
    lane_row = lax.broadcasted_iota(jnp.int32, (1, LANES), 1)

    @pl.when(i == 0)
    def _():
        kn2 = jnp.zeros((1, 1), F32)
        for c in range(nkv):
            rows = slice(c * tk, (c + 1) * tk)
            kc = k_ref[rows, :]
            kf = kc.astype(F32)
            kn2 = jnp.maximum(kn2, jnp.max(jnp.sum(kf * kf, axis=1, keepdims=True),
                                           axis=0, keepdims=True))
            ka_ref[rows, :dh] = kc
            ka_ref[rows, dh:] = _decay_lanes(_lane_col(cum_ref[rows, :], h) * (-LOG2E), 0)
            vt_ref[c] = v_ref[rows, :].T
        stat = jnp.where(lane_row == LANES - 1, kn2, 0.0)
        for k in range(nkv // 2):
            last = (k + 1) * tq - 1
            stat = jnp.where(lane_row == k, _lane_col(cum_ref[last:last + 1, :], h) * LOG2E, stat)
        kstat_ref[...] = stat

    qs = (q_ref[...].astype(F32) * qscale).astype(BF16)
    fq = _lane_col(cum_ref[pl.ds(pl.multiple_of(i * tq, tq), tq), :], h) * LOG2E
    qa_ref[:, :dh] = qs
    qa_ref[:, dh:] = _decay_lanes(fq, 3)
    qf = qs.astype(F32)
    qn2 = jnp.max(jnp.sum(qf * qf, axis=1, keepdims=True), axis=0, keepdims=True)
    fq_max = jnp.max(fq, axis=0, keepdims=True)
    m_ref[...] = jnp.full(m_ref.shape, -jnp.inf, F32)
    l_ref[...] = jnp.zeros(l_ref.shape, F32)
    acc_ref[...] = jnp.zeros(acc_ref.shape, F32)

    def qk(c, s_out):
        ka = ka_ref[pl.ds(pl.multiple_of(c * tk, tk), tk), :]
        s_out[...] = _dot_nt(ka, qa_ref[...])

    def pv(c, p_in, al_in):
        acc_ref[...] = al_in[...] * acc_ref[...] + _dot(vt_ref[c], p_in[...])

    def softmax(s_in, p_out, al_out, key_offset=None):
        st = s_in[...]
        if key_offset is not None:
            keep = (lax.broadcasted_iota(jnp.int32, st.shape, 0) + key_offset
                    <= lax.broadcasted_iota(jnp.int32, st.shape, 1))
            st = jnp.where(keep, st, -jnp.inf)
        m = m_ref[...]
        m_new = jnp.maximum(m, jnp.max(st, axis=0, keepdims=True))
        alpha = jnp.exp2(m - m_new)
        p = jnp.exp2(st - m_new)
        l_ref[...] = alpha * l_ref[...] + jnp.sum(p, axis=0, keepdims=True)
        m_ref[...] = m_new
        al_out[...] = alpha
        p_out[...] = p.astype(BF16)

    qk(2 * i, sa_ref)
    qk(2 * i + 1, sb_ref)
    softmax(sa_ref, pa_ref, ala_ref, 0)
    pv(2 * i, pa_ref, ala_ref)
    softmax(sb_ref, pb_ref, alb_ref, tk)

    stat = kstat_ref[...]
    kn2 = jnp.sum(jnp.where(lane_row == LANES - 1, stat, 0.0), axis=1, keepdims=True)
    bound = jnp.sqrt(qn2 * kn2) + fq_max - stat
    m_min = jnp.min(m_ref[...], axis=1, keepdims=True)
    live = jnp.logical_and(lane_row < i, bound >= m_min - SKIP_LOG2)
    first = jnp.min(jnp.where(live, lane_row, i).astype(F32), axis=1, keepdims=True)
    n_old = i - first[0, 0].astype(jnp.int32)

    def body(j, carry):
        k = i - 1 - j
        qk(2 * k, sa_ref)
        qk(2 * k + 1, sb_ref)
        pv(2 * k + 3, pb_ref, alb_ref)
        softmax(sa_ref, pa_ref, ala_ref)
        pv(2 * k, pa_ref, ala_ref)
        softmax(sb_ref, pb_ref, alb_ref)
        return carry

    lax.fori_loop(0, n_old, body, 0)
    pv(2 * (i - n_old) + 1, pb_ref, alb_ref)
    o_ref[...] = (acc_ref[...] / l_ref[...]).T.astype(o_ref.dtype)


def _fox_attn(qkv, cum, batch, heads, tq):
    t = qkv.shape[0]
    dh = qkv.shape[1] // (3 * heads)
    s = t // batch
    nq = s // tq
    tk = tq // 2
    return pl.pallas_call(
        functools.partial(_fox_attn_kernel, tk=tk, qscale=dh ** -0.5 * LOG2E),
        out_shape=jax.ShapeDtypeStruct((t, heads * dh), BF16),
        grid=(batch, heads, nq),
        in_specs=[
            pl.BlockSpec((tq, dh), lambda b, h, i: (b * nq + i, h)),
            pl.BlockSpec((s, dh), lambda b, h, i: (b, heads + h)),
            pl.BlockSpec((s, dh), lambda b, h, i: (b, 2 * heads + h)),
            pl.BlockSpec((s, LANES), lambda b, h, i: (b, 0)),
        ],
        out_specs=pl.BlockSpec((tq, dh), lambda b, h, i: (b * nq + i, h)),
        scratch_shapes=[
            pltpu.VMEM((s, 2 * dh), BF16),
            pltpu.VMEM((s // tk, dh, tk), BF16),
            pltpu.VMEM((tq, 2 * dh), BF16),
            pltpu.VMEM((1, LANES), F32),
            pltpu.VMEM((tk, tq), F32), pltpu.VMEM((tk, tq), F32),
            pltpu.VMEM((tk, tq), BF16), pltpu.VMEM((tk, tq), BF16),
            pltpu.VMEM((1, tq), F32), pltpu.VMEM((1, tq), F32),
            pltpu.VMEM((1, tq), F32), pltpu.VMEM((1, tq), F32),
            pltpu.VMEM((dh, tq), F32),
        ],
        compiler_params=_cparams("parallel", "parallel", "arbitrary"),
        name="fox_attn",
    )(qkv, qkv, qkv, cum)


def _sgu_kernel(x_ref, gpre_ref, sh_ref, sc_ref, win_ref, lng_ref, lnb_ref, ws_ref, bs_ref,
                wout_ref, gpost_ref, gate_ref, o_ref, u_ref, vn_ref, gated_ref):
    x = x_ref[...]
    tm = x.shape[0]
    width = u_ref.shape[1]
    groups, chunk, _ = ws_ref.shape
    gd = width // groups
    h = _prenorm(x, gpre_ref[...], sh_ref[0], sc_ref[0]).astype(BF16)
    u_ref[...] = jax.nn.gelu(_dot(h, win_ref[:, :width]))
    v = jax.nn.gelu(_dot(h, win_ref[:, width:]))
    mu = jnp.mean(v, axis=-1, keepdims=True)
    vc = v - mu
    var = jnp.mean(vc * vc, axis=-1, keepdims=True)
    vn_ref[...] = (vc * lax.rsqrt(var + EPS) * lng_ref[...] + lnb_ref[...]).astype(BF16)
    causal = (lax.broadcasted_iota(jnp.int32, (chunk, chunk), 1)
              <= lax.broadcasted_iota(jnp.int32, (chunk, chunk), 0))
    for g in range(groups):
        wg = jnp.where(causal, ws_ref[g], jnp.zeros((), BF16))
        bias = bs_ref[:, g:g + 1]
        cols = slice(g * gd, (g + 1) * gd)
        for c in range(tm // chunk):
            rows = slice(c * chunk, (c + 1) * chunk)
            f = _dot(wg, vn_ref[rows, cols]) + bias
            gated_ref[rows, cols] = (u_ref[rows, cols] * f).astype(BF16)
    y = _dot(gated_ref[...], wout_ref[...])
    o_ref[...] = _postnorm_residual(x, y, gpost_ref[...], gate_ref[0])


def _sgu(x, gpre, gpost, mod, w_in, ln_g, ln_b, w_s, b_st, w_out, tm):
    t, d = x.shape
    width = w_out.shape[0]
    row = lambda n: pl.BlockSpec((1, n), lambda i: (0, 0))
    return pl.pallas_call(
        _sgu_kernel,
        out_shape=jax.ShapeDtypeStruct((t, d), F32),
        grid=(t // tm,),
        in_specs=[
            pl.BlockSpec((tm, d), lambda i: (i, 0)),
            row(d), mod.spec(0, tm), mod.spec(1, tm),
            _const_spec(w_in.shape),
            row(width), row(width),
            _const_spec(w_s.shape),
            _const_spec(b_st.shape),
            _const_spec(w_out.shape),
            row(d), mod.spec(2, tm),
        ],
        out_specs=pl.BlockSpec((tm, d), lambda i: (i, 0)),
        scratch_shapes=[pltpu.VMEM((tm, width), F32), pltpu.VMEM((tm, width), BF16),
                        pltpu.VMEM((tm, width), BF16)],
        compiler_params=_cparams("parallel"),
        name="sgu",
    )(x, gpre, mod.mod, mod.mod, w_in, ln_g, ln_b, w_s, b_st, w_out, gpost, mod.mod)


def _swa_attn_kernel(sink_ref, q_ref, kc_ref, kp_ref, vc_ref, vp_ref, o_ref, *, nb, grp, scale):
    n = pl.program_id(0) % nb
    blk = q_ref.shape[0]
    half = LANES // 2
    kv_cols = kc_ref.shape[1] // LANES
    npair = grp // 2
    row = lax.broadcasted_iota(jnp.int32, (blk, blk), 0)
    col = lax.broadcasted_iota(jnp.int32, (blk, blk), 1)
    keep_p = jnp.logical_and(row > col, n > 0)
    keep_c = row <= col
    keep = jnp.concatenate([keep_p, keep_c], axis=0)
    keep = jnp.concatenate([keep] * npair, axis=1)
    lo = lax.broadcasted_iota(jnp.int32, (2 * blk, LANES), 1) < half
    scores, vts = {}, {}
    for j in range(kv_cols):
        csl = slice(j * LANES, (j + 1) * LANES)
        kf = jnp.concatenate([kp_ref[:, csl], kc_ref[:, csl]], axis=0).astype(F32) * scale
        kr = pltpu.roll(kf, half, 1)
        vt = jnp.concatenate([vp_ref[:, csl], vc_ref[:, csl]], axis=0).T
        for e in range(2):
            kh = 2 * j + e
            qcols = [q_ref[:, (kh * npair + a) * LANES:(kh * npair + a + 1) * LANES]
                     for a in range(npair)]
            rhs = jnp.concatenate(qcols, axis=0)
            vts[kh] = vt[e * half:(e + 1) * half, :]
            for p in range(2):
                src = kf if p == e else kr
                kz = jnp.where(lo if p == 0 else jnp.logical_not(lo), src, 0.0).astype(BF16)
                scores[kh, p] = _dot_nt(kz, rhs)
    probs = {}
    for (kh, p), st in scores.items():
        st = jnp.where(keep, st, -jnp.inf)
        sink = jnp.concatenate(
            [jnp.full((1, blk), sink_ref[kh * grp + 2 * a + p], F32) for a in range(npair)], axis=1)
        m = jnp.maximum(jnp.max(st, axis=0, keepdims=True), sink)
        pt = jnp.exp(st - m)
        den = jnp.sum(pt, axis=0, keepdims=True) + jnp.exp(sink - m)
        probs[kh, p] = (pt.astype(BF16), den)
    for kh in range(2 * kv_cols):
        outs = [_dot(vts[kh], probs[kh, p][0]) / probs[kh, p][1] for p in range(2)]
        for a in range(npair):
            ot = jnp.concatenate([o[:, a * blk:(a + 1) * blk] for o in outs], axis=0)
            c0 = (kh * npair + a) * LANES
            o_ref[:, c0:c0 + LANES] = ot.T.astype(o_ref.dtype)


def _swa_attn(qkv, sinks, batch, hq, hk, dh):
    t = qkv.shape[0]
    nb = t // batch // BLOCK
    grp = hq // hk
    assert 2 * dh == LANES and grp % 2 == 0 and hk % 2 == 0
    prev = lambda r: jnp.maximum(r - 1, 0)
    return pl.pallas_call(
        functools.partial(_swa_attn_kernel, nb=nb, grp=grp, scale=dh ** -0.5),
        out_shape=jax.ShapeDtypeStruct((t, hq * dh), BF16),
        grid=(t // BLOCK,),
        in_specs=[
            pl.BlockSpec(memory_space=pltpu.SMEM),
            pl.BlockSpec((BLOCK, hq * dh), lambda r: (r, 0)),
            pl.BlockSpec((BLOCK, hk * dh), lambda r: (r, grp)),
            pl.BlockSpec((BLOCK, hk * dh), lambda r: (prev(r), grp)),
            pl.BlockSpec((BLOCK, hk * dh), lambda r: (r, grp + 1)),
            pl.BlockSpec((BLOCK, hk * dh), lambda r: (prev(r), grp + 1)),
        ],
        out_specs=pl.BlockSpec((BLOCK, hq * dh), lambda r: (r, 0)),
        compiler_params=_cparams("parallel"),
        name="swa_attn",
    )(sinks, qkv, qkv, qkv, qkv, qkv)


def _pad_lanes(a, n=LANES):
    return jnp.pad(a, [(0, 0)] * (a.ndim - 1) + [(0, n - a.shape[-1])])


def kernel(x, c, positions, ada_w, ada_b, mix_pre_g, mix_post_g, ffn_pre_g, ffn_post_g, ffn_w_gu, ffn_w_down, fox_w_in, fox_b_f, fox_w_out, sgu_w_in, sgu_ln_g, sgu_ln_b, sgu_w_s, sgu_b_s, sgu_w_out, swa_w_in, swa_sinks, swa_w_out):
    batch, seq, d = x.shape
    depth = ada_w.shape[0]
    t = batch * seq
    assert batch <= SUBLANES and seq % BLOCK == 0 and d % LANES == 0

    tm = _pick(seq, 512)
    xf = x.reshape(t, d)

    c_pad = jnp.pad(c, ((0, SUBLANES - batch), (0, 0)))
    mod_all = _adaln(c_pad, ada_w, ada_b).reshape(depth * SUBLANES * N_MOD, 1, d)

    ffn_gu, ffn_down = ffn_w_gu.astype(BF16), ffn_w_down.astype(BF16)
    fox_in, fox_out = fox_w_in.astype(BF16), fox_w_out.astype(BF16)
    swa_out = swa_w_out.astype(BF16)

    for i in range(depth):
        kind, j = i % N_MIXERS, i // N_MIXERS
        mod = _Mod(mod_all, i, batch, seq)
        gpre, gpost = mix_pre_g[i].reshape(1, d), mix_post_g[i].reshape(1, d)
        if kind == 0:
            heads = fox_b_f.shape[1]
            nqkv = fox_w_in.shape[2] - heads
            qkv, fg = _fox_proj(xf, gpre, mod, fox_in, _pad_lanes(fox_in[j, :, nqkv:]), j, nqkv,
                                tm, _pick(nqkv, 1024))
            cum = _gate_cumsum(fg, _pad_lanes(fox_b_f[j].reshape(1, heads)), batch, _pick(seq, 512))
            a = _fox_attn(qkv, cum, batch, heads, _pick(seq, 1024))
            xf = _out_proj(a, fox_out, j, xf, gpost, mod, tm)
        elif kind == 1:
            groups = sgu_w_s.shape[1]
            xf = _sgu(xf, gpre, gpost, mod, sgu_w_in[j].astype(BF16),
                      sgu_ln_g[j].reshape(1, -1), sgu_ln_b[j].reshape(1, -1),
                      sgu_w_s[j].astype(BF16), _pad_lanes(sgu_b_s[j].T),
                      sgu_w_out[j].astype(BF16), _pick(seq, 256))
        else:
            dh = SWA_HEAD_DIM
            hq = swa_sinks.shape[1]
            hk = (swa_w_in.shape[2] // dh - hq) // 2
            rope = dh // 4
            half = rope // 2
            inv = ROPE_THETA ** (-jnp.arange(0, rope, 2, dtype=F32) / rope)
            lane_d = jnp.arange(LANES) % dh
            inv_l = jnp.where(lane_d < rope, inv[lane_d % half], 0.0).reshape(1, LANES).astype(F32)
            m1 = jnp.where(lane_d < half, -1.0, 0.0).reshape(1, LANES).astype(F32)
            m2 = jnp.where((lane_d >= half) & (lane_d < rope), 1.0, 0.0).reshape(1, LANES).astype(F32)
            qkv = _swa_proj(xf, gpre, mod, swa_w_in[j].astype(BF16), positions.reshape(t, 1),
                            inv_l, m1, m2, tm, _pick(hk * dh, 512), (hq + hk) * dh, half)
            a = _swa_attn(qkv, swa_sinks[j], batch, hq, hk, dh)
            xf = _out_proj(a, swa_out, j, xf, gpost, mod, tm)
        xf = _ffn(xf, ffn_pre_g[i].reshape(1, d), ffn_post_g[i].reshape(1, d), mod,
                  ffn_gu, ffn_down, i, tm, _pick(ffn_w_down.shape[1], 512))
    return xf.reshape(batch, seq, d)
```

```python
import functools

import jax
import jax.numpy as jnp
from jax import lax
from jax.experimental import pallas as pl
from jax.experimental.pallas import tpu as pltpu

F32 = jnp.float32
BF16 = jnp.bfloat16

EPS = 1e-6
N_MIXERS = 3
BLOCK = 128
SWA_HEAD_DIM = 64
ROPE_THETA = 500000.0
LANES = 128
SUBLANES = 8
N_MOD = 6
LOG2E = 1.4426950408889634
SOFTMAX_ROWS = 32
SKIP_LOG2 = 160.0
VMEM_LIMIT = 56 * 1024 * 1024


def _cparams(*sem):
    return pltpu.CompilerParams(dimension_semantics=sem, vmem_limit_bytes=VMEM_LIMIT)


def _const_spec(shape):
    nd = len(shape)
    return pl.BlockSpec(shape, lambda *_: (0,) * nd, pipeline_mode=pl.Buffered(1))


def _pick(n, pref):
    t = min(n, pref)
    while n % t:
        t -= LANES
    return t


def _rms(x):
    return x * lax.rsqrt(jnp.mean(x * x, axis=-1, keepdims=True) + EPS)


def _prenorm(x, g, sh, sc):
    return (_rms(x) * g) * (1.0 + sc) + sh


def _postnorm_residual(x, y, g, gate):
    return x + gate * (_rms(y) * g)


def _dot(a, b):
    return jnp.dot(a, b, preferred_element_type=F32)


def _dot_nt(a, b):
    return lax.dot_general(a, b, (((1,), (1,)), ((), ())), preferred_element_type=F32)


def _adaln_kernel(c_ref, w_ref, b_ref, o_ref):
    c = c_ref[...]
    ca = (c * jax.nn.sigmoid(c)).astype(BF16)
    o_ref[0] = _dot(ca, w_ref[0].astype(BF16)) + b_ref[0]


def _adaln(c_pad, ada_w, ada_b):
    depth, d, n = ada_w.shape
    tn = _pick(n, 1024)
    return pl.pallas_call(
        _adaln_kernel,
        out_shape=jax.ShapeDtypeStruct((depth, SUBLANES, n), F32),
        grid=(depth, n // tn),
        in_specs=[
            pl.BlockSpec((SUBLANES, d), lambda i, j: (0, 0)),
            pl.BlockSpec((1, d, tn), lambda i, j: (i, 0, j)),
            pl.BlockSpec((1, 1, tn), lambda i, j: (i, 0, j)),
        ],
        out_specs=pl.BlockSpec((1, SUBLANES, tn), lambda i, j: (i, 0, j)),
        compiler_params=_cparams("parallel", "parallel"),
        name="adaln",
    )(c_pad, ada_w, ada_b.reshape(depth, 1, n))


class _Mod:
    def __init__(self, mod, layer, batch, seq):
        self.mod = mod
        self.layer = layer
        self.batch = batch
        self.seq = seq
        self.d = mod.shape[-1]

    def spec(self, k, tm, ahead=0):
        layer, tpb = self.layer, self.seq // tm
        last = self.batch * tpb - 1
        return pl.BlockSpec(
            (1, 1, self.d),
            lambda i, *_: ((layer * SUBLANES + jnp.minimum(i + ahead, last) // tpb) * N_MOD + k, 0, 0))


def _x_ahead_spec(tm, d, ntiles, nj):
    return pl.BlockSpec(
        (tm, d), lambda i, j: (jnp.where(j == nj - 1, jnp.minimum(i + 1, ntiles - 1), i), 0))


def _fox_proj_kernel(x_ref, g_ref, sh_ref, sc_ref, shn_ref, scn_ref, w_ref, wf_ref, o_ref, fg_ref,
                     h_ref):
    i, j = pl.program_id(0), pl.program_id(1)
    cur = i % 2

    @pl.when(jnp.logical_and(i == 0, j == 0))
    def _():
        h_ref[0] = _prenorm(x_ref[...], g_ref[...], sh_ref[0], sc_ref[0]).astype(BF16)

    @pl.when(j == 0)
    def _():
        h = h_ref[cur]
        fg_ref[...] = _dot(h, wf_ref[...])
        o_ref[...] = _dot(h, w_ref[...]).astype(o_ref.dtype)

    @pl.when(jnp.logical_and(j > 0, j < pl.num_programs(1) - 1))
    def _():
        o_ref[...] = _dot(h_ref[cur], w_ref[...]).astype(o_ref.dtype)

    @pl.when(j == pl.num_programs(1) - 1)
    def _():
        h_ref[1 - cur] = _prenorm(x_ref[...], g_ref[...], shn_ref[0], scn_ref[0]).astype(BF16)
        o_ref[...] = _dot(h_ref[cur], w_ref[...]).astype(o_ref.dtype)


def _fox_proj(x, gain, mod, w, wf, layer, n, tm, tn):
    t, d = x.shape
    nj = n // tn
    assert nj >= 2
    return pl.pallas_call(
        _fox_proj_kernel,
        out_shape=(jax.ShapeDtypeStruct((t, n), BF16), jax.ShapeDtypeStruct((t, LANES), F32)),
        grid=(t // tm, nj),
        in_specs=[
            _x_ahead_spec(tm, d, t // tm, nj),
            pl.BlockSpec((1, d), lambda i, j: (0, 0)),
            mod.spec(0, tm), mod.spec(1, tm), mod.spec(0, tm, 1), mod.spec(1, tm, 1),
            pl.BlockSpec((None, d, tn), lambda i, j: (layer, 0, j)),
            _const_spec((d, LANES)),
        ],
        out_specs=(pl.BlockSpec((tm, tn), lambda i, j: (i, j)),
                   pl.BlockSpec((tm, LANES), lambda i, j: (i, 0))),
        scratch_shapes=[pltpu.VMEM((2, tm, d), BF16)],
        compiler_params=_cparams("arbitrary", "arbitrary"),
        name="fox_proj",
    )(x, gain, mod.mod, mod.mod, mod.mod, mod.mod, w, wf)


def _swa_proj_kernel(x_ref, g_ref, sh_ref, sc_ref, shn_ref, scn_ref, w_ref, pos_ref, inv_ref,
                     m1_ref, m2_ref, o_ref, h_ref, cos_ref, s1_ref, s2_ref, *, n_rope_tiles, shift):
    i, j = pl.program_id(0), pl.program_id(1)
    cur = i % 2

    def stage(slot, sh, sc):
        h_ref[slot] = _prenorm(x_ref[...], g_ref[...], sh[0], sc[0]).astype(BF16)
        ang = pos_ref[...].astype(F32) * inv_ref[...]
        sn = jnp.sin(ang)
        cos_ref[slot] = jnp.cos(ang)
        s1_ref[slot] = sn * m1_ref[...]
        s2_ref[slot] = sn * m2_ref[...]

    def rope_tile():
        acc = _dot(h_ref[cur], w_ref[...])
        cs, s1, s2 = cos_ref[cur], s1_ref[cur], s2_ref[cur]
        for c in range(acc.shape[1] // LANES):
            a = acc[:, c * LANES:(c + 1) * LANES]
            r = a * cs + pltpu.roll(a, LANES - shift, 1) * s1 + pltpu.roll(a, shift, 1) * s2
            o_ref[:, c * LANES:(c + 1) * LANES] = r.astype(o_ref.dtype)

    @pl.when(jnp.logical_and(i == 0, j == 0))
    def _():
        stage(0, sh_ref, sc_ref)

    @pl.when(j < n_rope_tiles)
    def _():
        rope_tile()

    @pl.when(jnp.logical_and(j >= n_rope_tiles, j < pl.num_programs(1) - 1))
    def _():
        o_ref[...] = _dot(h_ref[cur], w_ref[...]).astype(o_ref.dtype)

    @pl.when(j == pl.num_programs(1) - 1)
    def _():
        stage(1 - cur, shn_ref, scn_ref)
        o_ref[...] = _dot(h_ref[cur], w_ref[...]).astype(o_ref.dtype)


def _swa_proj(x, gain, mod, w, pos, inv_l, m1, m2, tm, tn, n_rope_cols, shift):
    t, d = x.shape
    n = w.shape[1]
    nt, nj = t // tm, n // tn
    assert n_rope_cols // tn < nj
    row = lambda shape: pl.BlockSpec(shape, lambda i, j: (0, 0))
    ahead = lambda i, j: (jnp.where(j == nj - 1, jnp.minimum(i + 1, nt - 1), i), 0)
    return pl.pallas_call(
        functools.partial(_swa_proj_kernel, n_rope_tiles=n_rope_cols // tn, shift=shift),
        out_shape=jax.ShapeDtypeStruct((t, n), BF16),
        grid=(nt, nj),
        in_specs=[
            _x_ahead_spec(tm, d, nt, nj),
            row((1, d)),
            mod.spec(0, tm), mod.spec(1, tm), mod.spec(0, tm, 1), mod.spec(1, tm, 1),
            pl.BlockSpec((d, tn), lambda i, j: (0, j)),
            pl.BlockSpec((tm, 1), ahead),
            row((1, LANES)), row((1, LANES)), row((1, LANES)),
        ],
        out_specs=pl.BlockSpec((tm, tn), lambda i, j: (i, j)),
        scratch_shapes=[pltpu.VMEM((2, tm, d), BF16)] + [pltpu.VMEM((2, tm, LANES), F32)] * 3,
        compiler_params=_cparams("arbitrary", "arbitrary"),
        name="swa_proj",
    )(x, gain, mod.mod, mod.mod, mod.mod, mod.mod, w, pos, inv_l, m1, m2)


def _out_kernel(a_ref, w_ref, x_ref, g_ref, gate_ref, o_ref):
    y = _dot(a_ref[...], w_ref[...])
    o_ref[...] = _postnorm_residual(x_ref[...], y, g_ref[...], gate_ref[0])


def _out_proj(a, w, layer, x, gain, mod, tm):
    t, k = a.shape
    d = w.shape[2]
    return pl.pallas_call(
        _out_kernel,
        out_shape=jax.ShapeDtypeStruct((t, d), F32),
        grid=(t // tm,),
        in_specs=[
            pl.BlockSpec((tm, k), lambda i: (i, 0)),
            pl.BlockSpec((None, k, d), lambda i: (layer, 0, 0), pipeline_mode=pl.Buffered(1)),
            pl.BlockSpec((tm, d), lambda i: (i, 0)),
            pl.BlockSpec((1, d), lambda i: (0, 0)),
            mod.spec(2, tm),
        ],
        out_specs=pl.BlockSpec((tm, d), lambda i: (i, 0)),
        compiler_params=_cparams("parallel"),
        name="out_proj",
    )(a, w, x, gain, mod.mod)


def _ffn_kernel(x_ref, gpre_ref, sh_ref, sc_ref, shn_ref, scn_ref, wg_ref, wu_ref, wd_ref,
                gpost_ref, gate_ref, o_ref, h_ref, xs_ref, acc_ref):
    i, j = pl.program_id(0), pl.program_id(1)
    last = pl.num_programs(1) - 1
    cur = i % 2

    def stage(slot, sh, sc):
        x = x_ref[...]
        xs_ref[slot] = x
        h_ref[slot] = _prenorm(x, gpre_ref[...], sh[0], sc[0]).astype(BF16)

    def chunk():
        h = h_ref[cur]
        g = _dot(h, wg_ref[...])
        u = _dot(h, wu_ref[...])
        a = (g * jax.nn.sigmoid(g) * u).astype(BF16)
        return _dot(a, wd_ref[...])

    @pl.when(jnp.logical_and(i == 0, j == 0))
    def _():
        stage(0, sh_ref, sc_ref)

    @pl.when(j == 0)
    def _():
        acc_ref[...] = chunk()

    @pl.when(jnp.logical_and(j > 0, j < last))
    def _():
        acc_ref[...] += chunk()

    @pl.when(j == last)
    def _():
        stage(1 - cur, shn_ref, scn_ref)
        y = acc_ref[...] + chunk()
        o_ref[...] = _postnorm_residual(xs_ref[cur], y, gpost_ref[...], gate_ref[0])


def _ffn(x, gpre, gpost, mod, w_gu, w_down, layer, tm, tf):
    t, d = x.shape
    dff = w_down.shape[1]
    nf = dff // tf
    assert nf >= 2
    row = lambda: pl.BlockSpec((1, d), lambda i, j: (0, 0))
    return pl.pallas_call(
        _ffn_kernel,
        out_shape=jax.ShapeDtypeStruct((t, d), F32),
        grid=(t // tm, nf),
        in_specs=[
            _x_ahead_spec(tm, d, t // tm, nf),
            row(), mod.spec(3, tm), mod.spec(4, tm), mod.spec(3, tm, 1), mod.spec(4, tm, 1),
            pl.BlockSpec((None, d, tf), lambda i, j: (layer, 0, j)),
            pl.BlockSpec((None, d, tf), lambda i, j: (layer, 0, nf + j)),
            pl.BlockSpec((None, tf, d), lambda i, j: (layer, j, 0)),
            row(), mod.spec(5, tm),
        ],
        out_specs=pl.BlockSpec((tm, d), lambda i, j: (i, 0)),
        scratch_shapes=[pltpu.VMEM((2, tm, d), BF16), pltpu.VMEM((2, tm, d), F32),
                        pltpu.VMEM((tm, d), F32)],
        compiler_params=_cparams("arbitrary", "arbitrary"),
        name="ffn",
    )(x, gpre, mod.mod, mod.mod, mod.mod, mod.mod, w_gu, w_gu, w_down, gpost, mod.mod)


def _gate_kernel(fg_ref, b_ref, cum_ref, carry_ref):
    @pl.when(pl.program_id(1) == 0)
    def _():
        carry_ref[...] = jnp.zeros_like(carry_ref)

    z = fg_ref[...] + b_ref[...]
    lf = jnp.minimum(z, 0.0) - jnp.log1p(jnp.exp(-jnp.abs(z)))
    n = lf.shape[0]
    tril = (lax.broadcasted_iota(jnp.int32, (n, n), 1)
            <= lax.broadcasted_iota(jnp.int32, (n, n), 0)).astype(BF16)
    hi = lf.astype(BF16)
    r1 = lf - hi.astype(F32)
    mid = r1.astype(BF16)
    lo = (r1 - mid.astype(F32)).astype(BF16)
    cum = (_dot(tril, hi) + _dot(tril, mid)) + _dot(tril, lo) + carry_ref[...]
    cum_ref[...] = cum
    carry_ref[...] = cum[n - 1:n, :]


def _gate_cumsum(fg, b_pad, batch, tg):
    t = fg.shape[0]
    s = t // batch
    nt = s // tg
    return pl.pallas_call(
        _gate_kernel,
        out_shape=jax.ShapeDtypeStruct((t, LANES), F32),
        grid=(batch, nt),
        in_specs=[
            pl.BlockSpec((tg, LANES), lambda b, j: (b * nt + j, 0)),
            pl.BlockSpec((1, LANES), lambda b, j: (0, 0)),
        ],
        out_specs=pl.BlockSpec((tg, LANES), lambda b, j: (b * nt + j, 0)),
        scratch_shapes=[pltpu.VMEM((1, LANES), F32)],
        compiler_params=_cparams("parallel", "arbitrary"),
        name="fox_gate",
    )(fg, b_pad)


def _lane_col(x, h):
    lane = lax.broadcasted_iota(jnp.int32, x.shape, 1)
    return jnp.sum(jnp.where(lane == h, x, 0.0), axis=-1, keepdims=True)


def _decay_lanes(f, base):
    hi = f.astype(BF16).astype(F32)
    r = f - hi
    mid = r.astype(BF16).astype(F32)
    lo = (r - mid).astype(BF16).astype(F32)
    lane = lax.broadcasted_iota(jnp.int32, (f.shape[0], LANES), 1)
    ones = jnp.logical_and(lane >= 3 - base, lane < 6 - base)
    out = jnp.where(lane == base, hi, jnp.where(lane == base + 1, mid, jnp.where(
        lane == base + 2, lo, jnp.where(ones, 1.0, 0.0))))
    return out.astype(BF16)


def _fox_attn_kernel(q_ref, k_ref, v_ref, cum_ref, o_ref, ka_ref, vt_ref, qa_ref, kstat_ref,
                     sa_ref, sb_ref, pa_ref, pb_ref, m_ref, l_ref, ala_ref, alb_ref, acc_ref,
                     *, tk, qscale):
    h = pl.program_id(1)
    i = pl.program_id(2)
    tq, dh = q_ref.shape
    nkv = vt_ref.shape[0]
    lane_row = lax.broadcasted_iota(jnp.int32, (1, LANES), 1)

    @pl.when(i == 0)
    def _():
        kn2 = jnp.zeros((1, 1), F32)
        for c in range(nkv):
            rows = slice(c * tk, (c + 1) * tk)
            kc = k_ref[rows, :]
            kf = kc.astype(F32)
            kn2 = jnp.maximum(kn2, jnp.max(jnp.sum(kf * kf, axis=1, keepdims=True),
                                           axis=0, keepdims=True))
            ka_ref[rows, :dh] = kc
            ka_ref[rows, dh:] = _decay_lanes(_lane_col(cum_ref[rows, :], h) * (-LOG2E), 0)
            vt_ref[c] = v_ref[rows, :].T
        stat = jnp.where(lane_row == LANES - 1, kn2, 0.0)
        for k in range(nkv // 2):
            last = (k + 1) * tq - 1
            stat = jnp.where(lane_row == k, _lane_col(cum_ref[last:last + 1, :], h) * LOG2E, stat)
        kstat_ref[...] = stat

    qs = (q_ref[...].astype(F32) * qscale).astype(BF16)
    fq = _lane_col(cum_ref[pl.ds(pl.multiple_of(i * tq, tq), tq), :], h) * LOG2E
    qa_ref[:, :dh] = qs
    qa_ref[:, dh:] = _decay_lanes(fq, 3)
    qf = qs.astype(F32)
    qn2 = jnp.max(jnp.sum(qf * qf, axis=1, keepdims=True), axis=0, keepdims=True)
    fq_max = jnp.max(fq, axis=0, keepdims=True)
    m_ref[...] = jnp.full(m_ref.shape, -jnp.inf, F32)
    l_ref[...] = jnp.zeros(l_ref.shape, F32)
    acc_ref[...] = jnp.zeros(acc_ref.shape, F32)

    def qk(c, s_out):
        ka = ka_ref[pl.ds(pl.multiple_of(c * tk, tk), tk), :]
        s_out[...] = _dot_nt(ka, qa_ref[...])

    def pv(c, p_in, al_in):
        acc_ref[...] = al_in[...] * acc_ref[...] + _dot(vt_ref[c], p_in[...])

    def softmax(s_in, p_out, al_out, key_offset=None):
        rc = SOFTMAX_ROWS
        col = lax.broadcasted_iota(jnp.int32, (rc, tq), 1)
        row = lax.broadcasted_iota(jnp.int32, (rc, tq), 0)

        def chunk(r):
            blk = s_in[r * rc:(r + 1) * rc, :]
            if key_offset is not None:
                blk = jnp.where(row + (r * rc + key_offset) <= col, blk, -jnp.inf)
            return blk

        part = jnp.full((SUBLANES, tq), -jnp.inf, F32)
        for r in range(tk // rc):
            part = jnp.maximum(part, jnp.max(chunk(r).reshape(rc // SUBLANES, SUBLANES, tq), axis=0))
        m = m_ref[...]
        m_new = jnp.maximum(m, jnp.max(part, axis=0, keepdims=True))
        alpha = jnp.exp2(m - m_new)
        part = jnp.zeros((SUBLANES, tq), F32)
        for r in range(tk // rc):
            p = jnp.exp2(chunk(r) - m_new)
            part = part + jnp.sum(p.reshape(rc // SUBLANES, SUBLANES, tq), axis=0)
            p_out[r * rc:(r + 1) * rc, :] = p.astype(BF16)
        l_ref[...] = alpha * l_ref[...] + jnp.sum(part, axis=0, keepdims=True)
        m_ref[...] = m_new
        al_out[...] = alpha

    qk(2 * i, sa_ref)
    qk(2 * i + 1, sb_ref)
    softmax(sa_ref, pa_ref, ala_ref, 0)
    pv(2 * i, pa_ref, ala_ref)
    softmax(sb_ref, pb_ref, alb_ref, tk)

    stat = kstat_ref[...]
    kn2 = jnp.sum(jnp.where(lane_row == LANES - 1, stat, 0.0), axis=1, keepdims=True)
    bound = jnp.sqrt(qn2 * kn2) + fq_max - stat
    m_min = jnp.min(m_ref[...], axis=1, keepdims=True)
    live = jnp.logical_and(lane_row < i, bound >= m_min - SKIP_LOG2)
    first = jnp.min(jnp.where(live, lane_row, i).astype(F32), axis=1, keepdims=True)
    n_old = i - first[0, 0].astype(jnp.int32)

    def body(j, carry):
        k = i - 1 - j
        qk(2 * k, sa_ref)
        qk(2 * k + 1, sb_ref)
        pv(2 * k + 3, pb_ref, alb_ref)
        softmax(sa_ref, pa_ref, ala_ref)
        pv(2 * k, pa_ref, ala_ref)
        softmax(sb_ref, pb_ref, alb_ref)
        return carry

    lax.fori_loop(0, n_old, body, 0)
    pv(2 * (i - n_old) + 1, pb_ref, alb_ref)
    o_ref[...] = (acc_ref[...] / l_ref[...]).T.astype(o_ref.dtype)


def _fox_attn(qkv, cum, batch, heads, tq):
    t = qkv.shape[0]
    dh = qkv.shape[1] // (3 * heads)
    s = t // batch
    nq = s // tq
    tk = tq // 2
    return pl.pallas_call(
        functools.partial(_fox_attn_kernel, tk=tk, qscale=dh ** -0.5 * LOG2E),
        out_shape=jax.ShapeDtypeStruct((t, heads * dh), BF16),
        grid=(batch, heads, nq),
        in_specs=[
            pl.BlockSpec((tq, dh), lambda b, h, i: (b * nq + i, h)),
            pl.BlockSpec((s, dh), lambda b, h, i: (b, heads + h)),
            pl.BlockSpec((s, dh), lambda b, h, i: (b, 2 * heads + h)),
            pl.BlockSpec((s, LANES), lambda b, h, i: (b, 0)),
        ],
        out_specs=pl.BlockSpec((tq, dh), lambda b, h, i: (b * nq + i, h)),
        scratch_shapes=[
            pltpu.VMEM((s, 2 * dh), BF16),
            pltpu.VMEM((s // tk, dh, tk), BF16),
            pltpu.VMEM((tq, 2 * dh), BF16),
            pltpu.VMEM((1, LANES), F32),
            pltpu.VMEM((tk, tq), F32), pltpu.VMEM((tk, tq), F32),
            pltpu.VMEM((tk, tq), BF16), pltpu.VMEM((tk, tq), BF16),
            pltpu.VMEM((1, tq), F32), pltpu.VMEM((1, tq), F32),
            pltpu.VMEM((1, tq), F32), pltpu.VMEM((1, tq), F32),
            pltpu.VMEM((dh, tq), F32),
        ],
        compiler_params=_cparams("parallel", "parallel", "arbitrary"),
        name="fox_attn",
    )(qkv, qkv, qkv, cum)


def _sgu_kernel(x_ref, gpre_ref, sh_ref, sc_ref, win_ref, lng_ref, lnb_ref, ws_ref, bs_ref,
                wout_ref, gpost_ref, gate_ref, o_ref, u_ref, vn_ref, gated_ref):
    x = x_ref[...]
    tm = x.shape[0]
    width = u_ref.shape[1]
    groups, chunk, _ = ws_ref.shape
    gd = width // groups
    h = _prenorm(x, gpre_ref[...], sh_ref[0], sc_ref[0]).astype(BF16)
    u_ref[...] = jax.nn.gelu(_dot(h, win_ref[:, :width]))
    v = jax.nn.gelu(_dot(h, win_ref[:, width:]))
    mu = jnp.mean(v, axis=-1, keepdims=True)
    vc = v - mu
    var = jnp.mean(vc * vc, axis=-1, keepdims=True)
    vn_ref[...] = (vc * lax.rsqrt(var + EPS) * lng_ref[...] + lnb_ref[...]).astype(BF16)
    causal = (lax.broadcasted_iota(jnp.int32, (chunk, chunk), 1)
              <= lax.broadcasted_iota(jnp.int32, (chunk, chunk), 0))
    for g in range(groups):
        wg = jnp.where(causal, ws_ref[g], jnp.zeros((), BF16))
        bias = bs_ref[:, g:g + 1]
        cols = slice(g * gd, (g + 1) * gd)
        for c in range(tm // chunk):
            rows = slice(c * chunk, (c + 1) * chunk)
            f = _dot(wg, vn_ref[rows, cols]) + bias
            gated_ref[rows, cols] = (u_ref[rows, cols] * f).astype(BF16)
    y = _dot(gated_ref[...], wout_ref[...])
    o_ref[...] = _postnorm_residual(x, y, gpost_ref[...], gate_ref[0])


def _sgu(x, gpre, gpost, mod, w_in, ln_g, ln_b, w_s, b_st, w_out, tm):
    t, d = x.shape
    width = w_out.shape[0]
    row = lambda n: pl.BlockSpec((1, n), lambda i: (0, 0))
    return pl.pallas_call(
        _sgu_kernel,
        out_shape=jax.ShapeDtypeStruct((t, d), F32),
        grid=(t // tm,),
        in_specs=[
            pl.BlockSpec((tm, d), lambda i: (i, 0)),
            row(d), mod.spec(0, tm), mod.spec(1, tm),
            _const_spec(w_in.shape),
            row(width), row(width),
            _const_spec(w_s.shape),
            _const_spec(b_st.shape),
            _const_spec(w_out.shape),
            row(d), mod.spec(2, tm),
        ],
        out_specs=pl.BlockSpec((tm, d), lambda i: (i, 0)),
        scratch_shapes=[pltpu.VMEM((tm, width), F32), pltpu.VMEM((tm, width), BF16),
                        pltpu.VMEM((tm, width), BF16)],
        compiler_params=_cparams("parallel"),
        name="sgu",
    )(x, gpre, mod.mod, mod.mod, w_in, ln_g, ln_b, w_s, b_st, w_out, gpost, mod.mod)


def _swa_attn_kernel(sink_ref, q_ref, kc_ref, kp_ref, vc_ref, vp_ref, o_ref, *, nb, grp, scale):
    n = pl.program_id(0) % nb
    blk = q_ref.shape[0]
    half = LANES // 2
    kv_cols = kc_ref.shape[1] // LANES
    npair = grp // 2
    row = lax.broadcasted_iota(jnp.int32, (blk, blk), 0)
    col = lax.broadcasted_iota(jnp.int32, (blk, blk), 1)
    keep_p = jnp.logical_and(row > col, n > 0)
    keep_c = row <= col
    keep = jnp.concatenate([keep_p, keep_c], axis=0)
    keep = jnp.concatenate([keep] * npair, axis=1)
    lo = lax.broadcasted_iota(jnp.int32, (2 * blk, LANES), 1) < half
    scores, vts = {}, {}
    for j in range(kv_cols):
        csl = slice(j * LANES, (j + 1) * LANES)
        kf = jnp.concatenate([kp_ref[:, csl], kc_ref[:, csl]], axis=0).astype(F32) * scale
        kr = pltpu.roll(kf, half, 1)
        vt = jnp.concatenate([vp_ref[:, csl], vc_ref[:, csl]], axis=0).T
        for e in range(2):
            kh = 2 * j + e
            qcols = [q_ref[:, (kh * npair + a) * LANES:(kh * npair + a + 1) * LANES]
                     for a in range(npair)]
            rhs = jnp.concatenate(qcols, axis=0)
            vts[kh] = vt[e * half:(e + 1) * half, :]
            for p in range(2):
                src = kf if p == e else kr
                kz = jnp.where(lo if p == 0 else jnp.logical_not(lo), src, 0.0).astype(BF16)
                scores[kh, p] = _dot_nt(kz, rhs)
    probs = {}
    for (kh, p), st in scores.items():
        st = jnp.where(keep, st, -jnp.inf)
        sink = jnp.concatenate(
            [jnp.full((1, blk), sink_ref[kh * grp + 2 * a + p], F32) for a in range(npair)], axis=1)
        m = jnp.maximum(jnp.max(st, axis=0, keepdims=True), sink)
        pt = jnp.exp(st - m)
        den = jnp.sum(pt, axis=0, keepdims=True) + jnp.exp(sink - m)
        probs[kh, p] = (pt.astype(BF16), den)
    for kh in range(2 * kv_cols):
        outs = [_dot(vts[kh], probs[kh, p][0]) / probs[kh, p][1] for p in range(2)]
        for a in range(npair):
            ot = jnp.concatenate([o[:, a * blk:(a + 1) * blk] for o in outs], axis=0)
            c0 = (kh * npair + a) * LANES
            o_ref[:, c0:c0 + LANES] = ot.T.astype(o_ref.dtype)


def _swa_attn(qkv, sinks, batch, hq, hk, dh):
    t = qkv.shape[0]
    nb = t // batch // BLOCK
    grp = hq // hk
    assert 2 * dh == LANES and grp % 2 == 0 and hk % 2 == 0
    prev = lambda r: jnp.maximum(r - 1, 0)
    return pl.pallas_call(
        functools.partial(_swa_attn_kernel, nb=nb, grp=grp, scale=dh ** -0.5),
        out_shape=jax.ShapeDtypeStruct((t, hq * dh), BF16),
        grid=(t // BLOCK,),
        in_specs=[
            pl.BlockSpec(memory_space=pltpu.SMEM),
            pl.BlockSpec((BLOCK, hq * dh), lambda r: (r, 0)),
            pl.BlockSpec((BLOCK, hk * dh), lambda r: (r, grp)),
            pl.BlockSpec((BLOCK, hk * dh), lambda r: (prev(r), grp)),
            pl.BlockSpec((BLOCK, hk * dh), lambda r: (r, grp + 1)),
            pl.BlockSpec((BLOCK, hk * dh), lambda r: (prev(r), grp + 1)),
        ],
        out_specs=pl.BlockSpec((BLOCK, hq * dh), lambda r: (r, 0)),
        compiler_params=_cparams("parallel"),
        name="swa_attn",
    )(sinks, qkv, qkv, qkv, qkv, qkv)


def _pad_lanes(a, n=LANES):
    return jnp.pad(a, [(0, 0)] * (a.ndim - 1) + [(0, n - a.shape[-1])])


def kernel(x, c, positions, ada_w, ada_b, mix_pre_g, mix_post_g, ffn_pre_g, ffn_post_g, ffn_w_gu, ffn_w_down, fox_w_in, fox_b_f, fox_w_out, sgu_w_in, sgu_ln_g, sgu_ln_b, sgu_w_s, sgu_b_s, sgu_w_out, swa_w_in, swa_sinks, swa_w_out):
    batch, seq, d = x.shape
    depth = ada_w.shape[0]
    t = batch * seq
    assert batch <= SUBLANES and seq % BLOCK == 0 and d % LANES == 0

    tm = _pick(seq, 512)
    tm_proj = _pick(seq, 1024)
    xf = x.reshape(t, d)

    c_pad = jnp.pad(c, ((0, SUBLANES - batch), (0, 0)))
    mod_all = _adaln(c_pad, ada_w, ada_b).reshape(depth * SUBLANES * N_MOD, 1, d)

    ffn_gu, ffn_down = ffn_w_gu.astype(BF16), ffn_w_down.astype(BF16)
    fox_in, fox_out = fox_w_in.astype(BF16), fox_w_out.astype(BF16)
    swa_out = swa_w_out.astype(BF16)

    for i in range(depth):
        kind, j = i % N_MIXERS, i // N_MIXERS
        mod = _Mod(mod_all, i, batch, seq)
        gpre, gpost = mix_pre_g[i].reshape(1, d), mix_post_g[i].reshape(1, d)
        if kind == 0:
            heads = fox_b_f.shape[1]
            nqkv = fox_w_in.shape[2] - heads
            qkv, fg = _fox_proj(xf, gpre, mod, fox_in, _pad_lanes(fox_in[j, :, nqkv:]), j, nqkv,
                                tm_proj, _pick(nqkv, min(1536, nqkv // 2)))
            cum = _gate_cumsum(fg, _pad_lanes(fox_b_f[j].reshape(1, heads)), batch, _pick(seq, 512))
            a = _fox_attn(qkv, cum, batch, heads, _pick(seq, 1024))
            xf = _out_proj(a, fox_out, j, xf, gpost, mod, tm)
        elif kind == 1:
            groups = sgu_w_s.shape[1]
            xf = _sgu(xf, gpre, gpost, mod, sgu_w_in[j].astype(BF16),
                      sgu_ln_g[j].reshape(1, -1), sgu_ln_b[j].reshape(1, -1),
                      sgu_w_s[j].astype(BF16), _pad_lanes(sgu_b_s[j].T),
                      sgu_w_out[j].astype(BF16), _pick(seq, 256))
        else:
            dh = SWA_HEAD_DIM
            hq = swa_sinks.shape[1]
            hk = (swa_w_in.shape[2] // dh - hq) // 2
            rope = dh // 4
            half = rope // 2
            inv = ROPE_THETA ** (-jnp.arange(0, rope, 2, dtype=F32) / rope)
            lane_d = jnp.arange(LANES) % dh
            inv_l = jnp.where(lane_d < rope, inv[lane_d % half], 0.0).reshape(1, LANES).astype(F32)
            m1 = jnp.where(lane_d < half, -1.0, 0.0).reshape(1, LANES).astype(F32)
            m2 = jnp.where((lane_d >= half) & (lane_d < rope), 1.0, 0.0).reshape(1, LANES).astype(F32)
            qkv = _swa_proj(xf, gpre, mod, swa_w_in[j].astype(BF16), positions.reshape(t, 1),
                            inv_l, m1, m2, tm_proj, _pick(hk * dh, 512), (hq + hk) * dh, half)
            a = _swa_attn(qkv, swa_sinks[j], batch, hq, hk, dh)
            xf = _out_proj(a, swa_out, j, xf, gpost, mod, tm)
        xf = _ffn(xf, ffn_pre_g[i].reshape(1, d), ffn_post_g[i].reshape(1, d), mod,
                  ffn_gu, ffn_down, i, tm, _pick(ffn_w_down.shape[1], 512))
    return xf.reshape(batch, seq, d)
```

```python
import functools

import jax
import jax.numpy as jnp
from jax import lax
from jax.experimental import pallas as pl
from jax.experimental.pallas import tpu as pltpu

F32 = jnp.float32
BF16 = jnp.bfloat16

EPS = 1e-6
N_MIXERS = 3
BLOCK = 128
SWA_HEAD_DIM = 64
ROPE_THETA = 500000.0
LANES = 128
SUBLANES = 8
N_MOD = 6
LOG2E = 1.4426950408889634
ONES_ROWS = 16
SOFTMAX_ROWS = 32
SKIP_LOG2 = 160.0
VMEM_LIMIT = 56 * 1024 * 1024


def _cparams(*sem):
    return pltpu.CompilerParams(dimension_semantics=sem, vmem_limit_bytes=VMEM_LIMIT)


def _const_spec(shape):
    nd = len(shape)
    return pl.BlockSpec(shape, lambda *_: (0,) * nd, pipeline_mode=pl.Buffered(1))


def _pick(n, pref):
    t = min(n, pref)
    while n % t:
        t -= LANES
    return t


def _rms(x):
    return x * lax.rsqrt(jnp.mean(x * x, axis=-1, keepdims=True) + EPS)


def _prenorm(x, g, sh, sc):
    return (_rms(x) * g) * (1.0 + sc) + sh


def _postnorm_residual(x, y, g, gate):
    return x + gate * (_rms(y) * g)


def _dot(a, b):
    return jnp.dot(a, b, preferred_element_type=F32)


def _dot_nt(a, b):
    return lax.dot_general(a, b, (((1,), (1,)), ((), ())), preferred_element_type=F32)


def _adaln_kernel(c_ref, w_ref, b_ref, o_ref):
    c = c_ref[...]
    ca = (c * jax.nn.sigmoid(c)).astype(BF16)
    o_ref[0] = _dot(ca, w_ref[0].astype(BF16)) + b_ref[0]


def _adaln(c_pad, ada_w, ada_b):
    depth, d, n = ada_w.shape
    tn = _pick(n, 1024)
    return pl.pallas_call(
        _adaln_kernel,
        out_shape=jax.ShapeDtypeStruct((depth, SUBLANES, n), F32),
        grid=(depth, n // tn),
        in_specs=[
            pl.BlockSpec((SUBLANES, d), lambda i, j: (0, 0)),
            pl.BlockSpec((1, d, tn), lambda i, j: (i, 0, j)),
            pl.BlockSpec((1, 1, tn), lambda i, j: (i, 0, j)),
        ],
        out_specs=pl.BlockSpec((1, SUBLANES, tn), lambda i, j: (i, 0, j)),
        compiler_params=_cparams("parallel", "parallel"),
        name="adaln",
    )(c_pad, ada_w, ada_b.reshape(depth, 1, n))


class _Mod:
    def __init__(self, mod, layer, batch, seq):
        self.mod = mod
        self.layer = layer
        self.batch = batch
        self.seq = seq
        self.d = mod.shape[-1]

    def spec(self, k, tm, ahead=0):
        layer, tpb = self.layer, self.seq // tm
        last = self.batch * tpb - 1
        return pl.BlockSpec(
            (1, 1, self.d),
            lambda i, *_: ((layer * SUBLANES + jnp.minimum(i + ahead, last) // tpb) * N_MOD + k, 0, 0))


def _x_ahead_spec(tm, d, ntiles, nj):
    first_ahead = nj - 2 if nj >= 3 else nj - 1
    return pl.BlockSpec(
        (tm, d), lambda i, j: (jnp.where(j >= first_ahead, jnp.minimum(i + 1, ntiles - 1), i), 0))


def _fox_proj_kernel(x_ref, g_ref, sh_ref, sc_ref, shn_ref, scn_ref, w_ref, wf_ref, o_ref, fg_ref,
                     h_ref):
    i, j = pl.program_id(0), pl.program_id(1)
    cur = i % 2

    @pl.when(jnp.logical_and(i == 0, j == 0))
    def _():
        h_ref[0] = _prenorm(x_ref[...], g_ref[...], sh_ref[0], sc_ref[0]).astype(BF16)

    @pl.when(j == 0)
    def _():
        h = h_ref[cur]
        fg_ref[...] = _dot(h, wf_ref[...])
        o_ref[...] = _dot(h, w_ref[...]).astype(o_ref.dtype)

    @pl.when(jnp.logical_and(j > 0, j < pl.num_programs(1) - 1))
    def _():
        o_ref[...] = _dot(h_ref[cur], w_ref[...]).astype(o_ref.dtype)

    @pl.when(j == pl.num_programs(1) - 1)
    def _():
        h_ref[1 - cur] = _prenorm(x_ref[...], g_ref[...], shn_ref[0], scn_ref[0]).astype(BF16)
        o_ref[...] = _dot(h_ref[cur], w_ref[...]).astype(o_ref.dtype)


def _fox_proj(x, gain, mod, w, wf, layer, n, tm, tn):
    t, d = x.shape
    nj = n // tn
    assert nj >= 2
    return pl.pallas_call(
        _fox_proj_kernel,
        out_shape=(jax.ShapeDtypeStruct((t, n), BF16), jax.ShapeDtypeStruct((t, LANES), F32)),
        grid=(t // tm, nj),
        in_specs=[
            _x_ahead_spec(tm, d, t // tm, nj),
            pl.BlockSpec((1, d), lambda i, j: (0, 0)),
            mod.spec(0, tm), mod.spec(1, tm), mod.spec(0, tm, 1), mod.spec(1, tm, 1),
            pl.BlockSpec((None, d, tn), lambda i, j: (layer, 0, j)),
            _const_spec((d, LANES)),
        ],
        out_specs=(pl.BlockSpec((tm, tn), lambda i, j: (i, j)),
                   pl.BlockSpec((tm, LANES), lambda i, j: (i, 0))),
        scratch_shapes=[pltpu.VMEM((2, tm, d), BF16)],
        compiler_params=_cparams("arbitrary", "arbitrary"),
        name="fox_proj",
    )(x, gain, mod.mod, mod.mod, mod.mod, mod.mod, w, wf)


def _swa_proj_kernel(x_ref, g_ref, sh_ref, sc_ref, shn_ref, scn_ref, w_ref, pos_ref, inv_ref,
                     m1_ref, m2_ref, o_ref, h_ref, cos_ref, s1_ref, s2_ref, *, n_rope_tiles, shift):
    i, j = pl.program_id(0), pl.program_id(1)
    cur = i % 2

    def stage(slot, sh, sc):
        h_ref[slot] = _prenorm(x_ref[...], g_ref[...], sh[0], sc[0]).astype(BF16)
        ang = pos_ref[...].astype(F32) * inv_ref[...]
        sn = jnp.sin(ang)
        cos_ref[slot] = jnp.cos(ang)
        s1_ref[slot] = sn * m1_ref[...]
        s2_ref[slot] = sn * m2_ref[...]

    def rope_tile():
        acc = _dot(h_ref[cur], w_ref[...])
        cs, s1, s2 = cos_ref[cur], s1_ref[cur], s2_ref[cur]
        for c in range(acc.shape[1] // LANES):
            a = acc[:, c * LANES:(c + 1) * LANES]
            r = a * cs + pltpu.roll(a, LANES - shift, 1) * s1 + pltpu.roll(a, shift, 1) * s2
            o_ref[:, c * LANES:(c + 1) * LANES] = r.astype(o_ref.dtype)

    @pl.when(jnp.logical_and(i == 0, j == 0))
    def _():
        stage(0, sh_ref, sc_ref)

    @pl.when(j < n_rope_tiles)
    def _():
        rope_tile()

    @pl.when(jnp.logical_and(j >= n_rope_tiles, j < pl.num_programs(1) - 1))
    def _():
        o_ref[...] = _dot(h_ref[cur], w_ref[...]).astype(o_ref.dtype)

    @pl.when(j == pl.num_programs(1) - 1)
    def _():
        stage(1 - cur, shn_ref, scn_ref)
        o_ref[...] = _dot(h_ref[cur], w_ref[...]).astype(o_ref.dtype)


def _swa_proj(x, gain, mod, w, pos, inv_l, m1, m2, tm, tn, n_rope_cols, shift):
    t, d = x.shape
    n = w.shape[1]
    nt, nj = t // tm, n // tn
    assert n_rope_cols // tn < nj
    row = lambda shape: pl.BlockSpec(shape, lambda i, j: (0, 0))
    ahead = lambda i, j: (jnp.where(j == nj - 1, jnp.minimum(i + 1, nt - 1), i), 0)
    return pl.pallas_call(
        functools.partial(_swa_proj_kernel, n_rope_tiles=n_rope_cols // tn, shift=shift),
        out_shape=jax.ShapeDtypeStruct((t, n), BF16),
        grid=(nt, nj),
        in_specs=[
            _x_ahead_spec(tm, d, nt, nj),
            row((1, d)),
            mod.spec(0, tm), mod.spec(1, tm), mod.spec(0, tm, 1), mod.spec(1, tm, 1),
            pl.BlockSpec((d, tn), lambda i, j: (0, j)),
            pl.BlockSpec((tm, 1), ahead),
            row((1, LANES)), row((1, LANES)), row((1, LANES)),
        ],
        out_specs=pl.BlockSpec((tm, tn), lambda i, j: (i, j)),
        scratch_shapes=[pltpu.VMEM((2, tm, d), BF16)] + [pltpu.VMEM((2, tm, LANES), F32)] * 3,
        compiler_params=_cparams("arbitrary", "arbitrary"),
        name="swa_proj",
    )(x, gain, mod.mod, mod.mod, mod.mod, mod.mod, w, pos, inv_l, m1, m2)


def _out_kernel(a_ref, w_ref, x_ref, g_ref, gate_ref, o_ref):
    y = _dot(a_ref[...], w_ref[...])
    o_ref[...] = _postnorm_residual(x_ref[...], y, g_ref[...], gate_ref[0])


def _out_proj(a, w, layer, x, gain, mod, tm):
    t, k = a.shape
    d = w.shape[2]
    return pl.pallas_call(
        _out_kernel,
        out_shape=jax.ShapeDtypeStruct((t, d), F32),
        grid=(t // tm,),
        in_specs=[
            pl.BlockSpec((tm, k), lambda i: (i, 0)),
            pl.BlockSpec((None, k, d), lambda i: (layer, 0, 0), pipeline_mode=pl.Buffered(1)),
            pl.BlockSpec((tm, d), lambda i: (i, 0)),
            pl.BlockSpec((1, d), lambda i: (0, 0)),
            mod.spec(2, tm),
        ],
        out_specs=pl.BlockSpec((tm, d), lambda i: (i, 0)),
        compiler_params=_cparams("parallel"),
        name="out_proj",
    )(a, w, x, gain, mod.mod)


def _ffn_kernel(x_ref, gpre_ref, sh_ref, sc_ref, shn_ref, scn_ref, wg_ref, wu_ref, wd_ref,
                gpost_ref, gate_ref, o_ref, h_ref, xs_ref, acc_ref):
    i, j = pl.program_id(0), pl.program_id(1)
    last = pl.num_programs(1) - 1
    cur = i % 2

    def stage(slot, sh, sc):
        x = x_ref[...]
        xs_ref[slot] = x
        h_ref[slot] = _prenorm(x, gpre_ref[...], sh[0], sc[0]).astype(BF16)

    def chunk():
        h = h_ref[cur]
        g = _dot(h, wg_ref[...])
        u = _dot(h, wu_ref[...])
        a = (g * jax.nn.sigmoid(g) * u).astype(BF16)
        return _dot(a, wd_ref[...])

    @pl.when(jnp.logical_and(i == 0, j == 0))
    def _():
        stage(0, sh_ref, sc_ref)

    @pl.when(j == 0)
    def _():
        acc_ref[...] = chunk()

    @pl.when(jnp.logical_and(j > 0, j < last))
    def _():
        acc_ref[...] += chunk()

    @pl.when(j == last)
    def _():
        stage(1 - cur, shn_ref, scn_ref)
        y = acc_ref[...] + chunk()
        o_ref[...] = _postnorm_residual(xs_ref[cur], y, gpost_ref[...], gate_ref[0])


def _ffn(x, gpre, gpost, mod, w_gu, w_down, layer, tm, tf):
    t, d = x.shape
    dff = w_down.shape[1]
    nf = dff // tf
    assert nf >= 2
    row = lambda: pl.BlockSpec((1, d), lambda i, j: (0, 0))
    return pl.pallas_call(
        _ffn_kernel,
        out_shape=jax.ShapeDtypeStruct((t, d), F32),
        grid=(t // tm, nf),
        in_specs=[
            _x_ahead_spec(tm, d, t // tm, nf),
            row(), mod.spec(3, tm), mod.spec(4, tm), mod.spec(3, tm, 1), mod.spec(4, tm, 1),
            pl.BlockSpec((None, d, tf), lambda i, j: (layer, 0, j)),
            pl.BlockSpec((None, d, tf), lambda i, j: (layer, 0, nf + j)),
            pl.BlockSpec((None, tf, d), lambda i, j: (layer, j, 0)),
            row(), mod.spec(5, tm),
        ],
        out_specs=pl.BlockSpec((tm, d), lambda i, j: (i, 0)),
        scratch_shapes=[pltpu.VMEM((2, tm, d), BF16), pltpu.VMEM((2, tm, d), F32),
                        pltpu.VMEM((tm, d), F32)],
        compiler_params=_cparams("arbitrary", "arbitrary"),
        name="ffn",
    )(x, gpre, mod.mod, mod.mod, mod.mod, mod.mod, w_gu, w_gu, w_down, gpost, mod.mod)


def _gate_kernel(fg_ref, b_ref, cum_ref, carry_ref):
    @pl.when(pl.program_id(1) == 0)
    def _():
        carry_ref[...] = jnp.zeros_like(carry_ref)

    z = fg_ref[...] + b_ref[...]
    lf = jnp.minimum(z, 0.0) - jnp.log1p(jnp.exp(-jnp.abs(z)))
    n = lf.shape[0]
    tril = (lax.broadcasted_iota(jnp.int32, (n, n), 1)
            <= lax.broadcasted_iota(jnp.int32, (n, n), 0)).astype(BF16)
    hi = lf.astype(BF16)
    r1 = lf - hi.astype(F32)
    mid = r1.astype(BF16)
    lo = (r1 - mid.astype(F32)).astype(BF16)
    cum = (_dot(tril, hi) + _dot(tril, mid)) + _dot(tril, lo) + carry_ref[...]
    cum_ref[...] = cum
    carry_ref[...] = cum[n - 1:n, :]


def _gate_cumsum(fg, b_pad, batch, tg):
    t = fg.shape[0]
    s = t // batch
    nt = s // tg
    return pl.pallas_call(
        _gate_kernel,
        out_shape=jax.ShapeDtypeStruct((t, LANES), F32),
        grid=(batch, nt),
        in_specs=[
            pl.BlockSpec((tg, LANES), lambda b, j: (b * nt + j, 0)),
            pl.BlockSpec((1, LANES), lambda b, j: (0, 0)),
        ],
        out_specs=pl.BlockSpec((tg, LANES), lambda b, j: (b * nt + j, 0)),
        scratch_shapes=[pltpu.VMEM((1, LANES), F32)],
        compiler_params=_cparams("parallel", "arbitrary"),
        name="fox_gate",
    )(fg, b_pad)


def _lane_col(x, h):
    lane = lax.broadcasted_iota(jnp.int32, x.shape, 1)
    return jnp.sum(jnp.where(lane == h, x, 0.0), axis=-1, keepdims=True)


def _decay_lanes(f, base):
    hi = f.astype(BF16).astype(F32)
    r = f - hi
    mid = r.astype(BF16).astype(F32)
    lo = (r - mid).astype(BF16).astype(F32)
    lane = lax.broadcasted_iota(jnp.int32, (f.shape[0], LANES), 1)
    ones = jnp.logical_and(lane >= 3 - base, lane < 6 - base)
    out = jnp.where(lane == base, hi, jnp.where(lane == base + 1, mid, jnp.where(
        lane == base + 2, lo, jnp.where(ones, 1.0, 0.0))))
    return out.astype(BF16)


def _fox_attn_kernel(q_ref, k_ref, v_ref, cum_ref, o_ref, ka_ref, vt_ref, qa_ref, kstat_ref,
                     sa_ref, sb_ref, pa_ref, pb_ref, m_ref, ala_ref, alb_ref, acc_ref,
                     *, tk, qscale):
    h = pl.program_id(1)
    i = pl.program_id(2)
    tq, dh = q_ref.shape
    nkv = vt_ref.shape[0]
    lane_row = lax.broadcasted_iota(jnp.int32, (1, LANES), 1)

    @pl.when(i == 0)
    def _():
        kn2 = jnp.zeros((1, 1), F32)
        for c in range(nkv):
            rows = slice(c * tk, (c + 1) * tk)
            kc = k_ref[rows, :]
            kf = kc.astype(F32)
            kn2 = jnp.maximum(kn2, jnp.max(jnp.sum(kf * kf, axis=1, keepdims=True),
                                           axis=0, keepdims=True))
            ka_ref[rows, :dh] = kc
            ka_ref[rows, dh:] = _decay_lanes(_lane_col(cum_ref[rows, :], h) * (-LOG2E), 0)
            vt_ref[c, :dh, :] = v_ref[rows, :].T
            vt_ref[c, dh:, :] = jnp.ones((vt_ref.shape[1] - dh, tk), BF16)
        stat = jnp.where(lane_row == LANES - 1, kn2, 0.0)
        for k in range(nkv // 2):
            last = (k + 1) * tq - 1
            stat = jnp.where(lane_row == k, _lane_col(cum_ref[last:last + 1, :], h) * LOG2E, stat)
        kstat_ref[...] = stat

    qs = (q_ref[...].astype(F32) * qscale).astype(BF16)
    fq = _lane_col(cum_ref[pl.ds(pl.multiple_of(i * tq, tq), tq), :], h) * LOG2E
    qa_ref[:, :dh] = qs
    qa_ref[:, dh:] = _decay_lanes(fq, 3)
    qf = qs.astype(F32)
    qn2 = jnp.max(jnp.sum(qf * qf, axis=1, keepdims=True), axis=0, keepdims=True)
    fq_max = jnp.max(fq, axis=0, keepdims=True)
    m_ref[...] = jnp.full(m_ref.shape, -jnp.inf, F32)
    acc_ref[...] = jnp.zeros(acc_ref.shape, F32)

    def qk(c, s_out):
        ka = ka_ref[pl.ds(pl.multiple_of(c * tk, tk), tk), :]
        s_out[...] = _dot_nt(ka, qa_ref[...])

    def pv(c, p_in, al_in):
        acc_ref[...] = al_in[...] * acc_ref[...] + _dot(vt_ref[c], p_in[...])

    def softmax(s_in, p_out, al_out, key_offset=None):
        rc = SOFTMAX_ROWS
        col = lax.broadcasted_iota(jnp.int32, (rc, tq), 1)
        row = lax.broadcasted_iota(jnp.int32, (rc, tq), 0)

        def chunk(r):
            blk = s_in[r * rc:(r + 1) * rc, :]
            if key_offset is not None:
                blk = jnp.where(row + (r * rc + key_offset) <= col, blk, -jnp.inf)
            return blk

        part = jnp.full((SUBLANES, tq), -jnp.inf, F32)
        for r in range(tk // rc):
            part = jnp.maximum(part, jnp.max(chunk(r).reshape(rc // SUBLANES, SUBLANES, tq), axis=0))
        m = m_ref[...]
        m_new = jnp.maximum(m, jnp.max(part, axis=0, keepdims=True))
        for r in range(tk // rc):
            p_out[r * rc:(r + 1) * rc, :] = jnp.exp2(chunk(r) - m_new).astype(BF16)
        m_ref[...] = m_new
        al_out[...] = jnp.exp2(m - m_new)

    qk(2 * i, sa_ref)
    qk(2 * i + 1, sb_ref)
    softmax(sa_ref, pa_ref, ala_ref, 0)
    pv(2 * i, pa_ref, ala_ref)
    softmax(sb_ref, pb_ref, alb_ref, tk)

    stat = kstat_ref[...]
    kn2 = jnp.sum(jnp.where(lane_row == LANES - 1, stat, 0.0), axis=1, keepdims=True)
    bound = jnp.sqrt(qn2 * kn2) + fq_max - stat
    m_min = jnp.min(m_ref[...], axis=1, keepdims=True)
    live = jnp.logical_and(lane_row < i, bound >= m_min - SKIP_LOG2)
    first = jnp.min(jnp.where(live, lane_row, i).astype(F32), axis=1, keepdims=True)
    n_old = i - first[0, 0].astype(jnp.int32)

    def body(j, carry):
        k = i - 1 - j
        qk(2 * k, sa_ref)
        qk(2 * k + 1, sb_ref)
        pv(2 * k + 3, pb_ref, alb_ref)
        softmax(sa_ref, pa_ref, ala_ref)
        pv(2 * k, pa_ref, ala_ref)
        softmax(sb_ref, pb_ref, alb_ref)
        return carry

    lax.fori_loop(0, n_old, body, 0)
    pv(2 * (i - n_old) + 1, pb_ref, alb_ref)
    o_ref[...] = (acc_ref[:dh, :] / acc_ref[dh:dh + 1, :]).T.astype(o_ref.dtype)


def _fox_attn(qkv, cum, batch, heads, tq):
    t = qkv.shape[0]
    dh = qkv.shape[1] // (3 * heads)
    s = t // batch
    nq = s // tq
    tk = tq // 2
    return pl.pallas_call(
        functools.partial(_fox_attn_kernel, tk=tk, qscale=dh ** -0.5 * LOG2E),
        out_shape=jax.ShapeDtypeStruct((t, heads * dh), BF16),
        grid=(batch, heads, nq),
        in_specs=[
            pl.BlockSpec((tq, dh), lambda b, h, i: (b * nq + i, h)),
            pl.BlockSpec((s, dh), lambda b, h, i: (b, heads + h)),
            pl.BlockSpec((s, dh), lambda b, h, i: (b, 2 * heads + h)),
            pl.BlockSpec((s, LANES), lambda b, h, i: (b, 0)),
        ],
        out_specs=pl.BlockSpec((tq, dh), lambda b, h, i: (b * nq + i, h)),
        scratch_shapes=[
            pltpu.VMEM((s, 2 * dh), BF16),
            pltpu.VMEM((s // tk, dh + ONES_ROWS, tk), BF16),
            pltpu.VMEM((tq, 2 * dh), BF16),
            pltpu.VMEM((1, LANES), F32),
            pltpu.VMEM((tk, tq), F32), pltpu.VMEM((tk, tq), F32),
            pltpu.VMEM((tk, tq), BF16), pltpu.VMEM((tk, tq), BF16),
            pltpu.VMEM((1, tq), F32),
            pltpu.VMEM((1, tq), F32), pltpu.VMEM((1, tq), F32),
            pltpu.VMEM((dh + ONES_ROWS, tq), F32),
        ],
        compiler_params=_cparams("parallel", "parallel", "arbitrary"),
        name="fox_attn",
    )(qkv, qkv, qkv, cum)


def _sgu_kernel(x_ref, gpre_ref, sh_ref, sc_ref, win_ref, lng_ref, lnb_ref, ws_ref, bs_ref,
                wout_ref, gpost_ref, gate_ref, o_ref, u_ref, vn_ref, gated_ref):
    x = x_ref[...]
    tm = x.shape[0]
    width = u_ref.shape[1]
    groups, chunk, _ = ws_ref.shape
    gd = width // groups
    h = _prenorm(x, gpre_ref[...], sh_ref[0], sc_ref[0]).astype(BF16)
    u_ref[...] = jax.nn.gelu(_dot(h, win_ref[:, :width]))
    v = jax.nn.gelu(_dot(h, win_ref[:, width:]))
    mu = jnp.mean(v, axis=-1, keepdims=True)
    vc = v - mu
    var = jnp.mean(vc * vc, axis=-1, keepdims=True)
    vn_ref[...] = (vc * lax.rsqrt(var + EPS) * lng_ref[...] + lnb_ref[...]).astype(BF16)
    causal = (lax.broadcasted_iota(jnp.int32, (chunk, chunk), 1)
              <= lax.broadcasted_iota(jnp.int32, (chunk, chunk), 0))
    for g in range(groups):
        wg = jnp.where(causal, ws_ref[g], jnp.zeros((), BF16))
        bias = bs_ref[:, g:g + 1]
        cols = slice(g * gd, (g + 1) * gd)
        for c in range(tm // chunk):
            rows = slice(c * chunk, (c + 1) * chunk)
            f = _dot(wg, vn_ref[rows, cols]) + bias
            gated_ref[rows, cols] = (u_ref[rows, cols] * f).astype(BF16)
    y = _dot(gated_ref[...], wout_ref[...])
    o_ref[...] = _postnorm_residual(x, y, gpost_ref[...], gate_ref[0])


def _sgu(x, gpre, gpost, mod, w_in, ln_g, ln_b, w_s, b_st, w_out, tm):
    t, d = x.shape
    width = w_out.shape[0]
    row = lambda n: pl.BlockSpec((1, n), lambda i: (0, 0))
    return pl.pallas_call(
        _sgu_kernel,
        out_shape=jax.ShapeDtypeStruct((t, d), F32),
        grid=(t // tm,),
        in_specs=[
            pl.BlockSpec((tm, d), lambda i: (i, 0)),
            row(d), mod.spec(0, tm), mod.spec(1, tm),
            _const_spec(w_in.shape),
            row(width), row(width),
            _const_spec(w_s.shape),
            _const_spec(b_st.shape),
            _const_spec(w_out.shape),
            row(d), mod.spec(2, tm),
        ],
        out_specs=pl.BlockSpec((tm, d), lambda i: (i, 0)),
        scratch_shapes=[pltpu.VMEM((tm, width), F32), pltpu.VMEM((tm, width), BF16),
                        pltpu.VMEM((tm, width), BF16)],
        compiler_params=_cparams("parallel"),
        name="sgu",
    )(x, gpre, mod.mod, mod.mod, w_in, ln_g, ln_b, w_s, b_st, w_out, gpost, mod.mod)


def _swa_attn_kernel(sink_ref, q_ref, kc_ref, kp_ref, vc_ref, vp_ref, o_ref, *, nb, grp, scale):
    n = pl.program_id(0) % nb
    blk = q_ref.shape[0]
    half = LANES // 2
    kv_cols = kc_ref.shape[1] // LANES
    npair = grp // 2
    row = lax.broadcasted_iota(jnp.int32, (blk, blk), 0)
    col = lax.broadcasted_iota(jnp.int32, (blk, blk), 1)
    keep_p = jnp.logical_and(row > col, n > 0)
    keep_c = row <= col
    keep = jnp.concatenate([keep_p, keep_c], axis=0)
    keep = jnp.concatenate([keep] * npair, axis=1)
    lo = lax.broadcasted_iota(jnp.int32, (2 * blk, LANES), 1) < half
    scores, vts = {}, {}
    for j in range(kv_cols):
        csl = slice(j * LANES, (j + 1) * LANES)
        kf = jnp.concatenate([kp_ref[:, csl], kc_ref[:, csl]], axis=0).astype(F32) * scale
        kr = pltpu.roll(kf, half, 1)
        vt = jnp.concatenate([vp_ref[:, csl], vc_ref[:, csl]], axis=0).T
        for e in range(2):
            kh = 2 * j + e
            qcols = [q_ref[:, (kh * npair + a) * LANES:(kh * npair + a + 1) * LANES]
                     for a in range(npair)]
            rhs = jnp.concatenate(qcols, axis=0)
            vts[kh] = vt[e * half:(e + 1) * half, :]
            for p in range(2):
                src = kf if p == e else kr
                kz = jnp.where(lo if p == 0 else jnp.logical_not(lo), src, 0.0).astype(BF16)
                scores[kh, p] = _dot_nt(kz, rhs)
    probs = {}
    for (kh, p), st in scores.items():
        st = jnp.where(keep, st, -jnp.inf)
        sink = jnp.concatenate(
            [jnp.full((1, blk), sink_ref[kh * grp + 2 * a + p], F32) for a in range(npair)], axis=1)
        m = jnp.maximum(jnp.max(st, axis=0, keepdims=True), sink)
        pt = jnp.exp(st - m)
        den = jnp.sum(pt, axis=0, keepdims=True) + jnp.exp(sink - m)
        probs[kh, p] = (pt.astype(BF16), den)
    for kh in range(2 * kv_cols):
        outs = [_dot(vts[kh], probs[kh, p][0]) / probs[kh, p][1] for p in range(2)]
        for a in range(npair):
            ot = jnp.concatenate([o[:, a * blk:(a + 1) * blk] for o in outs], axis=0)
            c0 = (kh * npair + a) * LANES
            o_ref[:, c0:c0 + LANES] = ot.T.astype(o_ref.dtype)


def _swa_attn(qkv, sinks, batch, hq, hk, dh):
    t = qkv.shape[0]
    nb = t // batch // BLOCK
    grp = hq // hk
    assert 2 * dh == LANES and grp % 2 == 0 and hk % 2 == 0
    prev = lambda r: jnp.maximum(r - 1, 0)
    return pl.pallas_call(
        functools.partial(_swa_attn_kernel, nb=nb, grp=grp, scale=dh ** -0.5),
        out_shape=jax.ShapeDtypeStruct((t, hq * dh), BF16),
        grid=(t // BLOCK,),
        in_specs=[
            pl.BlockSpec(memory_space=pltpu.SMEM),
            pl.BlockSpec((BLOCK, hq * dh), lambda r: (r, 0)),
            pl.BlockSpec((BLOCK, hk * dh), lambda r: (r, grp)),
            pl.BlockSpec((BLOCK, hk * dh), lambda r: (prev(r), grp)),
            pl.BlockSpec((BLOCK, hk * dh), lambda r: (r, grp + 1)),
            pl.BlockSpec((BLOCK, hk * dh), lambda r: (prev(r), grp + 1)),
        ],
        out_specs=pl.BlockSpec((BLOCK, hq * dh), lambda r: (r, 0)),
        compiler_params=_cparams("parallel"),
        name="swa_attn",
    )(sinks, qkv, qkv, qkv, qkv, qkv)


def _pad_lanes(a, n=LANES):
    return jnp.pad(a, [(0, 0)] * (a.ndim - 1) + [(0, n - a.shape[-1])])


def kernel(x, c, positions, ada_w, ada_b, mix_pre_g, mix_post_g, ffn_pre_g, ffn_post_g, ffn_w_gu, ffn_w_down, fox_w_in, fox_b_f, fox_w_out, sgu_w_in, sgu_ln_g, sgu_ln_b, sgu_w_s, sgu_b_s, sgu_w_out, swa_w_in, swa_sinks, swa_w_out):
    batch, seq, d = x.shape
    depth = ada_w.shape[0]
    t = batch * seq
    assert batch <= SUBLANES and seq % BLOCK == 0 and d % LANES == 0

    tm = _pick(seq, 512)
    tm_proj = _pick(seq, 1024)
    xf = x.reshape(t, d)

    c_pad = jnp.pad(c, ((0, SUBLANES - batch), (0, 0)))
    mod_all = _adaln(c_pad, ada_w, ada_b).reshape(depth * SUBLANES * N_MOD, 1, d)

    dff = ffn_w_down.shape[1]
    tf = _pick(dff, 512)
    ffn_gu, ffn_down = ffn_w_gu.astype(BF16), ffn_w_down.astype(BF16)
    fox_in, fox_out = fox_w_in.astype(BF16), fox_w_out.astype(BF16)
    swa_out = swa_w_out.astype(BF16)

    for i in range(depth):
        kind, j = i % N_MIXERS, i // N_MIXERS
        mod = _Mod(mod_all, i, batch, seq)
        gpre, gpost = mix_pre_g[i].reshape(1, d), mix_post_g[i].reshape(1, d)
        if kind == 0:
            heads = fox_b_f.shape[1]
            nqkv = fox_w_in.shape[2] - heads
            qkv, fg = _fox_proj(xf, gpre, mod, fox_in, _pad_lanes(fox_in[j, :, nqkv:]), j, nqkv,
                                tm_proj, _pick(nqkv, min(1536, nqkv // 2)))
            cum = _gate_cumsum(fg, _pad_lanes(fox_b_f[j].reshape(1, heads)), batch, _pick(seq, 512))
            a = _fox_attn(qkv, cum, batch, heads, _pick(seq, 1024))
            xf = _out_proj(a, fox_out, j, xf, gpost, mod, tm)
        elif kind == 1:
            groups = sgu_w_s.shape[1]
            xf = _sgu(xf, gpre, gpost, mod, sgu_w_in[j].astype(BF16),
                      sgu_ln_g[j].reshape(1, -1), sgu_ln_b[j].reshape(1, -1),
                      sgu_w_s[j].astype(BF16), _pad_lanes(sgu_b_s[j].T),
                      sgu_w_out[j].astype(BF16), _pick(seq, 256))
        else:
            dh = SWA_HEAD_DIM
            hq = swa_sinks.shape[1]
            hk = (swa_w_in.shape[2] // dh - hq) // 2
            rope = dh // 4
            half = rope // 2
            inv = ROPE_THETA ** (-jnp.arange(0, rope, 2, dtype=F32) / rope)
            lane_d = jnp.arange(LANES) % dh
            inv_l = jnp.where(lane_d < rope, inv[lane_d % half], 0.0).reshape(1, LANES).astype(F32)
            m1 = jnp.where(lane_d < half, -1.0, 0.0).reshape(1, LANES).astype(F32)
            m2 = jnp.where((lane_d >= half) & (lane_d < rope), 1.0, 0.0).reshape(1, LANES).astype(F32)
            qkv = _swa_proj(xf, gpre, mod, swa_w_in[j].astype(BF16), positions.reshape(t, 1),
                            inv_l, m1, m2, tm_proj, _pick(hk * dh, 512), (hq + hk) * dh, half)
            a = _swa_attn(qkv, swa_sinks[j], batch, hq, hk, dh)
            xf = _out_proj(a, swa_out, j, xf, gpost, mod, tm)
        xf = _ffn(xf, ffn_pre_g[i].reshape(1, d), ffn_post_g[i].reshape(1, d), mod,
                  ffn_gu, ffn_down, i, tm, tf)
    return xf.reshape(batch, seq, d)
```

```python
import functools

import jax
import jax.numpy as jnp
from jax import lax
from jax.experimental import pallas as pl
from jax.experimental.pallas import tpu as pltpu

F32 = jnp.float32
BF16 = jnp.bfloat16

EPS = 1e-6
N_MIXERS = 3
BLOCK = 128
SWA_HEAD_DIM = 64
ROPE_THETA = 500000.0
LANES = 128
SUBLANES = 8
N_MOD = 6
LOG2E = 1.4426950408889634
ONES_ROWS = 16
SOFTMAX_ROWS = 32
SKIP_LOG2 = 160.0
VMEM_LIMIT = 56 * 1024 * 1024


def _cparams(*sem):
    return pltpu.CompilerParams(dimension_semantics=sem, vmem_limit_bytes=VMEM_LIMIT)


def _const_spec(shape):
    nd = len(shape)
    return pl.BlockSpec(shape, lambda *_: (0,) * nd, pipeline_mode=pl.Buffered(1))


def _pick(n, pref):
    t = min(n, pref)
    while n % t:
        t -= LANES
    return t


def _rms(x):
    return x * lax.rsqrt(jnp.mean(x * x, axis=-1, keepdims=True) + EPS)


def _prenorm(x, g, sh, sc):
    return _rms(x) * (g * (1.0 + sc)) + sh


def _postnorm_residual(x, y, g, gate):
    return x + _rms(y) * (gate * g)


def _dot(a, b):
    return jnp.dot(a, b, preferred_element_type=F32)


def _dot_nt(a, b):
    return lax.dot_general(a, b, (((1,), (1,)), ((), ())), preferred_element_type=F32)


def _adaln_kernel(c_ref, w_ref, b_ref, o_ref):
    c = c_ref[...]
    ca = (c * jax.nn.sigmoid(c)).astype(BF16)
    o_ref[0] = _dot(ca, w_ref[0].astype(BF16)) + b_ref[0]


def _adaln(c_pad, ada_w, ada_b):
    depth, d, n = ada_w.shape
    tn = _pick(n, 1024)
    return pl.pallas_call(
        _adaln_kernel,
        out_shape=jax.ShapeDtypeStruct((depth, SUBLANES, n), F32),
        grid=(depth, n // tn),
        in_specs=[
            pl.BlockSpec((SUBLANES, d), lambda i, j: (0, 0)),
            pl.BlockSpec((1, d, tn), lambda i, j: (i, 0, j)),
            pl.BlockSpec((1, 1, tn), lambda i, j: (i, 0, j)),
        ],
        out_specs=pl.BlockSpec((1, SUBLANES, tn), lambda i, j: (i, 0, j)),
        compiler_params=_cparams("parallel", "parallel"),
        name="adaln",
    )(c_pad, ada_w, ada_b.reshape(depth, 1, n))


class _Mod:
    def __init__(self, mod, layer, batch, seq):
        self.mod = mod
        self.layer = layer
        self.batch = batch
        self.seq = seq
        self.d = mod.shape[-1]

    def spec(self, k, tm, ahead=0):
        layer, tpb = self.layer, self.seq // tm
        last = self.batch * tpb - 1
        return pl.BlockSpec(
            (1, 1, self.d),
            lambda i, *_: ((layer * SUBLANES + jnp.clip(i + ahead, 0, last) // tpb) * N_MOD + k, 0, 0))


def _x_ahead_spec(tm, d, ntiles, nj):
    first_ahead = nj - 2 if nj >= 3 else nj - 1
    return pl.BlockSpec(
        (tm, d), lambda i, j: (jnp.minimum(jnp.where(j >= first_ahead, i + 1, i), ntiles - 1), 0))


def _fox_proj_kernel(x_ref, g_ref, sh_ref, sc_ref, shn_ref, scn_ref, w_ref, wf_ref, o_ref, fg_ref,
                     h_ref):
    i, j = pl.program_id(0), pl.program_id(1)
    cur = i % 2

    @pl.when(jnp.logical_and(i == 0, j == 0))
    def _():
        h_ref[0] = _prenorm(x_ref[...], g_ref[...], sh_ref[0], sc_ref[0]).astype(BF16)

    @pl.when(j == 0)
    def _():
        h = h_ref[cur]
        fg_ref[...] = _dot(h, wf_ref[...])
        o_ref[...] = _dot(h, w_ref[...]).astype(o_ref.dtype)

    @pl.when(jnp.logical_and(j > 0, j < pl.num_programs(1) - 1))
    def _():
        o_ref[...] = _dot(h_ref[cur], w_ref[...]).astype(o_ref.dtype)

    @pl.when(j == pl.num_programs(1) - 1)
    def _():
        h_ref[1 - cur] = _prenorm(x_ref[...], g_ref[...], shn_ref[0], scn_ref[0]).astype(BF16)
        o_ref[...] = _dot(h_ref[cur], w_ref[...]).astype(o_ref.dtype)


def _fox_proj(x, gain, mod, w, wf, layer, n, tm, tn):
    t, d = x.shape
    nj = n // tn
    assert nj >= 2
    return pl.pallas_call(
        _fox_proj_kernel,
        out_shape=(jax.ShapeDtypeStruct((t, n), BF16), jax.ShapeDtypeStruct((t, LANES), F32)),
        grid=(t // tm, nj),
        in_specs=[
            _x_ahead_spec(tm, d, t // tm, nj),
            pl.BlockSpec((1, d), lambda i, j: (0, 0)),
            mod.spec(0, tm), mod.spec(1, tm), mod.spec(0, tm, 1), mod.spec(1, tm, 1),
            pl.BlockSpec((None, d, tn), lambda i, j: (layer, 0, j)),
            _const_spec((d, LANES)),
        ],
        out_specs=(pl.BlockSpec((tm, tn), lambda i, j: (i, j)),
                   pl.BlockSpec((tm, LANES), lambda i, j: (i, 0))),
        scratch_shapes=[pltpu.VMEM((2, tm, d), BF16)],
        compiler_params=_cparams("arbitrary", "arbitrary"),
        name="fox_proj",
    )(x, gain, mod.mod, mod.mod, mod.mod, mod.mod, w, wf)


def _swa_proj_kernel(x_ref, g_ref, sh_ref, sc_ref, shn_ref, scn_ref, w_ref, pos_ref, inv_ref,
                     m1_ref, m2_ref, o_ref, h_ref, cos_ref, s1_ref, s2_ref, *, n_rope_tiles, shift):
    i, j = pl.program_id(0), pl.program_id(1)
    cur = i % 2

    def stage(slot, sh, sc):
        h_ref[slot] = _prenorm(x_ref[...], g_ref[...], sh[0], sc[0]).astype(BF16)
        ang = pos_ref[...].astype(F32) * inv_ref[...]
        sn = jnp.sin(ang)
        cos_ref[slot] = jnp.cos(ang)
        s1_ref[slot] = sn * m1_ref[...]
        s2_ref[slot] = sn * m2_ref[...]

    def rope_tile():
        acc = _dot(h_ref[cur], w_ref[...])
        cs, s1, s2 = cos_ref[cur], s1_ref[cur], s2_ref[cur]
        for c in range(acc.shape[1] // LANES):
            a = acc[:, c * LANES:(c + 1) * LANES]
            r = a * cs + pltpu.roll(a, LANES - shift, 1) * s1 + pltpu.roll(a, shift, 1) * s2
            o_ref[:, c * LANES:(c + 1) * LANES] = r.astype(o_ref.dtype)

    @pl.when(jnp.logical_and(i == 0, j == 0))
    def _():
        stage(0, sh_ref, sc_ref)

    @pl.when(j < n_rope_tiles)
    def _():
        rope_tile()

    @pl.when(jnp.logical_and(j >= n_rope_tiles, j < pl.num_programs(1) - 1))
    def _():
        o_ref[...] = _dot(h_ref[cur], w_ref[...]).astype(o_ref.dtype)

    @pl.when(j == pl.num_programs(1) - 1)
    def _():
        stage(1 - cur, shn_ref, scn_ref)
        o_ref[...] = _dot(h_ref[cur], w_ref[...]).astype(o_ref.dtype)


def _swa_proj(x, gain, mod, w, pos, inv_l, m1, m2, tm, tn, n_rope_cols, shift):
    t, d = x.shape
    n = w.shape[1]
    nt, nj = t // tm, n // tn
    assert n_rope_cols // tn < nj
    row = lambda shape: pl.BlockSpec(shape, lambda i, j: (0, 0))
    ahead = lambda i, j: (jnp.where(j == nj - 1, jnp.minimum(i + 1, nt - 1), i), 0)
    return pl.pallas_call(
        functools.partial(_swa_proj_kernel, n_rope_tiles=n_rope_cols // tn, shift=shift),
        out_shape=jax.ShapeDtypeStruct((t, n), BF16),
        grid=(nt, nj),
        in_specs=[
            _x_ahead_spec(tm, d, nt, nj),
            row((1, d)),
            mod.spec(0, tm), mod.spec(1, tm), mod.spec(0, tm, 1), mod.spec(1, tm, 1),
            pl.BlockSpec((d, tn), lambda i, j: (0, j)),
            pl.BlockSpec((tm, 1), ahead),
            row((1, LANES)), row((1, LANES)), row((1, LANES)),
        ],
        out_specs=pl.BlockSpec((tm, tn), lambda i, j: (i, j)),
        scratch_shapes=[pltpu.VMEM((2, tm, d), BF16)] + [pltpu.VMEM((2, tm, LANES), F32)] * 3,
        compiler_params=_cparams("arbitrary", "arbitrary"),
        name="swa_proj",
    )(x, gain, mod.mod, mod.mod, mod.mod, mod.mod, w, pos, inv_l, m1, m2)


def _out_kernel(a_ref, w_ref, x_ref, g_ref, gate_ref, o_ref):
    y = _dot(a_ref[...], w_ref[...])
    o_ref[...] = _postnorm_residual(x_ref[...], y, g_ref[...], gate_ref[0])


def _out_proj(a, w, layer, x, gain, mod, tm):
    t, k = a.shape
    d = w.shape[2]
    return pl.pallas_call(
        _out_kernel,
        out_shape=jax.ShapeDtypeStruct((t, d), F32),
        grid=(t // tm,),
        in_specs=[
            pl.BlockSpec((tm, k), lambda i: (i, 0)),
            pl.BlockSpec((None, k, d), lambda i: (layer, 0, 0), pipeline_mode=pl.Buffered(1)),
            pl.BlockSpec((tm, d), lambda i: (i, 0)),
            pl.BlockSpec((1, d), lambda i: (0, 0)),
            mod.spec(2, tm),
        ],
        out_specs=pl.BlockSpec((tm, d), lambda i: (i, 0)),
        compiler_params=_cparams("parallel"),
        name="out_proj",
    )(a, w, x, gain, mod.mod)


def _ffn_kernel(x_ref, gpre_ref, sh_ref, sc_ref, shn_ref, scn_ref, wg_ref, wu_ref, wd_ref,
                gpost_ref, gate_ref, o_ref, h_ref, xs_ref, acc_ref):
    i, j = pl.program_id(0), pl.program_id(1)
    last = pl.num_programs(1) - 1
    cur = i % 2

    def stage(slot, sh, sc):
        x = x_ref[...]
        xs_ref[slot] = x
        h_ref[slot] = _prenorm(x, gpre_ref[...], sh[0], sc[0]).astype(BF16)

    def chunk():
        h = h_ref[cur]
        g = _dot(h, wg_ref[...])
        u = _dot(h, wu_ref[...])
        a = (g * jax.nn.sigmoid(g) * u).astype(BF16)
        return _dot(a, wd_ref[...])

    @pl.when(jnp.logical_and(i == 0, j == 0))
    def _():
        stage(0, sh_ref, sc_ref)

    @pl.when(j == 0)
    def _():
        acc_ref[...] = chunk()

    @pl.when(jnp.logical_and(j > 0, j < last))
    def _():
        acc_ref[...] += chunk()

    @pl.when(j == last)
    def _():
        stage(1 - cur, shn_ref, scn_ref)
        y = acc_ref[...] + chunk()
        o_ref[...] = _postnorm_residual(xs_ref[cur], y, gpost_ref[...], gate_ref[0])


def _ffn(x, gpre, gpost, mod, w_gu, w_down, layer, tm, tf):
    t, d = x.shape
    dff = w_down.shape[1]
    nf = dff // tf
    assert nf >= 2
    row = lambda: pl.BlockSpec((1, d), lambda i, j: (0, 0))
    return pl.pallas_call(
        _ffn_kernel,
        out_shape=jax.ShapeDtypeStruct((t, d), F32),
        grid=(t // tm, nf),
        in_specs=[
            _x_ahead_spec(tm, d, t // tm, nf),
            row(), mod.spec(3, tm), mod.spec(4, tm), mod.spec(3, tm, 1), mod.spec(4, tm, 1),
            pl.BlockSpec((None, d, tf), lambda i, j: (layer, 0, j)),
            pl.BlockSpec((None, d, tf), lambda i, j: (layer, 0, nf + j)),
            pl.BlockSpec((None, tf, d), lambda i, j: (layer, j, 0)),
            row(), mod.spec(5, tm),
        ],
        out_specs=pl.BlockSpec((tm, d), lambda i, j: (i, 0)),
        scratch_shapes=[pltpu.VMEM((2, tm, d), BF16), pltpu.VMEM((2, tm, d), F32),
                        pltpu.VMEM((tm, d), F32)],
        compiler_params=_cparams("arbitrary", "arbitrary"),
        name="ffn",
    )(x, gpre, mod.mod, mod.mod, mod.mod, mod.mod, w_gu, w_gu, w_down, gpost, mod.mod)


def _gate_kernel(fg_ref, b_ref, cum_ref, carry_ref):
    @pl.when(pl.program_id(1) == 0)
    def _():
        carry_ref[...] = jnp.zeros_like(carry_ref)

    z = fg_ref[...] + b_ref[...]
    lf = jnp.minimum(z, 0.0) - jnp.log1p(jnp.exp(-jnp.abs(z)))
    n = lf.shape[0]
    tril = (lax.broadcasted_iota(jnp.int32, (n, n), 1)
            <= lax.broadcasted_iota(jnp.int32, (n, n), 0)).astype(BF16)
    hi = lf.astype(BF16)
    r1 = lf - hi.astype(F32)
    mid = r1.astype(BF16)
    lo = (r1 - mid.astype(F32)).astype(BF16)
    cum = (_dot(tril, hi) + _dot(tril, mid)) + _dot(tril, lo) + carry_ref[...]
    cum_ref[...] = cum
    carry_ref[...] = cum[n - 1:n, :]


def _gate_cumsum(fg, b_pad, batch, tg):
    t = fg.shape[0]
    s = t // batch
    nt = s // tg
    return pl.pallas_call(
        _gate_kernel,
        out_shape=jax.ShapeDtypeStruct((t, LANES), F32),
        grid=(batch, nt),
        in_specs=[
            pl.BlockSpec((tg, LANES), lambda b, j: (b * nt + j, 0)),
            pl.BlockSpec((1, LANES), lambda b, j: (0, 0)),
        ],
        out_specs=pl.BlockSpec((tg, LANES), lambda b, j: (b * nt + j, 0)),
        scratch_shapes=[pltpu.VMEM((1, LANES), F32)],
        compiler_params=_cparams("parallel", "arbitrary"),
        name="fox_gate",
    )(fg, b_pad)


def _lane_col(x, h):
    lane = lax.broadcasted_iota(jnp.int32, x.shape, 1)
    return jnp.sum(jnp.where(lane == h, x, 0.0), axis=-1, keepdims=True)


def _decay_lanes(f, base):
    hi = f.astype(BF16).astype(F32)
    r = f - hi
    mid = r.astype(BF16).astype(F32)
    lo = (r - mid).astype(BF16).astype(F32)
    lane = lax.broadcasted_iota(jnp.int32, (f.shape[0], LANES), 1)
    ones = jnp.logical_and(lane >= 3 - base, lane < 6 - base)
    out = jnp.where(lane == base, hi, jnp.where(lane == base + 1, mid, jnp.where(
        lane == base + 2, lo, jnp.where(ones, 1.0, 0.0))))
    return out.astype(BF16)


def _fox_attn_kernel(q_ref, k_ref, v_ref, cum_ref, o_ref, ka_ref, vt_ref, qa_ref, kstat_ref,
                     sa_ref, sb_ref, pa_ref, pb_ref, m_ref, ala_ref, alb_ref, acc_ref,
                     *, tk, qscale):
    h = pl.program_id(1)
    i = pl.program_id(2)
    tq, dh = q_ref.shape
    nkv = vt_ref.shape[0]
    lane_row = lax.broadcasted_iota(jnp.int32, (1, LANES), 1)

    @pl.when(i == 0)
    def _():
        kn2 = jnp.zeros((1, 1), F32)
        for c in range(nkv):
            rows = slice(c * tk, (c + 1) * tk)
            kc = k_ref[rows, :]
            kf = kc.astype(F32)
            kn2 = jnp.maximum(kn2, jnp.max(jnp.sum(kf * kf, axis=1, keepdims=True),
                                           axis=0, keepdims=True))
            ka_ref[rows, :dh] = kc
            ka_ref[rows, dh:] = _decay_lanes(_lane_col(cum_ref[rows, :], h) * (-LOG2E), 0)
            vt_ref[c, :dh, :] = v_ref[rows, :].T
            vt_ref[c, dh:, :] = jnp.ones((vt_ref.shape[1] - dh, tk), BF16)
        stat = jnp.where(lane_row == LANES - 1, kn2, 0.0)
        for k in range(nkv // 2):
            last = (k + 1) * tq - 1
            stat = jnp.where(lane_row == k, _lane_col(cum_ref[last:last + 1, :], h) * LOG2E, stat)
        kstat_ref[...] = stat

    qs = (q_ref[...].astype(F32) * qscale).astype(BF16)
    fq = _lane_col(cum_ref[pl.ds(pl.multiple_of(i * tq, tq), tq), :], h) * LOG2E
    qa_ref[:, :dh] = qs
    qa_ref[:, dh:] = _decay_lanes(fq, 3)
    qf = qs.astype(F32)
    qn2 = jnp.max(jnp.sum(qf * qf, axis=1, keepdims=True), axis=0, keepdims=True)
    fq_max = jnp.max(fq, axis=0, keepdims=True)
    m_ref[...] = jnp.full(m_ref.shape, -jnp.inf, F32)
    acc_ref[...] = jnp.zeros(acc_ref.shape, F32)

    def qk(c, s_out, lo=0):
        ka = ka_ref[pl.ds(pl.multiple_of(c * tk, tk), tk), :]
        s_out[:, lo:] = _dot_nt(ka, qa_ref[lo:, :])

    def pv(c, p_in, al_in):
        acc_ref[...] = al_in[...] * acc_ref[...] + _dot(vt_ref[c], p_in[...])

    def softmax(s_in, p_out, al_out, key_offset=None, lo=0):
        rc = SOFTMAX_ROWS
        w = tq - lo
        col = lax.broadcasted_iota(jnp.int32, (rc, w), 1) + lo
        row = lax.broadcasted_iota(jnp.int32, (rc, w), 0)

        def chunk(r):
            blk = s_in[r * rc:(r + 1) * rc, lo:]
            if key_offset is not None:
                blk = jnp.where(row + (r * rc + key_offset) <= col, blk, -jnp.inf)
            return blk

        part = jnp.full((SUBLANES, w), -jnp.inf, F32)
        for r in range(tk // rc):
            part = jnp.maximum(part, jnp.max(chunk(r).reshape(rc // SUBLANES, SUBLANES, w), axis=0))
        m = m_ref[:, lo:]
        m_new = jnp.maximum(m, jnp.max(part, axis=0, keepdims=True))
        for r in range(tk // rc):
            p_out[r * rc:(r + 1) * rc, lo:] = jnp.exp2(chunk(r) - m_new).astype(BF16)
        m_ref[:, lo:] = m_new
        al_out[:, lo:] = jnp.exp2(m - m_new)

    qk(2 * i, sa_ref)
    qk(2 * i + 1, sb_ref, tk)
    softmax(sa_ref, pa_ref, ala_ref, 0)
    pv(2 * i, pa_ref, ala_ref)
    pb_ref[:, :tk] = jnp.zeros((tk, tk), BF16)
    alb_ref[:, :tk] = jnp.ones((1, tk), F32)
    softmax(sb_ref, pb_ref, alb_ref, tk, tk)

    stat = kstat_ref[...]
    kn2 = jnp.sum(jnp.where(lane_row == LANES - 1, stat, 0.0), axis=1, keepdims=True)
    bound = jnp.sqrt(qn2 * kn2) + fq_max - stat
    m_min = jnp.min(m_ref[...], axis=1, keepdims=True)
    live = jnp.logical_and(lane_row < i, bound >= m_min - SKIP_LOG2)
    first = jnp.min(jnp.where(live, lane_row, i).astype(F32), axis=1, keepdims=True)
    n_old = i - first[0, 0].astype(jnp.int32)

    def body(j, carry):
        k = i - 1 - j
        qk(2 * k, sa_ref)
        qk(2 * k + 1, sb_ref)
        pv(2 * k + 3, pb_ref, alb_ref)
        softmax(sa_ref, pa_ref, ala_ref)
        pv(2 * k, pa_ref, ala_ref)
        softmax(sb_ref, pb_ref, alb_ref)
        return carry

    lax.fori_loop(0, n_old, body, 0)
    pv(2 * (i - n_old) + 1, pb_ref, alb_ref)
    o_ref[...] = (acc_ref[:dh, :] / acc_ref[dh:dh + 1, :]).T.astype(o_ref.dtype)


def _fox_attn(qkv, cum, batch, heads, tq):
    t = qkv.shape[0]
    dh = qkv.shape[1] // (3 * heads)
    s = t // batch
    nq = s // tq
    tk = tq // 2
    return pl.pallas_call(
        functools.partial(_fox_attn_kernel, tk=tk, qscale=dh ** -0.5 * LOG2E),
        out_shape=jax.ShapeDtypeStruct((t, heads * dh), BF16),
        grid=(batch, heads, nq),
        in_specs=[
            pl.BlockSpec((tq, dh), lambda b, h, i: (b * nq + i, h)),
            pl.BlockSpec((s, dh), lambda b, h, i: (b, heads + h)),
            pl.BlockSpec((s, dh), lambda b, h, i: (b, 2 * heads + h)),
            pl.BlockSpec((s, LANES), lambda b, h, i: (b, 0)),
        ],
        out_specs=pl.BlockSpec((tq, dh), lambda b, h, i: (b * nq + i, h)),
        scratch_shapes=[
            pltpu.VMEM((s, 2 * dh), BF16),
            pltpu.VMEM((s // tk, dh + ONES_ROWS, tk), BF16),
            pltpu.VMEM((tq, 2 * dh), BF16),
            pltpu.VMEM((1, LANES), F32),
            pltpu.VMEM((tk, tq), F32), pltpu.VMEM((tk, tq), F32),
            pltpu.VMEM((tk, tq), BF16), pltpu.VMEM((tk, tq), BF16),
            pltpu.VMEM((1, tq), F32),
            pltpu.VMEM((1, tq), F32), pltpu.VMEM((1, tq), F32),
            pltpu.VMEM((dh + ONES_ROWS, tq), F32),
        ],
        compiler_params=_cparams("parallel", "parallel", "arbitrary"),
        name="fox_attn",
    )(qkv, qkv, qkv, cum)


def _sgu_kernel(x_ref, gpre_ref, sh_ref, sc_ref, win_ref, lng_ref, lnb_ref, ws_ref, bs_ref,
                wout_ref, gpost_ref, gate_ref, o_ref, u_ref, vn_ref, gated_ref):
    x = x_ref[...]
    tm = x.shape[0]
    width = u_ref.shape[1]
    groups, chunk, _ = ws_ref.shape
    gd = width // groups
    h = _prenorm(x, gpre_ref[...], sh_ref[0], sc_ref[0]).astype(BF16)
    u_ref[...] = jax.nn.gelu(_dot(h, win_ref[:, :width]))
    v = jax.nn.gelu(_dot(h, win_ref[:, width:]))
    mu = jnp.mean(v, axis=-1, keepdims=True)
    vc = v - mu
    var = jnp.mean(vc * vc, axis=-1, keepdims=True)
    vn_ref[...] = (vc * lax.rsqrt(var + EPS) * lng_ref[...] + lnb_ref[...]).astype(BF16)
    causal = (lax.broadcasted_iota(jnp.int32, (chunk, chunk), 1)
              <= lax.broadcasted_iota(jnp.int32, (chunk, chunk), 0))
    for g in range(groups):
        wg = jnp.where(causal, ws_ref[g], jnp.zeros((), BF16))
        bias = bs_ref[:, g:g + 1]
        cols = slice(g * gd, (g + 1) * gd)
        for c in range(tm // chunk):
            rows = slice(c * chunk, (c + 1) * chunk)
            f = _dot(wg, vn_ref[rows, cols]) + bias
            gated_ref[rows, cols] = (u_ref[rows, cols] * f).astype(BF16)
    y = _dot(gated_ref[...], wout_ref[...])
    o_ref[...] = _postnorm_residual(x, y, gpost_ref[...], gate_ref[0])


def _sgu(x, gpre, gpost, mod, w_in, ln_g, ln_b, w_s, b_st, w_out, tm):
    t, d = x.shape
    width = w_out.shape[0]
    row = lambda n: pl.BlockSpec((1, n), lambda i: (0, 0))
    return pl.pallas_call(
        _sgu_kernel,
        out_shape=jax.ShapeDtypeStruct((t, d), F32),
        grid=(t // tm,),
        in_specs=[
            pl.BlockSpec((tm, d), lambda i: (i, 0)),
            row(d), mod.spec(0, tm), mod.spec(1, tm),
            _const_spec(w_in.shape),
            row(width), row(width),
            _const_spec(w_s.shape),
            _const_spec(b_st.shape),
            _const_spec(w_out.shape),
            row(d), mod.spec(2, tm),
        ],
        out_specs=pl.BlockSpec((tm, d), lambda i: (i, 0)),
        scratch_shapes=[pltpu.VMEM((tm, width), F32), pltpu.VMEM((tm, width), BF16),
                        pltpu.VMEM((tm, width), BF16)],
        compiler_params=_cparams("parallel"),
        name="sgu",
    )(x, gpre, mod.mod, mod.mod, w_in, ln_g, ln_b, w_s, b_st, w_out, gpost, mod.mod)


def _swa_attn_kernel(sink_ref, q_ref, kc_ref, kp_ref, vc_ref, vp_ref, o_ref, *, nb, grp, scale):
    n = pl.program_id(0) % nb
    blk = q_ref.shape[0]
    half = LANES // 2
    kv_cols = kc_ref.shape[1] // LANES
    npair = grp // 2
    row = lax.broadcasted_iota(jnp.int32, (blk, blk), 0)
    col = lax.broadcasted_iota(jnp.int32, (blk, blk), 1)
    keep_p = jnp.logical_and(row > col, n > 0)
    keep_c = row <= col
    keep = jnp.concatenate([keep_p, keep_c], axis=0)
    keep = jnp.concatenate([keep] * npair, axis=1)
    lo = lax.broadcasted_iota(jnp.int32, (2 * blk, LANES), 1) < half
    scores, vts = {}, {}
    for j in range(kv_cols):
        csl = slice(j * LANES, (j + 1) * LANES)
        kf = jnp.concatenate([kp_ref[:, csl], kc_ref[:, csl]], axis=0).astype(F32) * scale
        kr = pltpu.roll(kf, half, 1)
        vt = jnp.concatenate([vp_ref[:, csl], vc_ref[:, csl]], axis=0).T
        for e in range(2):
            kh = 2 * j + e
            qcols = [q_ref[:, (kh * npair + a) * LANES:(kh * npair + a + 1) * LANES]
                     for a in range(npair)]
            rhs = jnp.concatenate(qcols, axis=0)
            vts[kh] = vt[e * half:(e + 1) * half, :]
            for p in range(2):
                src = kf if p == e else kr
                kz = jnp.where(lo if p == 0 else jnp.logical_not(lo), src, 0.0).astype(BF16)
                scores[kh, p] = _dot_nt(kz, rhs)
    probs = {}
    for (kh, p), st in scores.items():
        st = jnp.where(keep, st, -jnp.inf)
        sink = jnp.concatenate(
            [jnp.full((1, blk), sink_ref[kh * grp + 2 * a + p], F32) for a in range(npair)], axis=1)
        m = jnp.maximum(jnp.max(st, axis=0, keepdims=True), sink)
        pt = jnp.exp(st - m)
        den = jnp.sum(pt, axis=0, keepdims=True) + jnp.exp(sink - m)
        probs[kh, p] = (pt.astype(BF16), den)
    for kh in range(2 * kv_cols):
        outs = [_dot(vts[kh], probs[kh, p][0]) / probs[kh, p][1] for p in range(2)]
        for a in range(npair):
            ot = jnp.concatenate([o[:, a * blk:(a + 1) * blk] for o in outs], axis=0)
            c0 = (kh * npair + a) * LANES
            o_ref[:, c0:c0 + LANES] = ot.T.astype(o_ref.dtype)


def _swa_attn(qkv, sinks, batch, hq, hk, dh):
    t = qkv.shape[0]
    nb = t // batch // BLOCK
    grp = hq // hk
    assert 2 * dh == LANES and grp % 2 == 0 and hk % 2 == 0
    prev = lambda r: jnp.maximum(r - 1, 0)
    return pl.pallas_call(
        functools.partial(_swa_attn_kernel, nb=nb, grp=grp, scale=dh ** -0.5),
        out_shape=jax.ShapeDtypeStruct((t, hq * dh), BF16),
        grid=(t // BLOCK,),
        in_specs=[
            pl.BlockSpec(memory_space=pltpu.SMEM),
            pl.BlockSpec((BLOCK, hq * dh), lambda r: (r, 0)),
            pl.BlockSpec((BLOCK, hk * dh), lambda r: (r, grp)),
            pl.BlockSpec((BLOCK, hk * dh), lambda r: (prev(r), grp)),
            pl.BlockSpec((BLOCK, hk * dh), lambda r: (r, grp + 1)),
            pl.BlockSpec((BLOCK, hk * dh), lambda r: (prev(r), grp + 1)),
        ],
        out_specs=pl.BlockSpec((BLOCK, hq * dh), lambda r: (r, 0)),
        compiler_params=_cparams("parallel"),
        name="swa_attn",
    )(sinks, qkv, qkv, qkv, qkv, qkv)


def _pad_lanes(a, n=LANES):
    return jnp.pad(a, [(0, 0)] * (a.ndim - 1) + [(0, n - a.shape[-1])])


def kernel(x, c, positions, ada_w, ada_b, mix_pre_g, mix_post_g, ffn_pre_g, ffn_post_g, ffn_w_gu, ffn_w_down, fox_w_in, fox_b_f, fox_w_out, sgu_w_in, sgu_ln_g, sgu_ln_b, sgu_w_s, sgu_b_s, sgu_w_out, swa_w_in, swa_sinks, swa_w_out):
    batch, seq, d = x.shape
    depth = ada_w.shape[0]
    t = batch * seq
    assert batch <= SUBLANES and seq % BLOCK == 0 and d % LANES == 0

    tm = _pick(seq, 512)
    tm_proj = _pick(seq, 1024)
    xf = x.reshape(t, d)

    c_pad = jnp.pad(c, ((0, SUBLANES - batch), (0, 0)))
    mod_all = _adaln(c_pad, ada_w, ada_b).reshape(depth * SUBLANES * N_MOD, 1, d)

    dff = ffn_w_down.shape[1]
    tf = _pick(dff, 512)
    ffn_gu, ffn_down = ffn_w_gu.astype(BF16), ffn_w_down.astype(BF16)
    fox_in, fox_out = fox_w_in.astype(BF16), fox_w_out.astype(BF16)
    swa_out = swa_w_out.astype(BF16)

    for i in range(depth):
        kind, j = i % N_MIXERS, i // N_MIXERS
        mod = _Mod(mod_all, i, batch, seq)
        gpre, gpost = mix_pre_g[i].reshape(1, d), mix_post_g[i].reshape(1, d)
        if kind == 0:
            heads = fox_b_f.shape[1]
            nqkv = fox_w_in.shape[2] - heads
            qkv, fg = _fox_proj(xf, gpre, mod, fox_in, _pad_lanes(fox_in[j, :, nqkv:]), j, nqkv,
                                tm_proj, _pick(nqkv, min(1536, nqkv // 2)))
            cum = _gate_cumsum(fg, _pad_lanes(fox_b_f[j].reshape(1, heads)), batch, _pick(seq, 512))
            a = _fox_attn(qkv, cum, batch, heads, _pick(seq, 1024))
            xf = _out_proj(a, fox_out, j, xf, gpost, mod, tm)
        elif kind == 1:
            groups = sgu_w_s.shape[1]
            xf = _sgu(xf, gpre, gpost, mod, sgu_w_in[j].astype(BF16),
                      sgu_ln_g[j].reshape(1, -1), sgu_ln_b[j].reshape(1, -1),
                      sgu_w_s[j].astype(BF16), _pad_lanes(sgu_b_s[j].T),
                      sgu_w_out[j].astype(BF16), _pick(seq, 256))
        else:
            dh = SWA_HEAD_DIM
            hq = swa_sinks.shape[1]
            hk = (swa_w_in.shape[2] // dh - hq) // 2
            rope = dh // 4
            half = rope // 2
            inv = ROPE_THETA ** (-jnp.arange(0, rope, 2, dtype=F32) / rope)
            lane_d = jnp.arange(LANES) % dh
            inv_l = jnp.where(lane_d < rope, inv[lane_d % half], 0.0).reshape(1, LANES).astype(F32)
            m1 = jnp.where(lane_d < half, -1.0, 0.0).reshape(1, LANES).astype(F32)
            m2 = jnp.where((lane_d >= half) & (lane_d < rope), 1.0, 0.0).reshape(1, LANES).astype(F32)
            qkv = _swa_proj(xf, gpre, mod, swa_w_in[j].astype(BF16), positions.reshape(t, 1),
                            inv_l, m1, m2, tm_proj, _pick(hk * dh, 512), (hq + hk) * dh, half)
            a = _swa_attn(qkv, swa_sinks[j], batch, hq, hk, dh)
            xf = _out_proj(a, swa_out, j, xf, gpost, mod, tm)
        xf = _ffn(xf, ffn_pre_g[i].reshape(1, d), ffn_post_g[i].reshape(1, d), mod,
                  ffn_gu, ffn_down, i, tm, tf)
    return xf.reshape(batch, seq, d)
```

```python
import functools

import jax
import jax.numpy as jnp
from jax import lax
from jax.experimental import pallas as pl
from jax.experimental.pallas import tpu as pltpu

F32 = jnp.float32
BF16 = jnp.bfloat16

EPS = 1e-6
N_MIXERS = 3
BLOCK = 128
SWA_HEAD_DIM = 64
ROPE_THETA = 500000.0
LANES = 128
SUBLANES = 8
N_MOD = 6
LOG2E = 1.4426950408889634
ONES_ROWS = 16
SOFTMAX_ROWS = 32
FROZEN_LOG2 = 100.0
SKIP_LOG2 = 160.0
VMEM_LIMIT = 56 * 1024 * 1024


def _cparams(*sem):
    return pltpu.CompilerParams(dimension_semantics=sem, vmem_limit_bytes=VMEM_LIMIT)


def _const_spec(shape):
    nd = len(shape)
    return pl.BlockSpec(shape, lambda *_: (0,) * nd, pipeline_mode=pl.Buffered(1))


def _pick(n, pref):
    t = min(n, pref)
    while n % t:
        t -= LANES
    return t


def _rms(x):
    return x * lax.rsqrt(jnp.mean(x * x, axis=-1, keepdims=True) + EPS)


def _prenorm(x, g, sh, sc):
    return _rms(x) * (g * (1.0 + sc)) + sh


def _postnorm_residual(x, y, g, gate):
    return x + _rms(y) * (gate * g)


def _dot(a, b):
    return jnp.dot(a, b, preferred_element_type=F32)


def _dot_nt(a, b):
    return lax.dot_general(a, b, (((1,), (1,)), ((), ())), preferred_element_type=F32)


def _adaln_kernel(c_ref, w_ref, b_ref, o_ref):
    c = c_ref[...]
    ca = (c * jax.nn.sigmoid(c)).astype(BF16)
    o_ref[0] = _dot(ca, w_ref[0].astype(BF16)) + b_ref[0]


def _adaln(c_pad, ada_w, ada_b):
    depth, d, n = ada_w.shape
    tn = _pick(n, 1024)
    return pl.pallas_call(
        _adaln_kernel,
        out_shape=jax.ShapeDtypeStruct((depth, SUBLANES, n), F32),
        grid=(depth, n // tn),
        in_specs=[
            pl.BlockSpec((SUBLANES, d), lambda i, j: (0, 0)),
            pl.BlockSpec((1, d, tn), lambda i, j: (i, 0, j)),
            pl.BlockSpec((1, 1, tn), lambda i, j: (i, 0, j)),
        ],
        out_specs=pl.BlockSpec((1, SUBLANES, tn), lambda i, j: (i, 0, j)),
        compiler_params=_cparams("parallel", "parallel"),
        name="adaln",
    )(c_pad, ada_w, ada_b.reshape(depth, 1, n))


class _Mod:
    def __init__(self, mod, layer, batch, seq):
        self.mod = mod
        self.layer = layer
        self.batch = batch
        self.seq = seq
        self.d = mod.shape[-1]

    def spec(self, k, tm, ahead=0):
        layer, tpb = self.layer, self.seq // tm
        last = self.batch * tpb - 1
        return pl.BlockSpec(
            (1, 1, self.d),
            lambda i, *_: ((layer * SUBLANES + jnp.clip(i + ahead, 0, last) // tpb) * N_MOD + k, 0, 0))


def _x_ahead_spec(tm, d, ntiles, nj):
    first_ahead = nj - 2 if nj >= 3 else nj - 1
    return pl.BlockSpec(
        (tm, d), lambda i, j: (jnp.minimum(jnp.where(j >= first_ahead, i + 1, i), ntiles - 1), 0))


def _fox_proj_kernel(x_ref, g_ref, sh_ref, sc_ref, shn_ref, scn_ref, w_ref, wf_ref, o_ref, fg_ref,
                     h_ref):
    i, j = pl.program_id(0), pl.program_id(1)
    cur = i % 2

    @pl.when(jnp.logical_and(i == 0, j == 0))
    def _():
        h_ref[0] = _prenorm(x_ref[...], g_ref[...], sh_ref[0], sc_ref[0]).astype(BF16)

    @pl.when(j == 0)
    def _():
        h = h_ref[cur]
        fg_ref[...] = _dot(h, wf_ref[...])
        o_ref[...] = _dot(h, w_ref[...]).astype(o_ref.dtype)

    @pl.when(jnp.logical_and(j > 0, j < pl.num_programs(1) - 1))
    def _():
        o_ref[...] = _dot(h_ref[cur], w_ref[...]).astype(o_ref.dtype)

    @pl.when(j == pl.num_programs(1) - 1)
    def _():
        h_ref[1 - cur] = _prenorm(x_ref[...], g_ref[...], shn_ref[0], scn_ref[0]).astype(BF16)
        o_ref[...] = _dot(h_ref[cur], w_ref[...]).astype(o_ref.dtype)


def _fox_proj(x, gain, mod, w, wf, layer, n, tm, tn):
    t, d = x.shape
    nj = n // tn
    assert nj >= 2
    return pl.pallas_call(
        _fox_proj_kernel,
        out_shape=(jax.ShapeDtypeStruct((t, n), BF16), jax.ShapeDtypeStruct((t, LANES), F32)),
        grid=(t // tm, nj),
        in_specs=[
            _x_ahead_spec(tm, d, t // tm, nj),
            pl.BlockSpec((1, d), lambda i, j: (0, 0)),
            mod.spec(0, tm), mod.spec(1, tm), mod.spec(0, tm, 1), mod.spec(1, tm, 1),
            pl.BlockSpec((None, d, tn), lambda i, j: (layer, 0, j)),
            _const_spec((d, LANES)),
        ],
        out_specs=(pl.BlockSpec((tm, tn), lambda i, j: (i, j)),
                   pl.BlockSpec((tm, LANES), lambda i, j: (i, 0))),
        scratch_shapes=[pltpu.VMEM((2, tm, d), BF16)],
        compiler_params=_cparams("arbitrary", "arbitrary"),
        name="fox_proj",
    )(x, gain, mod.mod, mod.mod, mod.mod, mod.mod, w, wf)


def _swa_proj_kernel(x_ref, g_ref, sh_ref, sc_ref, shn_ref, scn_ref, w_ref, pos_ref, inv_ref,
                     m1_ref, m2_ref, o_ref, h_ref, cos_ref, s1_ref, s2_ref, *, n_rope_tiles, shift):
    i, j = pl.program_id(0), pl.program_id(1)
    cur = i % 2

    def stage(slot, sh, sc):
        h_ref[slot] = _prenorm(x_ref[...], g_ref[...], sh[0], sc[0]).astype(BF16)
        ang = pos_ref[...].astype(F32) * inv_ref[...]
        sn = jnp.sin(ang)
        cos_ref[slot] = jnp.cos(ang)
        s1_ref[slot] = sn * m1_ref[...]
        s2_ref[slot] = sn * m2_ref[...]

    def rope_tile():
        acc = _dot(h_ref[cur], w_ref[...])
        cs, s1, s2 = cos_ref[cur], s1_ref[cur], s2_ref[cur]
        for c in range(acc.shape[1] // LANES):
            a = acc[:, c * LANES:(c + 1) * LANES]
            r = a * cs + pltpu.roll(a, LANES - shift, 1) * s1 + pltpu.roll(a, shift, 1) * s2
            o_ref[:, c * LANES:(c + 1) * LANES] = r.astype(o_ref.dtype)

    @pl.when(jnp.logical_and(i == 0, j == 0))
    def _():
        stage(0, sh_ref, sc_ref)

    @pl.when(j < n_rope_tiles)
    def _():
        rope_tile()

    @pl.when(jnp.logical_and(j >= n_rope_tiles, j < pl.num_programs(1) - 1))
    def _():
        o_ref[...] = _dot(h_ref[cur], w_ref[...]).astype(o_ref.dtype)

    @pl.when(j == pl.num_programs(1) - 1)
    def _():
        stage(1 - cur, shn_ref, scn_ref)
        o_ref[...] = _dot(h_ref[cur], w_ref[...]).astype(o_ref.dtype)


def _swa_proj(x, gain, mod, w, pos, inv_l, m1, m2, tm, tn, n_rope_cols, shift):
    t, d = x.shape
    n = w.shape[1]
    nt, nj = t // tm, n // tn
    assert n_rope_cols // tn < nj
    row = lambda shape: pl.BlockSpec(shape, lambda i, j: (0, 0))
    ahead = lambda i, j: (jnp.where(j == nj - 1, jnp.minimum(i + 1, nt - 1), i), 0)
    return pl.pallas_call(
        functools.partial(_swa_proj_kernel, n_rope_tiles=n_rope_cols // tn, shift=shift),
        out_shape=jax.ShapeDtypeStruct((t, n), BF16),
        grid=(nt, nj),
        in_specs=[
            _x_ahead_spec(tm, d, nt, nj),
            row((1, d)),
            mod.spec(0, tm), mod.spec(1, tm), mod.spec(0, tm, 1), mod.spec(1, tm, 1),
            pl.BlockSpec((d, tn), lambda i, j: (0, j)),
            pl.BlockSpec((tm, 1), ahead),
            row((1, LANES)), row((1, LANES)), row((1, LANES)),
        ],
        out_specs=pl.BlockSpec((tm, tn), lambda i, j: (i, j)),
        scratch_shapes=[pltpu.VMEM((2, tm, d), BF16)] + [pltpu.VMEM((2, tm, LANES), F32)] * 3,
        compiler_params=_cparams("arbitrary", "arbitrary"),
        name="swa_proj",
    )(x, gain, mod.mod, mod.mod, mod.mod, mod.mod, w, pos, inv_l, m1, m2)


def _out_kernel(a_ref, w_ref, x_ref, g_ref, gate_ref, o_ref):
    y = _dot(a_ref[...], w_ref[...])
    o_ref[...] = _postnorm_residual(x_ref[...], y, g_ref[...], gate_ref[0])


def _out_proj(a, w, layer, x, gain, mod, tm):
    t, k = a.shape
    d = w.shape[2]
    return pl.pallas_call(
        _out_kernel,
        out_shape=jax.ShapeDtypeStruct((t, d), F32),
        grid=(t // tm,),
        in_specs=[
            pl.BlockSpec((tm, k), lambda i: (i, 0)),
            pl.BlockSpec((None, k, d), lambda i: (layer, 0, 0), pipeline_mode=pl.Buffered(1)),
            pl.BlockSpec((tm, d), lambda i: (i, 0)),
            pl.BlockSpec((1, d), lambda i: (0, 0)),
            mod.spec(2, tm),
        ],
        out_specs=pl.BlockSpec((tm, d), lambda i: (i, 0)),
        compiler_params=_cparams("parallel"),
        name="out_proj",
    )(a, w, x, gain, mod.mod)


def _ffn_kernel(x_ref, gpre_ref, sh_ref, sc_ref, shn_ref, scn_ref, wg_ref, wu_ref, wd_ref,
                gpost_ref, gate_ref, o_ref, h_ref, xs_ref, acc_ref):
    i, j = pl.program_id(0), pl.program_id(1)
    last = pl.num_programs(1) - 1
    cur = i % 2

    def stage(slot, sh, sc):
        x = x_ref[...]
        xs_ref[slot] = x
        h_ref[slot] = _prenorm(x, gpre_ref[...], sh[0], sc[0]).astype(BF16)

    def chunk():
        h = h_ref[cur]
        g = _dot(h, wg_ref[...])
        u = _dot(h, wu_ref[...])
        a = (g * jax.nn.sigmoid(g) * u).astype(BF16)
        return _dot(a, wd_ref[...])

    @pl.when(jnp.logical_and(i == 0, j == 0))
    def _():
        stage(0, sh_ref, sc_ref)

    @pl.when(j == 0)
    def _():
        acc_ref[...] = chunk()

    @pl.when(jnp.logical_and(j > 0, j < last))
    def _():
        acc_ref[...] += chunk()

    @pl.when(j == last)
    def _():
        stage(1 - cur, shn_ref, scn_ref)
        y = acc_ref[...] + chunk()
        o_ref[...] = _postnorm_residual(xs_ref[cur], y, gpost_ref[...], gate_ref[0])


def _ffn(x, gpre, gpost, mod, w_gu, w_down, layer, tm, tf):
    t, d = x.shape
    dff = w_down.shape[1]
    nf = dff // tf
    assert nf >= 2
    row = lambda: pl.BlockSpec((1, d), lambda i, j: (0, 0))
    return pl.pallas_call(
        _ffn_kernel,
        out_shape=jax.ShapeDtypeStruct((t, d), F32),
        grid=(t // tm, nf),
        in_specs=[
            _x_ahead_spec(tm, d, t // tm, nf),
            row(), mod.spec(3, tm), mod.spec(4, tm), mod.spec(3, tm, 1), mod.spec(4, tm, 1),
            pl.BlockSpec((None, d, tf), lambda i, j: (layer, 0, j)),
            pl.BlockSpec((None, d, tf), lambda i, j: (layer, 0, nf + j)),
            pl.BlockSpec((None, tf, d), lambda i, j: (layer, j, 0)),
            row(), mod.spec(5, tm),
        ],
        out_specs=pl.BlockSpec((tm, d), lambda i, j: (i, 0)),
        scratch_shapes=[pltpu.VMEM((2, tm, d), BF16), pltpu.VMEM((2, tm, d), F32),
                        pltpu.VMEM((tm, d), F32)],
        compiler_params=_cparams("arbitrary", "arbitrary"),
        name="ffn",
    )(x, gpre, mod.mod, mod.mod, mod.mod, mod.mod, w_gu, w_gu, w_down, gpost, mod.mod)


def _gate_kernel(fg_ref, b_ref, cum_ref, carry_ref):
    @pl.when(pl.program_id(1) == 0)
    def _():
        carry_ref[...] = jnp.zeros_like(carry_ref)

    z = fg_ref[...] + b_ref[...]
    lf = jnp.minimum(z, 0.0) - jnp.log1p(jnp.exp(-jnp.abs(z)))
    n = lf.shape[0]
    tril = (lax.broadcasted_iota(jnp.int32, (n, n), 1)
            <= lax.broadcasted_iota(jnp.int32, (n, n), 0)).astype(BF16)
    hi = lf.astype(BF16)
    r1 = lf - hi.astype(F32)
    mid = r1.astype(BF16)
    lo = (r1 - mid.astype(F32)).astype(BF16)
    cum = (_dot(tril, hi) + _dot(tril, mid)) + _dot(tril, lo) + carry_ref[...]
    cum_ref[...] = cum
    carry_ref[...] = cum[n - 1:n, :]


def _gate_cumsum(fg, b_pad, batch, tg):
    t = fg.shape[0]
    s = t // batch
    nt = s // tg
    return pl.pallas_call(
        _gate_kernel,
        out_shape=jax.ShapeDtypeStruct((t, LANES), F32),
        grid=(batch, nt),
        in_specs=[
            pl.BlockSpec((tg, LANES), lambda b, j: (b * nt + j, 0)),
            pl.BlockSpec((1, LANES), lambda b, j: (0, 0)),
        ],
        out_specs=pl.BlockSpec((tg, LANES), lambda b, j: (b * nt + j, 0)),
        scratch_shapes=[pltpu.VMEM((1, LANES), F32)],
        compiler_params=_cparams("parallel", "arbitrary"),
        name="fox_gate",
    )(fg, b_pad)


def _lane_col(x, h):
    lane = lax.broadcasted_iota(jnp.int32, x.shape, 1)
    return jnp.sum(jnp.where(lane == h, x, 0.0), axis=-1, keepdims=True)


def _decay_lanes(f, base):
    hi = f.astype(BF16).astype(F32)
    r = f - hi
    mid = r.astype(BF16).astype(F32)
    lo = (r - mid).astype(BF16).astype(F32)
    lane = lax.broadcasted_iota(jnp.int32, (f.shape[0], LANES), 1)
    ones = jnp.logical_and(lane >= 3 - base, lane < 6 - base)
    out = jnp.where(lane == base, hi, jnp.where(lane == base + 1, mid, jnp.where(
        lane == base + 2, lo, jnp.where(ones, 1.0, 0.0))))
    return out.astype(BF16)


def _fox_attn_kernel(q_ref, k_ref, v_ref, cum_ref, o_ref, ka_ref, vt_ref, qa_ref, kstat_ref,
                     sa_ref, sb_ref, pa_ref, pb_ref, m_ref, ala_ref, alb_ref, acc_ref,
                     *, tk, qscale):
    h = pl.program_id(1)
    i = pl.program_id(2)
    tq, dh = q_ref.shape
    nkv = vt_ref.shape[0]
    lane_row = lax.broadcasted_iota(jnp.int32, (1, LANES), 1)

    @pl.when(i == 0)
    def _():
        kn2 = jnp.zeros((1, 1), F32)
        for c in range(nkv):
            rows = slice(c * tk, (c + 1) * tk)
            kc = k_ref[rows, :]
            kf = kc.astype(F32)
            kn2 = jnp.maximum(kn2, jnp.max(jnp.sum(kf * kf, axis=1, keepdims=True),
                                           axis=0, keepdims=True))
            ka_ref[rows, :dh] = kc
            ka_ref[rows, dh:] = _decay_lanes(_lane_col(cum_ref[rows, :], h) * (-LOG2E), 0)
            vt_ref[c, :dh, :] = v_ref[rows, :].T
            vt_ref[c, dh:, :] = jnp.ones((vt_ref.shape[1] - dh, tk), BF16)
        stat = jnp.where(lane_row == LANES - 1, kn2, 0.0)
        for k in range(nkv // 2):
            last = (k + 1) * tq - 1
            stat = jnp.where(lane_row == k, _lane_col(cum_ref[last:last + 1, :], h) * LOG2E, stat)
        kstat_ref[...] = stat

    qs = (q_ref[...].astype(F32) * qscale).astype(BF16)
    fq = _lane_col(cum_ref[pl.ds(pl.multiple_of(i * tq, tq), tq), :], h) * LOG2E
    qa_ref[:, :dh] = qs
    qa_ref[:, dh:] = _decay_lanes(fq, 3)
    qf = qs.astype(F32)
    qn2 = jnp.max(jnp.sum(qf * qf, axis=1, keepdims=True), axis=0, keepdims=True)
    fq_max = jnp.max(fq, axis=0, keepdims=True)
    m_ref[...] = jnp.full(m_ref.shape, -jnp.inf, F32)
    acc_ref[...] = jnp.zeros(acc_ref.shape, F32)

    def qk(c, s_out, lo=0):
        ka = ka_ref[pl.ds(pl.multiple_of(c * tk, tk), tk), :]
        s_out[:, lo:] = _dot_nt(ka, qa_ref[lo:, :])

    def pv(c, p_in, al_in=None):
        acc = acc_ref[...] if al_in is None else al_in[...] * acc_ref[...]
        acc_ref[...] = acc + _dot(vt_ref[c], p_in[...])

    def softmax_frozen(s_in, p_out):
        rc = SOFTMAX_ROWS
        m = m_ref[...]
        for r in range(tk // rc):
            p_out[r * rc:(r + 1) * rc, :] = jnp.exp2(s_in[r * rc:(r + 1) * rc, :] - m).astype(BF16)

    def softmax(s_in, p_out, al_out, key_offset=None, lo=0):
        rc = SOFTMAX_ROWS
        w = tq - lo
        col = lax.broadcasted_iota(jnp.int32, (rc, w), 1) + lo
        row = lax.broadcasted_iota(jnp.int32, (rc, w), 0)

        def chunk(r):
            blk = s_in[r * rc:(r + 1) * rc, lo:]
            if key_offset is not None:
                blk = jnp.where(row + (r * rc + key_offset) <= col, blk, -jnp.inf)
            return blk

        part = jnp.full((SUBLANES, w), -jnp.inf, F32)
        for r in range(tk // rc):
            part = jnp.maximum(part, jnp.max(chunk(r).reshape(rc // SUBLANES, SUBLANES, w), axis=0))
        m = m_ref[:, lo:]
        m_new = jnp.maximum(m, jnp.max(part, axis=0, keepdims=True))
        for r in range(tk // rc):
            p_out[r * rc:(r + 1) * rc, lo:] = jnp.exp2(chunk(r) - m_new).astype(BF16)
        m_ref[:, lo:] = m_new
        al_out[:, lo:] = jnp.exp2(m - m_new)

    qk(2 * i, sa_ref)
    qk(2 * i + 1, sb_ref, tk)
    softmax(sa_ref, pa_ref, ala_ref, 0)
    pv(2 * i, pa_ref, ala_ref)
    pb_ref[:, :tk] = jnp.zeros((tk, tk), BF16)
    alb_ref[:, :tk] = jnp.ones((1, tk), F32)
    softmax(sb_ref, pb_ref, alb_ref, tk, tk)

    stat = kstat_ref[...]
    kn2 = jnp.sum(jnp.where(lane_row == LANES - 1, stat, 0.0), axis=1, keepdims=True)
    bound = jnp.sqrt(qn2 * kn2) + fq_max - stat
    m_min = jnp.min(m_ref[...], axis=1, keepdims=True)
    live = jnp.logical_and(lane_row < i, bound >= m_min - SKIP_LOG2)
    first = jnp.min(jnp.where(live, lane_row, i).astype(F32), axis=1, keepdims=True)
    n_old = i - first[0, 0].astype(jnp.int32)
    top = jnp.max(jnp.where(lane_row < i, bound, -jnp.inf), axis=1, keepdims=True)
    frozen = (top - m_min)[0, 0] < FROZEN_LOG2
    n_frozen = jnp.where(frozen, n_old, 0)

    def body_frozen(j, carry):
        k = i - 1 - j
        qk(2 * k, sa_ref)
        qk(2 * k + 1, sb_ref)
        pv(2 * k + 3, pb_ref, alb_ref)
        alb_ref[...] = jnp.ones(alb_ref.shape, F32)
        softmax_frozen(sa_ref, pa_ref)
        pv(2 * k, pa_ref)
        softmax_frozen(sb_ref, pb_ref)
        return carry

    def body(j, carry):
        k = i - 1 - j
        qk(2 * k, sa_ref)
        qk(2 * k + 1, sb_ref)
        pv(2 * k + 3, pb_ref, alb_ref)
        softmax(sa_ref, pa_ref, ala_ref)
        pv(2 * k, pa_ref, ala_ref)
        softmax(sb_ref, pb_ref, alb_ref)
        return carry

    lax.fori_loop(0, n_frozen, body_frozen, 0)
    lax.fori_loop(n_frozen, n_old, body, 0)
    pv(2 * (i - n_old) + 1, pb_ref, alb_ref)
    o_ref[...] = (acc_ref[:dh, :] / acc_ref[dh:dh + 1, :]).T.astype(o_ref.dtype)


def _fox_attn(qkv, cum, batch, heads, tq):
    t = qkv.shape[0]
    dh = qkv.shape[1] // (3 * heads)
    s = t // batch
    nq = s // tq
    tk = tq // 2
    return pl.pallas_call(
        functools.partial(_fox_attn_kernel, tk=tk, qscale=dh ** -0.5 * LOG2E),
        out_shape=jax.ShapeDtypeStruct((t, heads * dh), BF16),
        grid=(batch, heads, nq),
        in_specs=[
            pl.BlockSpec((tq, dh), lambda b, h, i: (b * nq + i, h)),
            pl.BlockSpec((s, dh), lambda b, h, i: (b, heads + h)),
            pl.BlockSpec((s, dh), lambda b, h, i: (b, 2 * heads + h)),
            pl.BlockSpec((s, LANES), lambda b, h, i: (b, 0)),
        ],
        out_specs=pl.BlockSpec((tq, dh), lambda b, h, i: (b * nq + i, h)),
        scratch_shapes=[
            pltpu.VMEM((s, 2 * dh), BF16),
            pltpu.VMEM((s // tk, dh + ONES_ROWS, tk), BF16),
            pltpu.VMEM((tq, 2 * dh), BF16),
            pltpu.VMEM((1, LANES), F32),
            pltpu.VMEM((tk, tq), F32), pltpu.VMEM((tk, tq), F32),
            pltpu.VMEM((tk, tq), BF16), pltpu.VMEM((tk, tq), BF16),
            pltpu.VMEM((1, tq), F32),
            pltpu.VMEM((1, tq), F32), pltpu.VMEM((1, tq), F32),
            pltpu.VMEM((dh + ONES_ROWS, tq), F32),
        ],
        compiler_params=_cparams("parallel", "parallel", "arbitrary"),
        name="fox_attn",
    )(qkv, qkv, qkv, cum)


def _sgu_kernel(x_ref, gpre_ref, sh_ref, sc_ref, win_ref, lng_ref, lnb_ref, ws_ref, bs_ref,
                wout_ref, gpost_ref, gate_ref, o_ref, u_ref, vn_ref, gated_ref):
    x = x_ref[...]
    tm = x.shape[0]
    width = u_ref.shape[1]
    groups, chunk, _ = ws_ref.shape
    gd = width // groups
    h = _prenorm(x, gpre_ref[...], sh_ref[0], sc_ref[0]).astype(BF16)
    u_ref[...] = jax.nn.gelu(_dot(h, win_ref[:, :width]))
    v = jax.nn.gelu(_dot(h, win_ref[:, width:]))
    mu = jnp.mean(v, axis=-1, keepdims=True)
    vc = v - mu
    var = jnp.mean(vc * vc, axis=-1, keepdims=True)
    vn_ref[...] = (vc * lax.rsqrt(var + EPS) * lng_ref[...] + lnb_ref[...]).astype(BF16)
    causal = (lax.broadcasted_iota(jnp.int32, (chunk, chunk), 1)
              <= lax.broadcasted_iota(jnp.int32, (chunk, chunk), 0))
    for g in range(groups):
        wg = jnp.where(causal, ws_ref[g], jnp.zeros((), BF16))
        bias = bs_ref[:, g:g + 1]
        cols = slice(g * gd, (g + 1) * gd)
        for c in range(tm // chunk):
            rows = slice(c * chunk, (c + 1) * chunk)
            f = _dot(wg, vn_ref[rows, cols]) + bias
            gated_ref[rows, cols] = (u_ref[rows, cols] * f).astype(BF16)
    y = _dot(gated_ref[...], wout_ref[...])
    o_ref[...] = _postnorm_residual(x, y, gpost_ref[...], gate_ref[0])


def _sgu(x, gpre, gpost, mod, w_in, ln_g, ln_b, w_s, b_st, w_out, tm):
    t, d = x.shape
    width = w_out.shape[0]
    row = lambda n: pl.BlockSpec((1, n), lambda i: (0, 0))
    return pl.pallas_call(
        _sgu_kernel,
        out_shape=jax.ShapeDtypeStruct((t, d), F32),
        grid=(t // tm,),
        in_specs=[
            pl.BlockSpec((tm, d), lambda i: (i, 0)),
            row(d), mod.spec(0, tm), mod.spec(1, tm),
            _const_spec(w_in.shape),
            row(width), row(width),
            _const_spec(w_s.shape),
            _const_spec(b_st.shape),
            _const_spec(w_out.shape),
            row(d), mod.spec(2, tm),
        ],
        out_specs=pl.BlockSpec((tm, d), lambda i: (i, 0)),
        scratch_shapes=[pltpu.VMEM((tm, width), F32), pltpu.VMEM((tm, width), BF16),
                        pltpu.VMEM((tm, width), BF16)],
        compiler_params=_cparams("parallel"),
        name="sgu",
    )(x, gpre, mod.mod, mod.mod, w_in, ln_g, ln_b, w_s, b_st, w_out, gpost, mod.mod)


def _swa_attn_kernel(sink_ref, q_ref, kc_ref, kp_ref, vc_ref, vp_ref, o_ref, *, nb, grp, scale):
    n = pl.program_id(0) % nb
    blk = q_ref.shape[0]
    half = LANES // 2
    kv_cols = kc_ref.shape[1] // LANES
    npair = grp // 2
    row = lax.broadcasted_iota(jnp.int32, (blk, blk), 0)
    col = lax.broadcasted_iota(jnp.int32, (blk, blk), 1)
    keep_p = jnp.logical_and(row > col, n > 0)
    keep_c = row <= col
    keep = jnp.concatenate([keep_p, keep_c], axis=0)
    keep = jnp.concatenate([keep] * npair, axis=1)
    lo = lax.broadcasted_iota(jnp.int32, (2 * blk, LANES), 1) < half
    scores, vts = {}, {}
    for j in range(kv_cols):
        csl = slice(j * LANES, (j + 1) * LANES)
        kf = jnp.concatenate([kp_ref[:, csl], kc_ref[:, csl]], axis=0).astype(F32) * scale
        kr = pltpu.roll(kf, half, 1)
        vt = jnp.concatenate([vp_ref[:, csl], vc_ref[:, csl]], axis=0).T
        for e in range(2):
            kh = 2 * j + e
            qcols = [q_ref[:, (kh * npair + a) * LANES:(kh * npair + a + 1) * LANES]
                     for a in range(npair)]
            rhs = jnp.concatenate(qcols, axis=0)
            vts[kh] = vt[e * half:(e + 1) * half, :]
            for p in range(2):
                src = kf if p == e else kr
                kz = jnp.where(lo if p == 0 else jnp.logical_not(lo), src, 0.0).astype(BF16)
                scores[kh, p] = _dot_nt(kz, rhs)
    probs = {}
    for (kh, p), st in scores.items():
        st = jnp.where(keep, st, -jnp.inf)
        sink = jnp.concatenate(
            [jnp.full((1, blk), sink_ref[kh * grp + 2 * a + p], F32) for a in range(npair)], axis=1)
        m = jnp.maximum(jnp.max(st, axis=0, keepdims=True), sink)
        pt = jnp.exp(st - m)
        den = jnp.sum(pt, axis=0, keepdims=True) + jnp.exp(sink - m)
        probs[kh, p] = (pt.astype(BF16), den)
    for kh in range(2 * kv_cols):
        outs = [_dot(vts[kh], probs[kh, p][0]) / probs[kh, p][1] for p in range(2)]
        for a in range(npair):
            ot = jnp.concatenate([o[:, a * blk:(a + 1) * blk] for o in outs], axis=0)
            c0 = (kh * npair + a) * LANES
            o_ref[:, c0:c0 + LANES] = ot.T.astype(o_ref.dtype)


def _swa_attn(qkv, sinks, batch, hq, hk, dh):
    t = qkv.shape[0]
    nb = t // batch // BLOCK
    grp = hq // hk
    assert 2 * dh == LANES and grp % 2 == 0 and hk % 2 == 0
    prev = lambda r: jnp.maximum(r - 1, 0)
    return pl.pallas_call(
        functools.partial(_swa_attn_kernel, nb=nb, grp=grp, scale=dh ** -0.5),
        out_shape=jax.ShapeDtypeStruct((t, hq * dh), BF16),
        grid=(t // BLOCK,),
        in_specs=[
            pl.BlockSpec(memory_space=pltpu.SMEM),
            pl.BlockSpec((BLOCK, hq * dh), lambda r: (r, 0)),
            pl.BlockSpec((BLOCK, hk * dh), lambda r: (r, grp)),
            pl.BlockSpec((BLOCK, hk * dh), lambda r: (prev(r), grp)),
            pl.BlockSpec((BLOCK, hk * dh), lambda r: (r, grp + 1)),
            pl.BlockSpec((BLOCK, hk * dh), lambda r: (prev(r), grp + 1)),
        ],
        out_specs=pl.BlockSpec((BLOCK, hq * dh), lambda r: (r, 0)),
        compiler_params=_cparams("parallel"),
        name="swa_attn",
    )(sinks, qkv, qkv, qkv, qkv, qkv)


def _pad_lanes(a, n=LANES):
    return jnp.pad(a, [(0, 0)] * (a.ndim - 1) + [(0, n - a.shape[-1])])


def kernel(x, c, positions, ada_w, ada_b, mix_pre_g, mix_post_g, ffn_pre_g, ffn_post_g, ffn_w_gu, ffn_w_down, fox_w_in, fox_b_f, fox_w_out, sgu_w_in, sgu_ln_g, sgu_ln_b, sgu_w_s, sgu_b_s, sgu_w_out, swa_w_in, swa_sinks, swa_w_out):
    batch, seq, d = x.shape
    depth = ada_w.shape[0]
    t = batch * seq
    assert batch <= SUBLANES and seq % BLOCK == 0 and d % LANES == 0

    tm = _pick(seq, 512)
    tm_proj = _pick(seq, 1024)
    xf = x.reshape(t, d)

    c_pad = jnp.pad(c, ((0, SUBLANES - batch), (0, 0)))
    mod_all = _adaln(c_pad, ada_w, ada_b).reshape(depth * SUBLANES * N_MOD, 1, d)

    dff = ffn_w_down.shape[1]
    tf = _pick(dff, 512)
    ffn_gu, ffn_down = ffn_w_gu.astype(BF16), ffn_w_down.astype(BF16)
    fox_in, fox_out = fox_w_in.astype(BF16), fox_w_out.astype(BF16)
    swa_out = swa_w_out.astype(BF16)

    for i in range(depth):
        kind, j = i % N_MIXERS, i // N_MIXERS
        mod = _Mod(mod_all, i, batch, seq)
        gpre, gpost = mix_pre_g[i].reshape(1, d), mix_post_g[i].reshape(1, d)
        if kind == 0:
            heads = fox_b_f.shape[1]
            nqkv = fox_w_in.shape[2] - heads
            qkv, fg = _fox_proj(xf, gpre, mod, fox_in, _pad_lanes(fox_in[j, :, nqkv:]), j, nqkv,
                                tm_proj, _pick(nqkv, min(1536, nqkv // 2)))
            cum = _gate_cumsum(fg, _pad_lanes(fox_b_f[j].reshape(1, heads)), batch, _pick(seq, 512))
            a = _fox_attn(qkv, cum, batch, heads, _pick(seq, 1024))
            xf = _out_proj(a, fox_out, j, xf, gpost, mod, tm)
        elif kind == 1:
            groups = sgu_w_s.shape[1]
            xf = _sgu(xf, gpre, gpost, mod, sgu_w_in[j].astype(BF16),
                      sgu_ln_g[j].reshape(1, -1), sgu_ln_b[j].reshape(1, -1),
                      sgu_w_s[j].astype(BF16), _pad_lanes(sgu_b_s[j].T),
                      sgu_w_out[j].astype(BF16), _pick(seq, 256))
        else:
            dh = SWA_HEAD_DIM
            hq = swa_sinks.shape[1]
            hk = (swa_w_in.shape[2] // dh - hq) // 2
            rope = dh // 4
            half = rope // 2
            inv = ROPE_THETA ** (-jnp.arange(0, rope, 2, dtype=F32) / rope)
            lane_d = jnp.arange(LANES) % dh
            inv_l = jnp.where(lane_d < rope, inv[lane_d % half], 0.0).reshape(1, LANES).astype(F32)
            m1 = jnp.where(lane_d < half, -1.0, 0.0).reshape(1, LANES).astype(F32)
            m2 = jnp.where((lane_d >= half) & (lane_d < rope), 1.0, 0.0).reshape(1, LANES).astype(F32)
            qkv = _swa_proj(xf, gpre, mod, swa_w_in[j].astype(BF16), positions.reshape(t, 1),
                            inv_l, m1, m2, tm_proj, _pick(hk * dh, 512), (hq + hk) * dh, half)
            a = _swa_attn(qkv, swa_sinks[j], batch, hq, hk, dh)
            xf = _out_proj(a, swa_out, j, xf, gpost, mod, tm)
        xf = _ffn(xf, ffn_pre_g[i].reshape(1, d), ffn_post_g[i].reshape(1, d), mod,
                  ffn_gu, ffn_down, i, tm, tf)
    return xf.reshape(batch, seq, d)
```

```python
import functools

import jax
import jax.numpy as jnp
from jax import lax
from jax.experimental import pallas as pl
from jax.experimental.pallas import tpu as pltpu

F32 = jnp.float32
BF16 = jnp.bfloat16

EPS = 1e-6
N_MIXERS = 3
BLOCK = 128
SWA_HEAD_DIM = 64
ROPE_THETA = 500000.0
LANES = 128
SUBLANES = 8
N_MOD = 6
LOG2E = 1.4426950408889634
ONES_ROWS = 16
SOFTMAX_ROWS = 32
SAFE_LOG2 = 100.0
SKIP_LOG2 = 160.0
VMEM_LIMIT = 56 * 1024 * 1024


def _cparams(*sem):
    return pltpu.CompilerParams(dimension_semantics=sem, vmem_limit_bytes=VMEM_LIMIT)


def _const_spec(shape):
    nd = len(shape)
    return pl.BlockSpec(shape, lambda *_: (0,) * nd, pipeline_mode=pl.Buffered(1))


def _pick(n, pref):
    t = min(n, pref)
    while n % t:
        t -= LANES
    return t


def _rms(x):
    return x * lax.rsqrt(jnp.mean(x * x, axis=-1, keepdims=True) + EPS)


def _prenorm(x, g, sh, sc):
    return _rms(x) * (g * (1.0 + sc)) + sh


def _postnorm_residual(x, y, g, gate):
    return x + _rms(y) * (gate * g)


def _dot(a, b):
    return jnp.dot(a, b, preferred_element_type=F32)


def _dot_nt(a, b):
    return lax.dot_general(a, b, (((1,), (1,)), ((), ())), preferred_element_type=F32)


def _adaln_kernel(c_ref, w_ref, b_ref, o_ref):
    c = c_ref[...]
    ca = (c * jax.nn.sigmoid(c)).astype(BF16)
    o_ref[0] = _dot(ca, w_ref[0].astype(BF16)) + b_ref[0]


def _adaln(c_pad, ada_w, ada_b):
    depth, d, n = ada_w.shape
    tn = _pick(n, 1024)
    return pl.pallas_call(
        _adaln_kernel,
        out_shape=jax.ShapeDtypeStruct((depth, SUBLANES, n), F32),
        grid=(depth, n // tn),
        in_specs=[
            pl.BlockSpec((SUBLANES, d), lambda i, j: (0, 0)),
            pl.BlockSpec((1, d, tn), lambda i, j: (i, 0, j)),
            pl.BlockSpec((1, 1, tn), lambda i, j: (i, 0, j)),
        ],
        out_specs=pl.BlockSpec((1, SUBLANES, tn), lambda i, j: (i, 0, j)),
        compiler_params=_cparams("parallel", "parallel"),
        name="adaln",
    )(c_pad, ada_w, ada_b.reshape(depth, 1, n))


class _Mod:
    def __init__(self, mod, layer, batch, seq):
        self.mod = mod
        self.layer = layer
        self.batch = batch
        self.seq = seq
        self.d = mod.shape[-1]

    def spec(self, k, tm, ahead=0):
        layer, tpb = self.layer, self.seq // tm
        last = self.batch * tpb - 1
        return pl.BlockSpec(
            (1, 1, self.d),
            lambda i, *_: ((layer * SUBLANES + jnp.clip(i + ahead, 0, last) // tpb) * N_MOD + k, 0, 0))


def _x_ahead_spec(tm, d, ntiles, nj):
    first_ahead = nj - 2 if nj >= 3 else nj - 1
    return pl.BlockSpec(
        (tm, d), lambda i, j: (jnp.minimum(jnp.where(j >= first_ahead, i + 1, i), ntiles - 1), 0))


def _fox_proj_kernel(x_ref, g_ref, sh_ref, sc_ref, shn_ref, scn_ref, w_ref, wf_ref, o_ref, fg_ref,
                     h_ref):
    i, j = pl.program_id(0), pl.program_id(1)
    cur = i % 2

    @pl.when(jnp.logical_and(i == 0, j == 0))
    def _():
        h_ref[0] = _prenorm(x_ref[...], g_ref[...], sh_ref[0], sc_ref[0]).astype(BF16)

    @pl.when(j == 0)
    def _():
        h = h_ref[cur]
        fg_ref[...] = _dot(h, wf_ref[...])
        o_ref[...] = _dot(h, w_ref[...]).astype(o_ref.dtype)

    @pl.when(jnp.logical_and(j > 0, j < pl.num_programs(1) - 1))
    def _():
        o_ref[...] = _dot(h_ref[cur], w_ref[...]).astype(o_ref.dtype)

    @pl.when(j == pl.num_programs(1) - 1)
    def _():
        h_ref[1 - cur] = _prenorm(x_ref[...], g_ref[...], shn_ref[0], scn_ref[0]).astype(BF16)
        o_ref[...] = _dot(h_ref[cur], w_ref[...]).astype(o_ref.dtype)


def _fox_proj(x, gain, mod, w, wf, layer, n, tm, tn):
    t, d = x.shape
    nj = n // tn
    assert nj >= 2
    return pl.pallas_call(
        _fox_proj_kernel,
        out_shape=(jax.ShapeDtypeStruct((t, n), BF16), jax.ShapeDtypeStruct((t, LANES), F32)),
        grid=(t // tm, nj),
        in_specs=[
            _x_ahead_spec(tm, d, t // tm, nj),
            pl.BlockSpec((1, d), lambda i, j: (0, 0)),
            mod.spec(0, tm), mod.spec(1, tm), mod.spec(0, tm, 1), mod.spec(1, tm, 1),
            pl.BlockSpec((None, d, tn), lambda i, j: (layer, 0, j)),
            _const_spec((d, LANES)),
        ],
        out_specs=(pl.BlockSpec((tm, tn), lambda i, j: (i, j)),
                   pl.BlockSpec((tm, LANES), lambda i, j: (i, 0))),
        scratch_shapes=[pltpu.VMEM((2, tm, d), BF16)],
        compiler_params=_cparams("arbitrary", "arbitrary"),
        name="fox_proj",
    )(x, gain, mod.mod, mod.mod, mod.mod, mod.mod, w, wf)


def _swa_proj_kernel(x_ref, g_ref, sh_ref, sc_ref, shn_ref, scn_ref, w_ref, pos_ref, inv_ref,
                     m1_ref, m2_ref, o_ref, h_ref, cos_ref, s1_ref, s2_ref, *, n_rope_tiles, shift):
    i, j = pl.program_id(0), pl.program_id(1)
    cur = i % 2

    def stage(slot, sh, sc):
        h_ref[slot] = _prenorm(x_ref[...], g_ref[...], sh[0], sc[0]).astype(BF16)
        ang = pos_ref[...].astype(F32) * inv_ref[...]
        sn = jnp.sin(ang)
        cos_ref[slot] = jnp.cos(ang)
        s1_ref[slot] = sn * m1_ref[...]
        s2_ref[slot] = sn * m2_ref[...]

    def rope_tile():
        acc = _dot(h_ref[cur], w_ref[...])
        cs, s1, s2 = cos_ref[cur], s1_ref[cur], s2_ref[cur]
        for c in range(acc.shape[1] // LANES):
            a = acc[:, c * LANES:(c + 1) * LANES]
            r = a * cs + pltpu.roll(a, LANES - shift, 1) * s1 + pltpu.roll(a, shift, 1) * s2
            o_ref[:, c * LANES:(c + 1) * LANES] = r.astype(o_ref.dtype)

    @pl.when(jnp.logical_and(i == 0, j == 0))
    def _():
        stage(0, sh_ref, sc_ref)

    @pl.when(j < n_rope_tiles)
    def _():
        rope_tile()

    @pl.when(jnp.logical_and(j >= n_rope_tiles, j < pl.num_programs(1) - 1))
    def _():
        o_ref[...] = _dot(h_ref[cur], w_ref[...]).astype(o_ref.dtype)

    @pl.when(j == pl.num_programs(1) - 1)
    def _():
        stage(1 - cur, shn_ref, scn_ref)
        o_ref[...] = _dot(h_ref[cur], w_ref[...]).astype(o_ref.dtype)


def _swa_proj(x, gain, mod, w, pos, inv_l, m1, m2, tm, tn, n_rope_cols, shift):
    t, d = x.shape
    n = w.shape[1]
    nt, nj = t // tm, n // tn
    assert n_rope_cols // tn < nj
    row = lambda shape: pl.BlockSpec(shape, lambda i, j: (0, 0))
    ahead = lambda i, j: (jnp.where(j == nj - 1, jnp.minimum(i + 1, nt - 1), i), 0)
    return pl.pallas_call(
        functools.partial(_swa_proj_kernel, n_rope_tiles=n_rope_cols // tn, shift=shift),
        out_shape=jax.ShapeDtypeStruct((t, n), BF16),
        grid=(nt, nj),
        in_specs=[
            _x_ahead_spec(tm, d, nt, nj),
            row((1, d)),
            mod.spec(0, tm), mod.spec(1, tm), mod.spec(0, tm, 1), mod.spec(1, tm, 1),
            pl.BlockSpec((d, tn), lambda i, j: (0, j)),
            pl.BlockSpec((tm, 1), ahead),
            row((1, LANES)), row((1, LANES)), row((1, LANES)),
        ],
        out_specs=pl.BlockSpec((tm, tn), lambda i, j: (i, j)),
        scratch_shapes=[pltpu.VMEM((2, tm, d), BF16)] + [pltpu.VMEM((2, tm, LANES), F32)] * 3,
        compiler_params=_cparams("arbitrary", "arbitrary"),
        name="swa_proj",
    )(x, gain, mod.mod, mod.mod, mod.mod, mod.mod, w, pos, inv_l, m1, m2)


def _out_kernel(a_ref, w_ref, x_ref, g_ref, gate_ref, o_ref):
    y = _dot(a_ref[...], w_ref[...])
    o_ref[...] = _postnorm_residual(x_ref[...], y, g_ref[...], gate_ref[0])


def _out_proj(a, w, layer, x, gain, mod, tm):
    t, k = a.shape
    d = w.shape[2]
    return pl.pallas_call(
        _out_kernel,
        out_shape=jax.ShapeDtypeStruct((t, d), F32),
        grid=(t // tm,),
        in_specs=[
            pl.BlockSpec((tm, k), lambda i: (i, 0)),
            pl.BlockSpec((None, k, d), lambda i: (layer, 0, 0), pipeline_mode=pl.Buffered(1)),
            pl.BlockSpec((tm, d), lambda i: (i, 0)),
            pl.BlockSpec((1, d), lambda i: (0, 0)),
            mod.spec(2, tm),
        ],
        out_specs=pl.BlockSpec((tm, d), lambda i: (i, 0)),
        compiler_params=_cparams("parallel"),
        name="out_proj",
    )(a, w, x, gain, mod.mod)


def _ffn_kernel(x_ref, gpre_ref, sh_ref, sc_ref, shn_ref, scn_ref, wg_ref, wu_ref, wd_ref,
                gpost_ref, gate_ref, o_ref, h_ref, xs_ref, acc_ref):
    i, j = pl.program_id(0), pl.program_id(1)
    last = pl.num_programs(1) - 1
    cur = i % 2

    def stage(slot, sh, sc):
        x = x_ref[...]
        xs_ref[slot] = x
        h_ref[slot] = _prenorm(x, gpre_ref[...], sh[0], sc[0]).astype(BF16)

    def chunk():
        h = h_ref[cur]
        g = _dot(h, wg_ref[...])
        u = _dot(h, wu_ref[...])
        a = (g * jax.nn.sigmoid(g) * u).astype(BF16)
        return _dot(a, wd_ref[...])

    @pl.when(jnp.logical_and(i == 0, j == 0))
    def _():
        stage(0, sh_ref, sc_ref)

    @pl.when(j == 0)
    def _():
        acc_ref[...] = chunk()

    @pl.when(jnp.logical_and(j > 0, j < last))
    def _():
        acc_ref[...] += chunk()

    @pl.when(j == last)
    def _():
        stage(1 - cur, shn_ref, scn_ref)
        y = acc_ref[...] + chunk()
        o_ref[...] = _postnorm_residual(xs_ref[cur], y, gpost_ref[...], gate_ref[0])


def _ffn(x, gpre, gpost, mod, w_gu, w_down, layer, tm, tf):
    t, d = x.shape
    dff = w_down.shape[1]
    nf = dff // tf
    assert nf >= 2
    row = lambda: pl.BlockSpec((1, d), lambda i, j: (0, 0))
    return pl.pallas_call(
        _ffn_kernel,
        out_shape=jax.ShapeDtypeStruct((t, d), F32),
        grid=(t // tm, nf),
        in_specs=[
            _x_ahead_spec(tm, d, t // tm, nf),
            row(), mod.spec(3, tm), mod.spec(4, tm), mod.spec(3, tm, 1), mod.spec(4, tm, 1),
            pl.BlockSpec((None, d, tf), lambda i, j: (layer, 0, j)),
            pl.BlockSpec((None, d, tf), lambda i, j: (layer, 0, nf + j)),
            pl.BlockSpec((None, tf, d), lambda i, j: (layer, j, 0)),
            row(), mod.spec(5, tm),
        ],
        out_specs=pl.BlockSpec((tm, d), lambda i, j: (i, 0)),
        scratch_shapes=[pltpu.VMEM((2, tm, d), BF16), pltpu.VMEM((2, tm, d), F32),
                        pltpu.VMEM((tm, d), F32)],
        compiler_params=_cparams("arbitrary", "arbitrary"),
        name="ffn",
    )(x, gpre, mod.mod, mod.mod, mod.mod, mod.mod, w_gu, w_gu, w_down, gpost, mod.mod)


def _gate_kernel(fg_ref, b_ref, cum_ref, carry_ref):
    @pl.when(pl.program_id(1) == 0)
    def _():
        carry_ref[...] = jnp.zeros_like(carry_ref)

    z = fg_ref[...] + b_ref[...]
    lf = jnp.minimum(z, 0.0) - jnp.log1p(jnp.exp(-jnp.abs(z)))
    n = lf.shape[0]
    tril = (lax.broadcasted_iota(jnp.int32, (n, n), 1)
            <= lax.broadcasted_iota(jnp.int32, (n, n), 0)).astype(BF16)
    hi = lf.astype(BF16)
    r1 = lf - hi.astype(F32)
    mid = r1.astype(BF16)
    lo = (r1 - mid.astype(F32)).astype(BF16)
    cum = (_dot(tril, hi) + _dot(tril, mid)) + _dot(tril, lo) + carry_ref[...]
    cum_ref[...] = cum
    carry_ref[...] = cum[n - 1:n, :]


def _gate_cumsum(fg, b_pad, batch, tg):
    t = fg.shape[0]
    s = t // batch
    nt = s // tg
    return pl.pallas_call(
        _gate_kernel,
        out_shape=jax.ShapeDtypeStruct((t, LANES), F32),
        grid=(batch, nt),
        in_specs=[
            pl.BlockSpec((tg, LANES), lambda b, j: (b * nt + j, 0)),
            pl.BlockSpec((1, LANES), lambda b, j: (0, 0)),
        ],
        out_specs=pl.BlockSpec((tg, LANES), lambda b, j: (b * nt + j, 0)),
        scratch_shapes=[pltpu.VMEM((1, LANES), F32)],
        compiler_params=_cparams("parallel", "arbitrary"),
        name="fox_gate",
    )(fg, b_pad)


def _lane_col(x, h):
    lane = lax.broadcasted_iota(jnp.int32, x.shape, 1)
    return jnp.sum(jnp.where(lane == h, x, 0.0), axis=-1, keepdims=True)


def _decay_lanes(f, base):
    hi = f.astype(BF16).astype(F32)
    r = f - hi
    mid = r.astype(BF16).astype(F32)
    lo = (r - mid).astype(BF16).astype(F32)
    lane = lax.broadcasted_iota(jnp.int32, (f.shape[0], LANES), 1)
    ones = jnp.logical_and(lane >= 3 - base, lane < 6 - base)
    out = jnp.where(lane == base, hi, jnp.where(lane == base + 1, mid, jnp.where(
        lane == base + 2, lo, jnp.where(ones, 1.0, 0.0))))
    return out.astype(BF16)


def _fox_attn_kernel(q_ref, k_ref, v_ref, cum_ref, o_ref, ka_ref, vt_ref, qa_ref, kstat_ref,
                     sa_ref, sb_ref, pa_ref, pb_ref, m_ref, ala_ref, alb_ref, acc_ref,
                     *, tk, qscale):
    h = pl.program_id(1)
    i = pl.program_id(2)
    tq, dh = q_ref.shape
    nkv = vt_ref.shape[0]
    lane_row = lax.broadcasted_iota(jnp.int32, (1, LANES), 1)

    @pl.when(i == 0)
    def _():
        kn2 = jnp.zeros((1, 1), F32)
        for c in range(nkv):
            rows = slice(c * tk, (c + 1) * tk)
            kc = k_ref[rows, :]
            kf = kc.astype(F32)
            kn2 = jnp.maximum(kn2, jnp.max(jnp.sum(kf * kf, axis=1, keepdims=True),
                                           axis=0, keepdims=True))
            ka_ref[rows, :dh] = kc
            ka_ref[rows, dh:] = _decay_lanes(_lane_col(cum_ref[rows, :], h) * (-LOG2E), 0)
            vt_ref[c, :dh, :] = v_ref[rows, :].T
            vt_ref[c, dh:, :] = jnp.ones((vt_ref.shape[1] - dh, tk), BF16)
        stat = jnp.where(lane_row == LANES - 1, kn2, 0.0)
        for k in range(nkv // 2):
            last = (k + 1) * tq - 1
            stat = jnp.where(lane_row == k, _lane_col(cum_ref[last:last + 1, :], h) * LOG2E, stat)
        kstat_ref[...] = stat

    qs = (q_ref[...].astype(F32) * qscale).astype(BF16)
    fq = _lane_col(cum_ref[pl.ds(pl.multiple_of(i * tq, tq), tq), :], h) * LOG2E
    qa_ref[:, :dh] = qs
    qa_ref[:, dh:] = _decay_lanes(fq, 3)
    qf = qs.astype(F32)
    qn2 = jnp.max(jnp.sum(qf * qf, axis=1, keepdims=True), axis=0, keepdims=True)
    fq_max = jnp.max(fq, axis=0, keepdims=True)
    acc_ref[...] = jnp.zeros(acc_ref.shape, F32)

    def qk(c, s_out, lo=0):
        ka = ka_ref[pl.ds(pl.multiple_of(c * tk, tk), tk), :]
        s_out[:, lo:] = _dot_nt(ka, qa_ref[lo:, :])

    def pv(c, p_in, al_in=None):
        acc = acc_ref[...] if al_in is None else al_in[...] * acc_ref[...]
        acc_ref[...] = acc + _dot(vt_ref[c], p_in[...])

    rc = SOFTMAX_ROWS
    nrc = tk // rc

    def chunk(s_in, r, key_offset, lo):
        blk = s_in[r * rc:(r + 1) * rc, lo:]
        if key_offset is not None:
            col = lax.broadcasted_iota(jnp.int32, blk.shape, 1) + lo
            row = lax.broadcasted_iota(jnp.int32, blk.shape, 0)
            blk = jnp.where(row + (r * rc + key_offset) <= col, blk, -jnp.inf)
        return blk

    def column_max(blocks, width):
        part = jnp.full((SUBLANES, width), -jnp.inf, F32)
        for blk in blocks:
            part = jnp.maximum(part, jnp.max(blk.reshape(rc // SUBLANES, SUBLANES, width), axis=0))
        return jnp.max(part, axis=0, keepdims=True)

    def softmax_fixed(s_in, p_out, ref, key_offset=None, lo=0):
        for r in range(nrc):
            p_out[r * rc:(r + 1) * rc, lo:] = jnp.exp2(chunk(s_in, r, key_offset, lo) - ref).astype(BF16)

    def softmax(s_in, p_out, al_out, key_offset=None, lo=0):
        m = m_ref[:, lo:]
        m_new = jnp.maximum(m, column_max((chunk(s_in, r, key_offset, lo) for r in range(nrc)),
                                          tq - lo))
        for r in range(nrc):
            p_out[r * rc:(r + 1) * rc, lo:] = jnp.exp2(chunk(s_in, r, key_offset, lo) - m_new).astype(BF16)
        m_ref[:, lo:] = m_new
        al_out[:, lo:] = jnp.exp2(m - m_new)

    def older_pairs(m_min):
        live = jnp.logical_and(lane_row < i, qk_bound + fq_max - stat >= m_min - SKIP_LOG2)
        first = jnp.min(jnp.where(live, lane_row, i).astype(F32), axis=1, keepdims=True)
        return i - first[0, 0].astype(jnp.int32)

    def finish():
        o_ref[...] = (acc_ref[:dh, :] / acc_ref[dh:dh + 1, :]).T.astype(o_ref.dtype)

    def diagonal_scores():
        qk(2 * i, sa_ref)
        qk(2 * i + 1, sb_ref, tk)
        pb_ref[:, :tk] = jnp.zeros((tk, tk), BF16)

    stat = kstat_ref[...]
    kn2 = jnp.sum(jnp.where(lane_row == LANES - 1, stat, 0.0), axis=1, keepdims=True)
    qk_bound = jnp.sqrt(qn2 * kn2)
    calm = (2.0 * qk_bound)[0, 0] < SAFE_LOG2

    @pl.when(calm)
    def _():
        diagonal_scores()
        softmax_fixed(sa_ref, pa_ref, qk_bound, 0)
        pv(2 * i, pa_ref)
        softmax_fixed(sb_ref, pb_ref, qk_bound, tk, tk)
        n_old = older_pairs(qk_bound)

        def body(j, carry):
            k = i - 1 - j
            qk(2 * k, sa_ref)
            qk(2 * k + 1, sb_ref)
            pv(2 * k + 3, pb_ref)
            softmax_fixed(sa_ref, pa_ref, qk_bound)
            pv(2 * k, pa_ref)
            softmax_fixed(sb_ref, pb_ref, qk_bound)
            return carry

        lax.fori_loop(0, n_old, body, 0)
        pv(2 * (i - n_old) + 1, pb_ref)
        finish()

    @pl.when(jnp.logical_not(calm))
    def _():
        diagonal_scores()
        m_ref[...] = jnp.full(m_ref.shape, -jnp.inf, F32)
        softmax(sa_ref, pa_ref, ala_ref, 0)
        pv(2 * i, pa_ref, ala_ref)
        alb_ref[:, :tk] = jnp.ones((1, tk), F32)
        softmax(sb_ref, pb_ref, alb_ref, tk, tk)
        n_old = older_pairs(jnp.min(m_ref[...], axis=1, keepdims=True))

        def body(j, carry):
            k = i - 1 - j
            qk(2 * k, sa_ref)
            qk(2 * k + 1, sb_ref)
            pv(2 * k + 3, pb_ref, alb_ref)
            softmax(sa_ref, pa_ref, ala_ref)
            pv(2 * k, pa_ref, ala_ref)
            softmax(sb_ref, pb_ref, alb_ref)
            return carry

        lax.fori_loop(0, n_old, body, 0)
        pv(2 * (i - n_old) + 1, pb_ref, alb_ref)
        finish()


def _fox_attn(qkv, cum, batch, heads, tq):
    t = qkv.shape[0]
    dh = qkv.shape[1] // (3 * heads)
    s = t // batch
    nq = s // tq
    tk = tq // 2
    return pl.pallas_call(
        functools.partial(_fox_attn_kernel, tk=tk, qscale=dh ** -0.5 * LOG2E),
        out_shape=jax.ShapeDtypeStruct((t, heads * dh), BF16),
        grid=(batch, heads, nq),
        in_specs=[
            pl.BlockSpec((tq, dh), lambda b, h, i: (b * nq + i, h)),
            pl.BlockSpec((s, dh), lambda b, h, i: (b, heads + h)),
            pl.BlockSpec((s, dh), lambda b, h, i: (b, 2 * heads + h)),
            pl.BlockSpec((s, LANES), lambda b, h, i: (b, 0)),
        ],
        out_specs=pl.BlockSpec((tq, dh), lambda b, h, i: (b * nq + i, h)),
        scratch_shapes=[
            pltpu.VMEM((s, 2 * dh), BF16),
            pltpu.VMEM((s // tk, dh + ONES_ROWS, tk), BF16),
            pltpu.VMEM((tq, 2 * dh), BF16),
            pltpu.VMEM((1, LANES), F32),
            pltpu.VMEM((tk, tq), F32), pltpu.VMEM((tk, tq), F32),
            pltpu.VMEM((tk, tq), BF16), pltpu.VMEM((tk, tq), BF16),
            pltpu.VMEM((1, tq), F32),
            pltpu.VMEM((1, tq), F32), pltpu.VMEM((1, tq), F32),
            pltpu.VMEM((dh + ONES_ROWS, tq), F32),
        ],
        compiler_params=_cparams("parallel", "parallel", "arbitrary"),
        name="fox_attn",
    )(qkv, qkv, qkv, cum)


def _sgu_kernel(x_ref, gpre_ref, sh_ref, sc_ref, win_ref, lng_ref, lnb_ref, ws_ref, bs_ref,
                wout_ref, gpost_ref, gate_ref, o_ref, u_ref, vn_ref, gated_ref):
    x = x_ref[...]
    tm = x.shape[0]
    width = u_ref.shape[1]
    groups, chunk, _ = ws_ref.shape
    gd = width // groups
    h = _prenorm(x, gpre_ref[...], sh_ref[0], sc_ref[0]).astype(BF16)
    u_ref[...] = jax.nn.gelu(_dot(h, win_ref[:, :width]))
    v = jax.nn.gelu(_dot(h, win_ref[:, width:]))
    mu = jnp.mean(v, axis=-1, keepdims=True)
    vc = v - mu
    var = jnp.mean(vc * vc, axis=-1, keepdims=True)
    vn_ref[...] = (vc * lax.rsqrt(var + EPS) * lng_ref[...] + lnb_ref[...]).astype(BF16)
    causal = (lax.broadcasted_iota(jnp.int32, (chunk, chunk), 1)
              <= lax.broadcasted_iota(jnp.int32, (chunk, chunk), 0))
    for g in range(groups):
        wg = jnp.where(causal, ws_ref[g], jnp.zeros((), BF16))
        bias = bs_ref[:, g:g + 1]
        cols = slice(g * gd, (g + 1) * gd)
        for c in range(tm // chunk):
            rows = slice(c * chunk, (c + 1) * chunk)
            f = _dot(wg, vn_ref[rows, cols]) + bias
            gated_ref[rows, cols] = (u_ref[rows, cols] * f).astype(BF16)
    y = _dot(gated_ref[...], wout_ref[...])
    o_ref[...] = _postnorm_residual(x, y, gpost_ref[...], gate_ref[0])


def _sgu(x, gpre, gpost, mod, w_in, ln_g, ln_b, w_s, b_st, w_out, tm):
    t, d = x.shape
    width = w_out.shape[0]
    row = lambda n: pl.BlockSpec((1, n), lambda i: (0, 0))
    return pl.pallas_call(
        _sgu_kernel,
        out_shape=jax.ShapeDtypeStruct((t, d), F32),
        grid=(t // tm,),
        in_specs=[
            pl.BlockSpec((tm, d), lambda i: (i, 0)),
            row(d), mod.spec(0, tm), mod.spec(1, tm),
            _const_spec(w_in.shape),
            row(width), row(width),
            _const_spec(w_s.shape),
            _const_spec(b_st.shape),
            _const_spec(w_out.shape),
            row(d), mod.spec(2, tm),
        ],
        out_specs=pl.BlockSpec((tm, d), lambda i: (i, 0)),
        scratch_shapes=[pltpu.VMEM((tm, width), F32), pltpu.VMEM((tm, width), BF16),
                        pltpu.VMEM((tm, width), BF16)],
        compiler_params=_cparams("parallel"),
        name="sgu",
    )(x, gpre, mod.mod, mod.mod, w_in, ln_g, ln_b, w_s, b_st, w_out, gpost, mod.mod)


def _swa_attn_kernel(sink_ref, q_ref, kc_ref, kp_ref, vc_ref, vp_ref, o_ref, *, nb, grp, scale):
    n = pl.program_id(0) % nb
    blk = q_ref.shape[0]
    half = LANES // 2
    kv_cols = kc_ref.shape[1] // LANES
    npair = grp // 2
    row = lax.broadcasted_iota(jnp.int32, (blk, blk), 0)
    col = lax.broadcasted_iota(jnp.int32, (blk, blk), 1)
    keep_p = jnp.logical_and(row > col, n > 0)
    keep_c = row <= col
    keep = jnp.concatenate([keep_p, keep_c], axis=0)
    keep = jnp.concatenate([keep] * npair, axis=1)
    lo = lax.broadcasted_iota(jnp.int32, (2 * blk, LANES), 1) < half
    scores, vts = {}, {}
    for j in range(kv_cols):
        csl = slice(j * LANES, (j + 1) * LANES)
        kf = jnp.concatenate([kp_ref[:, csl], kc_ref[:, csl]], axis=0).astype(F32) * scale
        kr = pltpu.roll(kf, half, 1)
        vt = jnp.concatenate([vp_ref[:, csl], vc_ref[:, csl]], axis=0).T
        for e in range(2):
            kh = 2 * j + e
            qcols = [q_ref[:, (kh * npair + a) * LANES:(kh * npair + a + 1) * LANES]
                     for a in range(npair)]
            rhs = jnp.concatenate(qcols, axis=0)
            vts[kh] = vt[e * half:(e + 1) * half, :]
            for p in range(2):
                src = kf if p == e else kr
                kz = jnp.where(lo if p == 0 else jnp.logical_not(lo), src, 0.0).astype(BF16)
                scores[kh, p] = _dot_nt(kz, rhs)
    probs = {}
    for (kh, p), st in scores.items():
        st = jnp.where(keep, st, -jnp.inf)
        sink = jnp.concatenate(
            [jnp.full((1, blk), sink_ref[kh * grp + 2 * a + p], F32) for a in range(npair)], axis=1)
        m = jnp.maximum(jnp.max(st, axis=0, keepdims=True), sink)
        pt = jnp.exp(st - m)
        den = jnp.sum(pt, axis=0, keepdims=True) + jnp.exp(sink - m)
        probs[kh, p] = (pt.astype(BF16), den)
    for kh in range(2 * kv_cols):
        outs = [_dot(vts[kh], probs[kh, p][0]) / probs[kh, p][1] for p in range(2)]
        for a in range(npair):
            ot = jnp.concatenate([o[:, a * blk:(a + 1) * blk] for o in outs], axis=0)
            c0 = (kh * npair + a) * LANES
            o_ref[:, c0:c0 + LANES] = ot.T.astype(o_ref.dtype)


def _swa_attn(qkv, sinks, batch, hq, hk, dh):
    t = qkv.shape[0]
    nb = t // batch // BLOCK
    grp = hq // hk
    assert 2 * dh == LANES and grp % 2 == 0 and hk % 2 == 0
    prev = lambda r: jnp.maximum(r - 1, 0)
    return pl.pallas_call(
        functools.partial(_swa_attn_kernel, nb=nb, grp=grp, scale=dh ** -0.5),
        out_shape=jax.ShapeDtypeStruct((t, hq * dh), BF16),
        grid=(t // BLOCK,),
        in_specs=[
            pl.BlockSpec(memory_space=pltpu.SMEM),
            pl.BlockSpec((BLOCK, hq * dh), lambda r: (r, 0)),
            pl.BlockSpec((BLOCK, hk * dh), lambda r: (r, grp)),
            pl.BlockSpec((BLOCK, hk * dh), lambda r: (prev(r), grp)),
            pl.BlockSpec((BLOCK, hk * dh), lambda r: (r, grp + 1)),
            pl.BlockSpec((BLOCK, hk * dh), lambda r: (prev(r), grp + 1)),
        ],
        out_specs=pl.BlockSpec((BLOCK, hq * dh), lambda r: (r, 0)),
        compiler_params=_cparams("parallel"),
        name="swa_attn",
    )(sinks, qkv, qkv, qkv, qkv, qkv)


def _pad_lanes(a, n=LANES):
    return jnp.pad(a, [(0, 0)] * (a.ndim - 1) + [(0, n - a.shape[-1])])


def kernel(x, c, positions, ada_w, ada_b, mix_pre_g, mix_post_g, ffn_pre_g, ffn_post_g, ffn_w_gu, ffn_w_down, fox_w_in, fox_b_f, fox_w_out, sgu_w_in, sgu_ln_g, sgu_ln_b, sgu_w_s, sgu_b_s, sgu_w_out, swa_w_in, swa_sinks, swa_w_out):
    batch, seq, d = x.shape
    depth = ada_w.shape[0]
    t = batch * seq
    assert batch <= SUBLANES and seq % BLOCK == 0 and d % LANES == 0

    tm = _pick(seq, 512)
    tm_proj = _pick(seq, 1024)
    xf = x.reshape(t, d)

    c_pad = jnp.pad(c, ((0, SUBLANES - batch), (0, 0)))
    mod_all = _adaln(c_pad, ada_w, ada_b).reshape(depth * SUBLANES * N_MOD, 1, d)

    dff = ffn_w_down.shape[1]
    tf = _pick(dff, 512)
    ffn_gu, ffn_down = ffn_w_gu.astype(BF16), ffn_w_down.astype(BF16)
    fox_in, fox_out = fox_w_in.astype(BF16), fox_w_out.astype(BF16)
    swa_out = swa_w_out.astype(BF16)

    for i in range(depth):
        kind, j = i % N_MIXERS, i // N_MIXERS
        mod = _Mod(mod_all, i, batch, seq)
        gpre, gpost = mix_pre_g[i].reshape(1, d), mix_post_g[i].reshape(1, d)
        if kind == 0:
            heads = fox_b_f.shape[1]
            nqkv = fox_w_in.shape[2] - heads
            qkv, fg = _fox_proj(xf, gpre, mod, fox_in, _pad_lanes(fox_in[j, :, nqkv:]), j, nqkv,
                                tm_proj, _pick(nqkv, min(1536, nqkv // 2)))
            cum = _gate_cumsum(fg, _pad_lanes(fox_b_f[j].reshape(1, heads)), batch, _pick(seq, 512))
            a = _fox_attn(qkv, cum, batch, heads, _pick(seq, 1024))
            xf = _out_proj(a, fox_out, j, xf, gpost, mod, tm)
        elif kind == 1:
            groups = sgu_w_s.shape[1]
            xf = _sgu(xf, gpre, gpost, mod, sgu_w_in[j].astype(BF16),
                      sgu_ln_g[j].reshape(1, -1), sgu_ln_b[j].reshape(1, -1),
                      sgu_w_s[j].astype(BF16), _pad_lanes(sgu_b_s[j].T),
                      sgu_w_out[j].astype(BF16), _pick(seq, 256))
        else:
            dh = SWA_HEAD_DIM
            hq = swa_sinks.shape[1]
            hk = (swa_w_in.shape[2] // dh - hq) // 2
            rope = dh // 4
            half = rope // 2
            inv = ROPE_THETA ** (-jnp.arange(0, rope, 2, dtype=F32) / rope)
            lane_d = jnp.arange(LANES) % dh
            inv_l = jnp.where(lane_d < rope, inv[lane_d % half], 0.0).reshape(1, LANES).astype(F32)
            m1 = jnp.where(lane_d < half, -1.0, 0.0).reshape(1, LANES).astype(F32)
            m2 = jnp.where((lane_d >= half) & (lane_d < rope), 1.0, 0.0).reshape(1, LANES).astype(F32)
            qkv = _swa_proj(xf, gpre, mod, swa_w_in[j].astype(BF16), positions.reshape(t, 1),
                            inv_l, m1, m2, tm_proj, _pick(hk * dh, 512), (hq + hk) * dh, half)
            a = _swa_attn(qkv, swa_sinks[j], batch, hq, hk, dh)
            xf = _out_proj(a, swa_out, j, xf, gpost, mod, tm)
        xf = _ffn(xf, ffn_pre_g[i].reshape(1, d), ffn_post_g[i].reshape(1, d), mod,
                  ffn_gu, ffn_down, i, tm, tf)
    return xf.reshape(batch, seq, d)
```

```python
import functools

import jax
import jax.numpy as jnp
from jax import lax
from jax.experimental import pallas as pl
from jax.experimental.pallas import tpu as pltpu

F32 = jnp.float32
BF16 = jnp.bfloat16

EPS = 1e-6
N_MIXERS = 3
BLOCK = 128
SWA_HEAD_DIM = 64
ROPE_THETA = 500000.0
LANES = 128
SUBLANES = 8
N_MOD = 6
LOG2E = 1.4426950408889634
ONES_ROWS = 16
SOFTMAX_ROWS = 32
SAFE_LOG2 = 100.0
SKIP_LOG2 = 160.0
VMEM_LIMIT = 56 * 1024 * 1024


def _cparams(*sem):
    return pltpu.CompilerParams(dimension_semantics=sem, vmem_limit_bytes=VMEM_LIMIT)


def _const_spec(shape):
    nd = len(shape)
    return pl.BlockSpec(shape, lambda *_: (0,) * nd, pipeline_mode=pl.Buffered(1))


def _pick(n, pref):
    t = min(n, pref)
    while n % t:
        t -= LANES
    return t


def _rms(x):
    return x * lax.rsqrt(jnp.mean(x * x, axis=-1, keepdims=True) + EPS)


def _prenorm(x, g, sh, sc):
    return _rms(x) * (g * (1.0 + sc)) + sh


def _postnorm_residual(x, y, g, gate):
    return x + _rms(y) * (gate * g)


def _dot(a, b):
    return jnp.dot(a, b, preferred_element_type=F32)


def _dot_nt(a, b):
    return lax.dot_general(a, b, (((1,), (1,)), ((), ())), preferred_element_type=F32)


def _adaln_kernel(c_ref, w_ref, b_ref, o_ref):
    c = c_ref[...]
    ca = (c * jax.nn.sigmoid(c)).astype(BF16)
    o_ref[0] = _dot(ca, w_ref[0].astype(BF16)) + b_ref[0]


def _adaln(c_pad, ada_w, ada_b):
    depth, d, n = ada_w.shape
    tn = _pick(n, 1024)
    return pl.pallas_call(
        _adaln_kernel,
        out_shape=jax.ShapeDtypeStruct((depth, SUBLANES, n), F32),
        grid=(depth, n // tn),
        in_specs=[
            pl.BlockSpec((SUBLANES, d), lambda i, j: (0, 0)),
            pl.BlockSpec((1, d, tn), lambda i, j: (i, 0, j)),
            pl.BlockSpec((1, 1, tn), lambda i, j: (i, 0, j)),
        ],
        out_specs=pl.BlockSpec((1, SUBLANES, tn), lambda i, j: (i, 0, j)),
        compiler_params=_cparams("parallel", "parallel"),
        name="adaln",
    )(c_pad, ada_w, ada_b.reshape(depth, 1, n))


class _Mod:
    def __init__(self, mod, layer, batch, seq):
        self.mod = mod
        self.layer = layer
        self.batch = batch
        self.seq = seq
        self.d = mod.shape[-1]

    def spec(self, k, tm, ahead=0):
        layer, tpb = self.layer, self.seq // tm
        last = self.batch * tpb - 1
        return pl.BlockSpec(
            (1, 1, self.d),
            lambda i, *_: ((layer * SUBLANES + jnp.clip(i + ahead, 0, last) // tpb) * N_MOD + k, 0, 0))


def _x_ahead_spec(tm, d, ntiles, nj):
    first_ahead = nj - 2 if nj >= 3 else nj - 1
    return pl.BlockSpec(
        (tm, d), lambda i, j: (jnp.minimum(jnp.where(j >= first_ahead, i + 1, i), ntiles - 1), 0))


def _fox_proj_kernel(x_ref, g_ref, sh_ref, sc_ref, shn_ref, scn_ref, w_ref, wf_ref, o_ref, fg_ref,
                     h_ref):
    i, j = pl.program_id(0), pl.program_id(1)
    cur = i % 2

    @pl.when(jnp.logical_and(i == 0, j == 0))
    def _():
        h_ref[0] = _prenorm(x_ref[...], g_ref[...], sh_ref[0], sc_ref[0]).astype(BF16)

    @pl.when(j == 0)
    def _():
        h = h_ref[cur]
        fg_ref[...] = _dot(h, wf_ref[...])
        o_ref[...] = _dot(h, w_ref[...]).astype(o_ref.dtype)

    @pl.when(jnp.logical_and(j > 0, j < pl.num_programs(1) - 1))
    def _():
        o_ref[...] = _dot(h_ref[cur], w_ref[...]).astype(o_ref.dtype)

    @pl.when(j == pl.num_programs(1) - 1)
    def _():
        h_ref[1 - cur] = _prenorm(x_ref[...], g_ref[...], shn_ref[0], scn_ref[0]).astype(BF16)
        o_ref[...] = _dot(h_ref[cur], w_ref[...]).astype(o_ref.dtype)


def _fox_proj(x, gain, mod, w, wf, layer, n, tm, tn):
    t, d = x.shape
    nj = n // tn
    assert nj >= 2
    return pl.pallas_call(
        _fox_proj_kernel,
        out_shape=(jax.ShapeDtypeStruct((t, n), BF16), jax.ShapeDtypeStruct((t, LANES), F32)),
        grid=(t // tm, nj),
        in_specs=[
            _x_ahead_spec(tm, d, t // tm, nj),
            pl.BlockSpec((1, d), lambda i, j: (0, 0)),
            mod.spec(0, tm), mod.spec(1, tm), mod.spec(0, tm, 1), mod.spec(1, tm, 1),
            pl.BlockSpec((None, d, tn), lambda i, j: (layer, 0, j)),
            _const_spec((d, LANES)),
        ],
        out_specs=(pl.BlockSpec((tm, tn), lambda i, j: (i, j)),
                   pl.BlockSpec((tm, LANES), lambda i, j: (i, 0))),
        scratch_shapes=[pltpu.VMEM((2, tm, d), BF16)],
        compiler_params=_cparams("arbitrary", "arbitrary"),
        name="fox_proj",
    )(x, gain, mod.mod, mod.mod, mod.mod, mod.mod, w, wf)


def _swa_proj_kernel(x_ref, g_ref, sh_ref, sc_ref, shn_ref, scn_ref, w_ref, pos_ref, inv_ref,
                     m1_ref, m2_ref, o_ref, h_ref, cos_ref, s1_ref, s2_ref, *, n_rope_tiles, shift):
    i, j = pl.program_id(0), pl.program_id(1)
    cur = i % 2

    def stage(slot, sh, sc):
        h_ref[slot] = _prenorm(x_ref[...], g_ref[...], sh[0], sc[0]).astype(BF16)
        ang = pos_ref[...].astype(F32) * inv_ref[...]
        sn = jnp.sin(ang)
        cos_ref[slot] = jnp.cos(ang)
        s1_ref[slot] = sn * m1_ref[...]
        s2_ref[slot] = sn * m2_ref[...]

    def rope_tile():
        acc = _dot(h_ref[cur], w_ref[...])
        cs, s1, s2 = cos_ref[cur], s1_ref[cur], s2_ref[cur]
        for c in range(acc.shape[1] // LANES):
            a = acc[:, c * LANES:(c + 1) * LANES]
            r = a * cs + pltpu.roll(a, LANES - shift, 1) * s1 + pltpu.roll(a, shift, 1) * s2
            o_ref[:, c * LANES:(c + 1) * LANES] = r.astype(o_ref.dtype)

    @pl.when(jnp.logical_and(i == 0, j == 0))
    def _():
        stage(0, sh_ref, sc_ref)

    @pl.when(j < n_rope_tiles)
    def _():
        rope_tile()

    @pl.when(jnp.logical_and(j >= n_rope_tiles, j < pl.num_programs(1) - 1))
    def _():
        o_ref[...] = _dot(h_ref[cur], w_ref[...]).astype(o_ref.dtype)

    @pl.when(j == pl.num_programs(1) - 1)
    def _():
        stage(1 - cur, shn_ref, scn_ref)
        o_ref[...] = _dot(h_ref[cur], w_ref[...]).astype(o_ref.dtype)


def _swa_proj(x, gain, mod, w, pos, inv_l, m1, m2, tm, tn, n_rope_cols, shift):
    t, d = x.shape
    n = w.shape[1]
    nt, nj = t // tm, n // tn
    assert n_rope_cols // tn < nj
    row = lambda shape: pl.BlockSpec(shape, lambda i, j: (0, 0))
    ahead = lambda i, j: (jnp.where(j == nj - 1, jnp.minimum(i + 1, nt - 1), i), 0)
    return pl.pallas_call(
        functools.partial(_swa_proj_kernel, n_rope_tiles=n_rope_cols // tn, shift=shift),
        out_shape=jax.ShapeDtypeStruct((t, n), BF16),
        grid=(nt, nj),
        in_specs=[
            _x_ahead_spec(tm, d, nt, nj),
            row((1, d)),
            mod.spec(0, tm), mod.spec(1, tm), mod.spec(0, tm, 1), mod.spec(1, tm, 1),
            pl.BlockSpec((d, tn), lambda i, j: (0, j)),
            pl.BlockSpec((tm, 1), ahead),
            row((1, LANES)), row((1, LANES)), row((1, LANES)),
        ],
        out_specs=pl.BlockSpec((tm, tn), lambda i, j: (i, j)),
        scratch_shapes=[pltpu.VMEM((2, tm, d), BF16)] + [pltpu.VMEM((2, tm, LANES), F32)] * 3,
        compiler_params=_cparams("arbitrary", "arbitrary"),
        name="swa_proj",
    )(x, gain, mod.mod, mod.mod, mod.mod, mod.mod, w, pos, inv_l, m1, m2)


def _out_kernel(a_ref, w_ref, x_ref, g_ref, gate_ref, o_ref):
    y = _dot(a_ref[...], w_ref[...])
    o_ref[...] = _postnorm_residual(x_ref[...], y, g_ref[...], gate_ref[0])


def _out_proj(a, w, layer, x, gain, mod, tm):
    t, k = a.shape
    d = w.shape[2]
    return pl.pallas_call(
        _out_kernel,
        out_shape=jax.ShapeDtypeStruct((t, d), F32),
        grid=(t // tm,),
        in_specs=[
            pl.BlockSpec((tm, k), lambda i: (i, 0)),
            pl.BlockSpec((None, k, d), lambda i: (layer, 0, 0), pipeline_mode=pl.Buffered(1)),
            pl.BlockSpec((tm, d), lambda i: (i, 0)),
            pl.BlockSpec((1, d), lambda i: (0, 0)),
            mod.spec(2, tm),
        ],
        out_specs=pl.BlockSpec((tm, d), lambda i: (i, 0)),
        compiler_params=_cparams("parallel"),
        name="out_proj",
    )(a, w, x, gain, mod.mod)


def _ffn_kernel(x_ref, gpre_ref, sh_ref, sc_ref, shn_ref, scn_ref, wg_ref, wu_ref, wd_ref,
                gpost_ref, gate_ref, o_ref, h_ref, xs_ref, acc_ref):
    i, j = pl.program_id(0), pl.program_id(1)
    last = pl.num_programs(1) - 1
    cur = i % 2

    def stage(slot, sh, sc):
        x = x_ref[...]
        xs_ref[slot] = x
        h_ref[slot] = _prenorm(x, gpre_ref[...], sh[0], sc[0]).astype(BF16)

    def chunk():
        h = h_ref[cur]
        g = _dot(h, wg_ref[...])
        u = _dot(h, wu_ref[...])
        a = (g * jax.nn.sigmoid(g) * u).astype(BF16)
        return _dot(a, wd_ref[...])

    @pl.when(jnp.logical_and(i == 0, j == 0))
    def _():
        stage(0, sh_ref, sc_ref)

    @pl.when(j == 0)
    def _():
        acc_ref[...] = chunk()

    @pl.when(jnp.logical_and(j > 0, j < last))
    def _():
        acc_ref[...] += chunk()

    @pl.when(j == last)
    def _():
        stage(1 - cur, shn_ref, scn_ref)
        y = acc_ref[...] + chunk()
        o_ref[...] = _postnorm_residual(xs_ref[cur], y, gpost_ref[...], gate_ref[0])


def _ffn(x, gpre, gpost, mod, w_gu, w_down, layer, tm, tf):
    t, d = x.shape
    dff = w_down.shape[1]
    nf = dff // tf
    assert nf >= 2
    row = lambda: pl.BlockSpec((1, d), lambda i, j: (0, 0))
    return pl.pallas_call(
        _ffn_kernel,
        out_shape=jax.ShapeDtypeStruct((t, d), F32),
        grid=(t // tm, nf),
        in_specs=[
            _x_ahead_spec(tm, d, t // tm, nf),
            row(), mod.spec(3, tm), mod.spec(4, tm), mod.spec(3, tm, 1), mod.spec(4, tm, 1),
            pl.BlockSpec((None, d, tf), lambda i, j: (layer, 0, j)),
            pl.BlockSpec((None, d, tf), lambda i, j: (layer, 0, nf + j)),
            pl.BlockSpec((None, tf, d), lambda i, j: (layer, j, 0)),
            row(), mod.spec(5, tm),
        ],
        out_specs=pl.BlockSpec((tm, d), lambda i, j: (i, 0)),
        scratch_shapes=[pltpu.VMEM((2, tm, d), BF16), pltpu.VMEM((2, tm, d), F32),
                        pltpu.VMEM((tm, d), F32)],
        compiler_params=_cparams("arbitrary", "arbitrary"),
        name="ffn",
    )(x, gpre, mod.mod, mod.mod, mod.mod, mod.mod, w_gu, w_gu, w_down, gpost, mod.mod)


def _gate_kernel(fg_ref, b_ref, cum_ref, carry_ref):
    @pl.when(pl.program_id(1) == 0)
    def _():
        carry_ref[...] = jnp.zeros_like(carry_ref)

    z = fg_ref[...] + b_ref[...]
    lf = jnp.minimum(z, 0.0) - jnp.log1p(jnp.exp(-jnp.abs(z)))
    n = lf.shape[0]
    tril = (lax.broadcasted_iota(jnp.int32, (n, n), 1)
            <= lax.broadcasted_iota(jnp.int32, (n, n), 0)).astype(BF16)
    hi = lf.astype(BF16)
    r1 = lf - hi.astype(F32)
    mid = r1.astype(BF16)
    lo = (r1 - mid.astype(F32)).astype(BF16)
    cum = (_dot(tril, hi) + _dot(tril, mid)) + _dot(tril, lo) + carry_ref[...]
    cum_ref[...] = cum
    carry_ref[...] = cum[n - 1:n, :]


def _gate_cumsum(fg, b_pad, batch, tg):
    t = fg.shape[0]
    s = t // batch
    nt = s // tg
    return pl.pallas_call(
        _gate_kernel,
        out_shape=jax.ShapeDtypeStruct((t, LANES), F32),
        grid=(batch, nt),
        in_specs=[
            pl.BlockSpec((tg, LANES), lambda b, j: (b * nt + j, 0)),
            pl.BlockSpec((1, LANES), lambda b, j: (0, 0)),
        ],
        out_specs=pl.BlockSpec((tg, LANES), lambda b, j: (b * nt + j, 0)),
        scratch_shapes=[pltpu.VMEM((1, LANES), F32)],
        compiler_params=_cparams("parallel", "arbitrary"),
        name="fox_gate",
    )(fg, b_pad)


def _lane_col(x, h):
    lane = lax.broadcasted_iota(jnp.int32, x.shape, 1)
    return jnp.sum(jnp.where(lane == h, x, 0.0), axis=-1, keepdims=True)


def _decay_lanes(x, h, base):
    hi = x.astype(BF16)
    r = x - hi.astype(F32)
    mid = r.astype(BF16)
    lo = (r - mid.astype(F32)).astype(BF16)
    src = lax.broadcasted_iota(jnp.int32, (3 * LANES, LANES), 0)
    dst = lax.broadcasted_iota(jnp.int32, (3 * LANES, LANES), 1)
    place = jnp.where(jnp.logical_and(src % LANES == h, dst == base + src // LANES), 1.0, 0.0)
    moved = _dot(jnp.concatenate([hi, mid, lo], axis=1), place.astype(BF16))
    lane = lax.broadcasted_iota(jnp.int32, (1, LANES), 1)
    ones = jnp.where(jnp.logical_and(lane >= 3 - base, lane < 6 - base), 1.0, 0.0)
    return (moved + ones).astype(BF16)


def _fox_attn_kernel(q_ref, k_ref, v_ref, cum_ref, o_ref, ka_ref, vt_ref, qa_ref, kstat_ref,
                     sa_ref, sb_ref, pa_ref, pb_ref, m_ref, ala_ref, alb_ref, acc_ref,
                     *, tk, qscale):
    h = pl.program_id(1)
    i = pl.program_id(2)
    tq, dh = q_ref.shape
    nkv = vt_ref.shape[0]
    lane_row = lax.broadcasted_iota(jnp.int32, (1, LANES), 1)

    @pl.when(i == 0)
    def _():
        kn2 = jnp.zeros((1, 1), F32)
        for c in range(nkv):
            rows = slice(c * tk, (c + 1) * tk)
            kc = k_ref[rows, :]
            kf = kc.astype(F32)
            kn2 = jnp.maximum(kn2, jnp.max(jnp.sum(kf * kf, axis=1, keepdims=True),
                                           axis=0, keepdims=True))
            ka_ref[rows, :dh] = kc
            ka_ref[rows, dh:] = _decay_lanes(cum_ref[rows, :] * (-LOG2E), h, 0)
            vt_ref[c, :dh, :] = v_ref[rows, :].T
            vt_ref[c, dh:, :] = jnp.ones((vt_ref.shape[1] - dh, tk), BF16)
        stat = jnp.where(lane_row == LANES - 1, kn2, 0.0)
        for k in range(nkv // 2):
            last = (k + 1) * tq - 1
            stat = jnp.where(lane_row == k, _lane_col(cum_ref[last:last + 1, :], h) * LOG2E, stat)
        kstat_ref[...] = stat

    qs = (q_ref[...].astype(F32) * qscale).astype(BF16)
    fq = cum_ref[pl.ds(pl.multiple_of(i * tq, tq), tq), :] * LOG2E
    qa_ref[:, :dh] = qs
    qa_ref[:, dh:] = _decay_lanes(fq, h, 3)
    qf = qs.astype(F32)
    qn2 = jnp.max(jnp.sum(qf * qf, axis=1, keepdims=True), axis=0, keepdims=True)
    fq_max = _lane_col(jnp.max(fq, axis=0, keepdims=True), h)
    acc_ref[...] = jnp.zeros(acc_ref.shape, F32)

    def qk(c, s_out, lo=0):
        ka = ka_ref[pl.ds(pl.multiple_of(c * tk, tk), tk), :]
        s_out[:, lo:] = _dot_nt(ka, qa_ref[lo:, :])

    def pv(c, p_in, al_in=None):
        acc = acc_ref[...] if al_in is None else al_in[...] * acc_ref[...]
        acc_ref[...] = acc + _dot(vt_ref[c], p_in[...])

    rc = SOFTMAX_ROWS
    nrc = tk // rc

    def chunk(s_in, r, key_offset, lo):
        blk = s_in[r * rc:(r + 1) * rc, lo:]
        if key_offset is not None:
            col = lax.broadcasted_iota(jnp.int32, blk.shape, 1) + lo
            row = lax.broadcasted_iota(jnp.int32, blk.shape, 0)
            blk = jnp.where(row + (r * rc + key_offset) <= col, blk, -jnp.inf)
        return blk

    def column_max(blocks, width):
        part = jnp.full((SUBLANES, width), -jnp.inf, F32)
        for blk in blocks:
            part = jnp.maximum(part, jnp.max(blk.reshape(rc // SUBLANES, SUBLANES, width), axis=0))
        return jnp.max(part, axis=0, keepdims=True)

    def softmax_fixed(s_in, p_out, ref, key_offset=None, lo=0):
        for r in range(nrc):
            p_out[r * rc:(r + 1) * rc, lo:] = jnp.exp2(chunk(s_in, r, key_offset, lo) - ref).astype(BF16)

    def softmax(s_in, p_out, al_out, key_offset=None, lo=0):
        m = m_ref[:, lo:]
        m_new = jnp.maximum(m, column_max((chunk(s_in, r, key_offset, lo) for r in range(nrc)),
                                          tq - lo))
        for r in range(nrc):
            p_out[r * rc:(r + 1) * rc, lo:] = jnp.exp2(chunk(s_in, r, key_offset, lo) - m_new).astype(BF16)
        m_ref[:, lo:] = m_new
        al_out[:, lo:] = jnp.exp2(m - m_new)

    def older_pairs(m_min):
        live = jnp.logical_and(lane_row < i, qk_bound + fq_max - stat >= m_min - SKIP_LOG2)
        first = jnp.min(jnp.where(live, lane_row, i).astype(F32), axis=1, keepdims=True)
        return i - first[0, 0].astype(jnp.int32)

    def finish():
        o_ref[...] = (acc_ref[:dh, :] / acc_ref[dh:dh + 1, :]).T.astype(o_ref.dtype)

    def diagonal_scores():
        qk(2 * i, sa_ref)
        qk(2 * i + 1, sb_ref, tk)
        pb_ref[:, :tk] = jnp.zeros((tk, tk), BF16)

    stat = kstat_ref[...]
    kn2 = jnp.sum(jnp.where(lane_row == LANES - 1, stat, 0.0), axis=1, keepdims=True)
    qk_bound = jnp.sqrt(qn2 * kn2)
    calm = (2.0 * qk_bound)[0, 0] < SAFE_LOG2

    @pl.when(calm)
    def _():
        diagonal_scores()
        softmax_fixed(sa_ref, pa_ref, qk_bound, 0)
        pv(2 * i, pa_ref)
        softmax_fixed(sb_ref, pb_ref, qk_bound, tk, tk)
        n_old = older_pairs(qk_bound)

        def body(j, carry):
            k = i - 1 - j
            qk(2 * k, sa_ref)
            qk(2 * k + 1, sb_ref)
            pv(2 * k + 3, pb_ref)
            softmax_fixed(sa_ref, pa_ref, qk_bound)
            pv(2 * k, pa_ref)
            softmax_fixed(sb_ref, pb_ref, qk_bound)
            return carry

        lax.fori_loop(0, n_old, body, 0)
        pv(2 * (i - n_old) + 1, pb_ref)
        finish()

    @pl.when(jnp.logical_not(calm))
    def _():
        diagonal_scores()
        m_ref[...] = jnp.full(m_ref.shape, -jnp.inf, F32)
        softmax(sa_ref, pa_ref, ala_ref, 0)
        pv(2 * i, pa_ref, ala_ref)
        alb_ref[:, :tk] = jnp.ones((1, tk), F32)
        softmax(sb_ref, pb_ref, alb_ref, tk, tk)
        n_old = older_pairs(jnp.min(m_ref[...], axis=1, keepdims=True))

        def body(j, carry):
            k = i - 1 - j
            qk(2 * k, sa_ref)
            qk(2 * k + 1, sb_ref)
            pv(2 * k + 3, pb_ref, alb_ref)
            softmax(sa_ref, pa_ref, ala_ref)
            pv(2 * k, pa_ref, ala_ref)
            softmax(sb_ref, pb_ref, alb_ref)
            return carry

        lax.fori_loop(0, n_old, body, 0)
        pv(2 * (i - n_old) + 1, pb_ref, alb_ref)
        finish()


def _fox_attn(qkv, cum, batch, heads, tq):
    t = qkv.shape[0]
    dh = qkv.shape[1] // (3 * heads)
    s = t // batch
    nq = s // tq
    tk = tq // 2
    return pl.pallas_call(
        functools.partial(_fox_attn_kernel, tk=tk, qscale=dh ** -0.5 * LOG2E),
        out_shape=jax.ShapeDtypeStruct((t, heads * dh), BF16),
        grid=(batch, heads, nq),
        in_specs=[
            pl.BlockSpec((tq, dh), lambda b, h, i: (b * nq + i, h)),
            pl.BlockSpec((s, dh), lambda b, h, i: (b, heads + h)),
            pl.BlockSpec((s, dh), lambda b, h, i: (b, 2 * heads + h)),
            pl.BlockSpec((s, LANES), lambda b, h, i: (b, 0)),
        ],
        out_specs=pl.BlockSpec((tq, dh), lambda b, h, i: (b * nq + i, h)),
        scratch_shapes=[
            pltpu.VMEM((s, 2 * dh), BF16),
            pltpu.VMEM((s // tk, dh + ONES_ROWS, tk), BF16),
            pltpu.VMEM((tq, 2 * dh), BF16),
            pltpu.VMEM((1, LANES), F32),
            pltpu.VMEM((tk, tq), F32), pltpu.VMEM((tk, tq), F32),
            pltpu.VMEM((tk, tq), BF16), pltpu.VMEM((tk, tq), BF16),
            pltpu.VMEM((1, tq), F32),
            pltpu.VMEM((1, tq), F32), pltpu.VMEM((1, tq), F32),
            pltpu.VMEM((dh + ONES_ROWS, tq), F32),
        ],
        compiler_params=_cparams("parallel", "parallel", "arbitrary"),
        name="fox_attn",
    )(qkv, qkv, qkv, cum)


def _sgu_kernel(x_ref, gpre_ref, sh_ref, sc_ref, win_ref, lng_ref, lnb_ref, ws_ref, bs_ref,
                wout_ref, gpost_ref, gate_ref, o_ref, u_ref, vn_ref, gated_ref):
    x = x_ref[...]
    tm = x.shape[0]
    width = u_ref.shape[1]
    groups, chunk, _ = ws_ref.shape
    gd = width // groups
    h = _prenorm(x, gpre_ref[...], sh_ref[0], sc_ref[0]).astype(BF16)
    u_ref[...] = jax.nn.gelu(_dot(h, win_ref[:, :width]))
    v = jax.nn.gelu(_dot(h, win_ref[:, width:]))
    mu = jnp.mean(v, axis=-1, keepdims=True)
    vc = v - mu
    var = jnp.mean(vc * vc, axis=-1, keepdims=True)
    vn_ref[...] = (vc * lax.rsqrt(var + EPS) * lng_ref[...] + lnb_ref[...]).astype(BF16)
    causal = (lax.broadcasted_iota(jnp.int32, (chunk, chunk), 1)
              <= lax.broadcasted_iota(jnp.int32, (chunk, chunk), 0))
    for g in range(groups):
        wg = jnp.where(causal, ws_ref[g], jnp.zeros((), BF16))
        bias = bs_ref[:, g:g + 1]
        cols = slice(g * gd, (g + 1) * gd)
        for c in range(tm // chunk):
            rows = slice(c * chunk, (c + 1) * chunk)
            f = _dot(wg, vn_ref[rows, cols]) + bias
            gated_ref[rows, cols] = (u_ref[rows, cols] * f).astype(BF16)
    y = _dot(gated_ref[...], wout_ref[...])
    o_ref[...] = _postnorm_residual(x, y, gpost_ref[...], gate_ref[0])


def _sgu(x, gpre, gpost, mod, w_in, ln_g, ln_b, w_s, b_st, w_out, tm):
    t, d = x.shape
    width = w_out.shape[0]
    row = lambda n: pl.BlockSpec((1, n), lambda i: (0, 0))
    return pl.pallas_call(
        _sgu_kernel,
        out_shape=jax.ShapeDtypeStruct((t, d), F32),
        grid=(t // tm,),
        in_specs=[
            pl.BlockSpec((tm, d), lambda i: (i, 0)),
            row(d), mod.spec(0, tm), mod.spec(1, tm),
            _const_spec(w_in.shape),
            row(width), row(width),
            _const_spec(w_s.shape),
            _const_spec(b_st.shape),
            _const_spec(w_out.shape),
            row(d), mod.spec(2, tm),
        ],
        out_specs=pl.BlockSpec((tm, d), lambda i: (i, 0)),
        scratch_shapes=[pltpu.VMEM((tm, width), F32), pltpu.VMEM((tm, width), BF16),
                        pltpu.VMEM((tm, width), BF16)],
        compiler_params=_cparams("parallel"),
        name="sgu",
    )(x, gpre, mod.mod, mod.mod, w_in, ln_g, ln_b, w_s, b_st, w_out, gpost, mod.mod)


def _swa_attn_kernel(sink_ref, q_ref, kc_ref, kp_ref, vc_ref, vp_ref, o_ref, *, nb, grp, scale):
    n = pl.program_id(0) % nb
    blk = q_ref.shape[0]
    half = LANES // 2
    kv_cols = kc_ref.shape[1] // LANES
    npair = grp // 2
    row = lax.broadcasted_iota(jnp.int32, (blk, blk), 0)
    col = lax.broadcasted_iota(jnp.int32, (blk, blk), 1)
    keep_p = jnp.logical_and(row > col, n > 0)
    keep_c = row <= col
    keep = jnp.concatenate([keep_p, keep_c], axis=0)
    keep = jnp.concatenate([keep] * npair, axis=1)
    lo = lax.broadcasted_iota(jnp.int32, (2 * blk, LANES), 1) < half
    scores, vts = {}, {}
    for j in range(kv_cols):
        csl = slice(j * LANES, (j + 1) * LANES)
        kf = jnp.concatenate([kp_ref[:, csl], kc_ref[:, csl]], axis=0).astype(F32) * scale
        kr = pltpu.roll(kf, half, 1)
        vt = jnp.concatenate([vp_ref[:, csl], vc_ref[:, csl]], axis=0).T
        for e in range(2):
            kh = 2 * j + e
            qcols = [q_ref[:, (kh * npair + a) * LANES:(kh * npair + a + 1) * LANES]
                     for a in range(npair)]
            rhs = jnp.concatenate(qcols, axis=0)
            vts[kh] = vt[e * half:(e + 1) * half, :]
            for p in range(2):
                src = kf if p == e else kr
                kz = jnp.where(lo if p == 0 else jnp.logical_not(lo), src, 0.0).astype(BF16)
                scores[kh, p] = _dot_nt(kz, rhs)
    probs = {}
    for (kh, p), st in scores.items():
        st = jnp.where(keep, st, -jnp.inf)
        sink = jnp.concatenate(
            [jnp.full((1, blk), sink_ref[kh * grp + 2 * a + p], F32) for a in range(npair)], axis=1)
        m = jnp.maximum(jnp.max(st, axis=0, keepdims=True), sink)
        pt = jnp.exp(st - m)
        den = jnp.sum(pt, axis=0, keepdims=True) + jnp.exp(sink - m)
        probs[kh, p] = (pt.astype(BF16), den)
    for kh in range(2 * kv_cols):
        outs = [_dot(vts[kh], probs[kh, p][0]) / probs[kh, p][1] for p in range(2)]
        for a in range(npair):
            ot = jnp.concatenate([o[:, a * blk:(a + 1) * blk] for o in outs], axis=0)
            c0 = (kh * npair + a) * LANES
            o_ref[:, c0:c0 + LANES] = ot.T.astype(o_ref.dtype)


def _swa_attn(qkv, sinks, batch, hq, hk, dh):
    t = qkv.shape[0]
    nb = t // batch // BLOCK
    grp = hq // hk
    assert 2 * dh == LANES and grp % 2 == 0 and hk % 2 == 0
    prev = lambda r: jnp.maximum(r - 1, 0)
    return pl.pallas_call(
        functools.partial(_swa_attn_kernel, nb=nb, grp=grp, scale=dh ** -0.5),
        out_shape=jax.ShapeDtypeStruct((t, hq * dh), BF16),
        grid=(t // BLOCK,),
        in_specs=[
            pl.BlockSpec(memory_space=pltpu.SMEM),
            pl.BlockSpec((BLOCK, hq * dh), lambda r: (r, 0)),
            pl.BlockSpec((BLOCK, hk * dh), lambda r: (r, grp)),
            pl.BlockSpec((BLOCK, hk * dh), lambda r: (prev(r), grp)),
            pl.BlockSpec((BLOCK, hk * dh), lambda r: (r, grp + 1)),
            pl.BlockSpec((BLOCK, hk * dh), lambda r: (prev(r), grp + 1)),
        ],
        out_specs=pl.BlockSpec((BLOCK, hq * dh), lambda r: (r, 0)),
        compiler_params=_cparams("parallel"),
        name="swa_attn",
    )(sinks, qkv, qkv, qkv, qkv, qkv)


def _pad_lanes(a, n=LANES):
    return jnp.pad(a, [(0, 0)] * (a.ndim - 1) + [(0, n - a.shape[-1])])


def kernel(x, c, positions, ada_w, ada_b, mix_pre_g, mix_post_g, ffn_pre_g, ffn_post_g, ffn_w_gu, ffn_w_down, fox_w_in, fox_b_f, fox_w_out, sgu_w_in, sgu_ln_g, sgu_ln_b, sgu_w_s, sgu_b_s, sgu_w_out, swa_w_in, swa_sinks, swa_w_out):
    batch, seq, d = x.shape
    depth = ada_w.shape[0]
    t = batch * seq
    assert batch <= SUBLANES and seq % BLOCK == 0 and d % LANES == 0

    tm = _pick(seq, 512)
    tm_proj = _pick(seq, 1024)
    xf = x.reshape(t, d)

    c_pad = jnp.pad(c, ((0, SUBLANES - batch), (0, 0)))
    mod_all = _adaln(c_pad, ada_w, ada_b).reshape(depth * SUBLANES * N_MOD, 1, d)

    dff = ffn_w_down.shape[1]
    tf = _pick(dff, 512)
    ffn_gu, ffn_down = ffn_w_gu.astype(BF16), ffn_w_down.astype(BF16)
    fox_in, fox_out = fox_w_in.astype(BF16), fox_w_out.astype(BF16)
    swa_out = swa_w_out.astype(BF16)

    for i in range(depth):
        kind, j = i % N_MIXERS, i // N_MIXERS
        mod = _Mod(mod_all, i, batch, seq)
        gpre, gpost = mix_pre_g[i].reshape(1, d), mix_post_g[i].reshape(1, d)
        if kind == 0:
            heads = fox_b_f.shape[1]
            nqkv = fox_w_in.shape[2] - heads
            qkv, fg = _fox_proj(xf, gpre, mod, fox_in, _pad_lanes(fox_in[j, :, nqkv:]), j, nqkv,
                                tm_proj, _pick(nqkv, min(1536, nqkv // 2)))
            cum = _gate_cumsum(fg, _pad_lanes(fox_b_f[j].reshape(1, heads)), batch, _pick(seq, 512))
            a = _fox_attn(qkv, cum, batch, heads, _pick(seq, 1024))
            xf = _out_proj(a, fox_out, j, xf, gpost, mod, tm)
        elif kind == 1:
            groups = sgu_w_s.shape[1]
            xf = _sgu(xf, gpre, gpost, mod, sgu_w_in[j].astype(BF16),
                      sgu_ln_g[j].reshape(1, -1), sgu_ln_b[j].reshape(1, -1),
                      sgu_w_s[j].astype(BF16), _pad_lanes(sgu_b_s[j].T),
                      sgu_w_out[j].astype(BF16), _pick(seq, 256))
        else:
            dh = SWA_HEAD_DIM
            hq = swa_sinks.shape[1]
            hk = (swa_w_in.shape[2] // dh - hq) // 2
            rope = dh // 4
            half = rope // 2
            inv = ROPE_THETA ** (-jnp.arange(0, rope, 2, dtype=F32) / rope)
            lane_d = jnp.arange(LANES) % dh
            inv_l = jnp.where(lane_d < rope, inv[lane_d % half], 0.0).reshape(1, LANES).astype(F32)
            m1 = jnp.where(lane_d < half, -1.0, 0.0).reshape(1, LANES).astype(F32)
            m2 = jnp.where((lane_d >= half) & (lane_d < rope), 1.0, 0.0).reshape(1, LANES).astype(F32)
            qkv = _swa_proj(xf, gpre, mod, swa_w_in[j].astype(BF16), positions.reshape(t, 1),
                            inv_l, m1, m2, tm_proj, _pick(hk * dh, 512), (hq + hk) * dh, half)
            a = _swa_attn(qkv, swa_sinks[j], batch, hq, hk, dh)
            xf = _out_proj(a, swa_out, j, xf, gpost, mod, tm)
        xf = _ffn(xf, ffn_pre_g[i].reshape(1, d), ffn_post_g[i].reshape(1, d), mod,
                  ffn_gu, ffn_down, i, tm, tf)
    return xf.reshape(batch, seq, d)
```

```python
import functools

import jax
import jax.numpy as jnp
from jax import lax
from jax.experimental import pallas as pl
from jax.experimental.pallas import tpu as pltpu

F32 = jnp.float32
BF16 = jnp.bfloat16

EPS = 1e-6
N_MIXERS = 3
BLOCK = 128
SWA_HEAD_DIM = 64
ROPE_THETA = 500000.0
LANES = 128
SUBLANES = 8
N_MOD = 6
LOG2E = 1.4426950408889634
ONES_ROWS = 16
SOFTMAX_ROWS = 32
SAFE_LOG2 = 100.0
SKIP_LOG2 = 160.0
VMEM_LIMIT = 56 * 1024 * 1024


def _cparams(*sem):
    return pltpu.CompilerParams(dimension_semantics=sem, vmem_limit_bytes=VMEM_LIMIT)


def _const_spec(shape):
    nd = len(shape)
    return pl.BlockSpec(shape, lambda *_: (0,) * nd, pipeline_mode=pl.Buffered(1))


def _pick(n, pref):
    t = min(n, pref)
    while n % t:
        t -= LANES
    return t


def _rms(x):
    return x * lax.rsqrt(jnp.mean(x * x, axis=-1, keepdims=True) + EPS)


def _prenorm(x, g, sh, sc):
    return _rms(x) * (g * (1.0 + sc)) + sh


def _postnorm_residual(x, y, g, gate):
    return x + _rms(y) * (gate * g)


def _dot(a, b):
    return jnp.dot(a, b, preferred_element_type=F32)


def _dot_nt(a, b):
    return lax.dot_general(a, b, (((1,), (1,)), ((), ())), preferred_element_type=F32)


def _adaln_kernel(c_ref, w_ref, b_ref, o_ref):
    c = c_ref[...]
    ca = (c * jax.nn.sigmoid(c)).astype(BF16)
    o_ref[0] = _dot(ca, w_ref[0].astype(BF16)) + b_ref[0]


def _adaln(c_pad, ada_w, ada_b):
    depth, d, n = ada_w.shape
    tn = _pick(n, 1024)
    return pl.pallas_call(
        _adaln_kernel,
        out_shape=jax.ShapeDtypeStruct((depth, SUBLANES, n), F32),
        grid=(depth, n // tn),
        in_specs=[
            pl.BlockSpec((SUBLANES, d), lambda i, j: (0, 0)),
            pl.BlockSpec((1, d, tn), lambda i, j: (i, 0, j)),
            pl.BlockSpec((1, 1, tn), lambda i, j: (i, 0, j)),
        ],
        out_specs=pl.BlockSpec((1, SUBLANES, tn), lambda i, j: (i, 0, j)),
        compiler_params=_cparams("parallel", "parallel"),
        name="adaln",
    )(c_pad, ada_w, ada_b.reshape(depth, 1, n))


class _Mod:
    def __init__(self, mod, layer, batch, seq):
        self.mod = mod
        self.layer = layer
        self.batch = batch
        self.seq = seq
        self.d = mod.shape[-1]

    def spec(self, k, tm, ahead=0):
        layer, tpb = self.layer, self.seq // tm
        last = self.batch * tpb - 1
        return pl.BlockSpec(
            (1, 1, self.d),
            lambda i, *_: ((layer * SUBLANES + jnp.clip(i + ahead, 0, last) // tpb) * N_MOD + k, 0, 0))


def _x_ahead_spec(tm, d, ntiles, nj):
    first_ahead = nj - 2 if nj >= 3 else nj - 1
    return pl.BlockSpec(
        (tm, d), lambda i, j: (jnp.minimum(jnp.where(j >= first_ahead, i + 1, i), ntiles - 1), 0))


def _fox_proj_kernel(x_ref, g_ref, sh_ref, sc_ref, shn_ref, scn_ref, w_ref, wf_ref, o_ref, fg_ref,
                     h_ref):
    i, j = pl.program_id(0), pl.program_id(1)
    cur = i % 2

    @pl.when(jnp.logical_and(i == 0, j == 0))
    def _():
        h_ref[0] = _prenorm(x_ref[...], g_ref[...], sh_ref[0], sc_ref[0]).astype(BF16)

    @pl.when(j == 0)
    def _():
        h = h_ref[cur]
        fg_ref[...] = _dot(h, wf_ref[...])
        o_ref[...] = _dot(h, w_ref[...]).astype(o_ref.dtype)

    @pl.when(jnp.logical_and(j > 0, j < pl.num_programs(1) - 1))
    def _():
        o_ref[...] = _dot(h_ref[cur], w_ref[...]).astype(o_ref.dtype)

    @pl.when(j == pl.num_programs(1) - 1)
    def _():
        h_ref[1 - cur] = _prenorm(x_ref[...], g_ref[...], shn_ref[0], scn_ref[0]).astype(BF16)
        o_ref[...] = _dot(h_ref[cur], w_ref[...]).astype(o_ref.dtype)


def _fox_proj(x, gain, mod, w, wf, layer, n, tm, tn):
    t, d = x.shape
    nj = n // tn
    assert nj >= 2
    return pl.pallas_call(
        _fox_proj_kernel,
        out_shape=(jax.ShapeDtypeStruct((t, n), BF16), jax.ShapeDtypeStruct((t, LANES), F32)),
        grid=(t // tm, nj),
        in_specs=[
            _x_ahead_spec(tm, d, t // tm, nj),
            pl.BlockSpec((1, d), lambda i, j: (0, 0)),
            mod.spec(0, tm), mod.spec(1, tm), mod.spec(0, tm, 1), mod.spec(1, tm, 1),
            pl.BlockSpec((None, d, tn), lambda i, j: (layer, 0, j)),
            _const_spec((d, LANES)),
        ],
        out_specs=(pl.BlockSpec((tm, tn), lambda i, j: (i, j)),
                   pl.BlockSpec((tm, LANES), lambda i, j: (i, 0))),
        scratch_shapes=[pltpu.VMEM((2, tm, d), BF16)],
        compiler_params=_cparams("arbitrary", "arbitrary"),
        name="fox_proj",
    )(x, gain, mod.mod, mod.mod, mod.mod, mod.mod, w, wf)


def _swa_proj_kernel(x_ref, g_ref, sh_ref, sc_ref, shn_ref, scn_ref, w_ref, pos_ref, inv_ref,
                     m1_ref, m2_ref, o_ref, h_ref, cos_ref, s1_ref, s2_ref, *, n_rope_tiles, shift):
    i, j = pl.program_id(0), pl.program_id(1)
    cur = i % 2

    def stage(slot, sh, sc):
        h_ref[slot] = _prenorm(x_ref[...], g_ref[...], sh[0], sc[0]).astype(BF16)
        ang = pos_ref[...].astype(F32) * inv_ref[...]
        sn = jnp.sin(ang)
        cos_ref[slot] = jnp.cos(ang)
        s1_ref[slot] = sn * m1_ref[...]
        s2_ref[slot] = sn * m2_ref[...]

    def rope_tile():
        acc = _dot(h_ref[cur], w_ref[...])
        cs, s1, s2 = cos_ref[cur], s1_ref[cur], s2_ref[cur]
        for c in range(acc.shape[1] // LANES):
            a = acc[:, c * LANES:(c + 1) * LANES]
            r = a * cs + pltpu.roll(a, LANES - shift, 1) * s1 + pltpu.roll(a, shift, 1) * s2
            o_ref[:, c * LANES:(c + 1) * LANES] = r.astype(o_ref.dtype)

    @pl.when(jnp.logical_and(i == 0, j == 0))
    def _():
        stage(0, sh_ref, sc_ref)

    @pl.when(j < n_rope_tiles)
    def _():
        rope_tile()

    @pl.when(jnp.logical_and(j >= n_rope_tiles, j < pl.num_programs(1) - 1))
    def _():
        o_ref[...] = _dot(h_ref[cur], w_ref[...]).astype(o_ref.dtype)

    @pl.when(j == pl.num_programs(1) - 1)
    def _():
        stage(1 - cur, shn_ref, scn_ref)
        o_ref[...] = _dot(h_ref[cur], w_ref[...]).astype(o_ref.dtype)


def _swa_proj(x, gain, mod, w, pos, inv_l, m1, m2, tm, tn, n_rope_cols, shift):
    t, d = x.shape
    n = w.shape[1]
    nt, nj = t // tm, n // tn
    assert n_rope_cols // tn < nj
    row = lambda shape: pl.BlockSpec(shape, lambda i, j: (0, 0))
    ahead = lambda i, j: (jnp.where(j == nj - 1, jnp.minimum(i + 1, nt - 1), i), 0)
    return pl.pallas_call(
        functools.partial(_swa_proj_kernel, n_rope_tiles=n_rope_cols // tn, shift=shift),
        out_shape=jax.ShapeDtypeStruct((t, n), BF16),
        grid=(nt, nj),
        in_specs=[
            _x_ahead_spec(tm, d, nt, nj),
            row((1, d)),
            mod.spec(0, tm), mod.spec(1, tm), mod.spec(0, tm, 1), mod.spec(1, tm, 1),
            pl.BlockSpec((d, tn), lambda i, j: (0, j)),
            pl.BlockSpec((tm, 1), ahead),
            row((1, LANES)), row((1, LANES)), row((1, LANES)),
        ],
        out_specs=pl.BlockSpec((tm, tn), lambda i, j: (i, j)),
        scratch_shapes=[pltpu.VMEM((2, tm, d), BF16)] + [pltpu.VMEM((2, tm, LANES), F32)] * 3,
        compiler_params=_cparams("arbitrary", "arbitrary"),
        name="swa_proj",
    )(x, gain, mod.mod, mod.mod, mod.mod, mod.mod, w, pos, inv_l, m1, m2)


def _out_kernel(a_ref, w_ref, x_ref, g_ref, gate_ref, o_ref):
    y = _dot(a_ref[...], w_ref[...])
    o_ref[...] = _postnorm_residual(x_ref[...], y, g_ref[...], gate_ref[0])


def _out_proj(a, w, layer, x, gain, mod, tm):
    t, k = a.shape
    d = w.shape[2]
    return pl.pallas_call(
        _out_kernel,
        out_shape=jax.ShapeDtypeStruct((t, d), F32),
        grid=(t // tm,),
        in_specs=[
            pl.BlockSpec((tm, k), lambda i: (i, 0)),
            pl.BlockSpec((None, k, d), lambda i: (layer, 0, 0), pipeline_mode=pl.Buffered(1)),
            pl.BlockSpec((tm, d), lambda i: (i, 0)),
            pl.BlockSpec((1, d), lambda i: (0, 0)),
            mod.spec(2, tm),
        ],
        out_specs=pl.BlockSpec((tm, d), lambda i: (i, 0)),
        compiler_params=_cparams("parallel"),
        name="out_proj",
    )(a, w, x, gain, mod.mod)


def _ffn_kernel(x_ref, gpre_ref, sh_ref, sc_ref, shn_ref, scn_ref, wg_ref, wu_ref, wd_ref,
                gpost_ref, gate_ref, o_ref, h_ref, xs_ref, acc_ref):
    i, j = pl.program_id(0), pl.program_id(1)
    last = pl.num_programs(1) - 1
    cur = i % 2

    def stage(slot, sh, sc):
        x = x_ref[...]
        xs_ref[slot] = x
        h_ref[slot] = _prenorm(x, gpre_ref[...], sh[0], sc[0]).astype(BF16)

    def chunk():
        h = h_ref[cur]
        g = _dot(h, wg_ref[...])
        u = _dot(h, wu_ref[...])
        a = (g * jax.nn.sigmoid(g) * u).astype(BF16)
        return _dot(a, wd_ref[...])

    @pl.when(jnp.logical_and(i == 0, j == 0))
    def _():
        stage(0, sh_ref, sc_ref)

    @pl.when(j == 0)
    def _():
        acc_ref[...] = chunk()

    @pl.when(jnp.logical_and(j > 0, j < last))
    def _():
        acc_ref[...] += chunk()

    @pl.when(j == last)
    def _():
        stage(1 - cur, shn_ref, scn_ref)
        y = acc_ref[...] + chunk()
        o_ref[...] = _postnorm_residual(xs_ref[cur], y, gpost_ref[...], gate_ref[0])


def _ffn(x, gpre, gpost, mod, w_gu, w_down, layer, tm, tf):
    t, d = x.shape
    dff = w_down.shape[1]
    nf = dff // tf
    assert nf >= 2
    row = lambda: pl.BlockSpec((1, d), lambda i, j: (0, 0))
    return pl.pallas_call(
        _ffn_kernel,
        out_shape=jax.ShapeDtypeStruct((t, d), F32),
        grid=(t // tm, nf),
        in_specs=[
            _x_ahead_spec(tm, d, t // tm, nf),
            row(), mod.spec(3, tm), mod.spec(4, tm), mod.spec(3, tm, 1), mod.spec(4, tm, 1),
            pl.BlockSpec((None, d, tf), lambda i, j: (layer, 0, j)),
            pl.BlockSpec((None, d, tf), lambda i, j: (layer, 0, nf + j)),
            pl.BlockSpec((None, tf, d), lambda i, j: (layer, j, 0)),
            row(), mod.spec(5, tm),
        ],
        out_specs=pl.BlockSpec((tm, d), lambda i, j: (i, 0)),
        scratch_shapes=[pltpu.VMEM((2, tm, d), BF16), pltpu.VMEM((2, tm, d), F32),
                        pltpu.VMEM((tm, d), F32)],
        compiler_params=_cparams("arbitrary", "arbitrary"),
        name="ffn",
    )(x, gpre, mod.mod, mod.mod, mod.mod, mod.mod, w_gu, w_gu, w_down, gpost, mod.mod)


def _gate_kernel(fg_ref, b_ref, cum_ref, carry_ref):
    @pl.when(pl.program_id(1) == 0)
    def _():
        carry_ref[...] = jnp.zeros_like(carry_ref)

    z = fg_ref[...] + b_ref[...]
    lf = jnp.minimum(z, 0.0) - jnp.log1p(jnp.exp(-jnp.abs(z)))
    n = lf.shape[0]
    tril = (lax.broadcasted_iota(jnp.int32, (n, n), 1)
            <= lax.broadcasted_iota(jnp.int32, (n, n), 0)).astype(BF16)
    hi = lf.astype(BF16)
    r1 = lf - hi.astype(F32)
    mid = r1.astype(BF16)
    lo = (r1 - mid.astype(F32)).astype(BF16)
    cum = (_dot(tril, hi) + _dot(tril, mid)) + _dot(tril, lo) + carry_ref[...]
    cum_ref[...] = cum
    carry_ref[...] = cum[n - 1:n, :]


def _gate_cumsum(fg, b_pad, batch, tg):
    t = fg.shape[0]
    s = t // batch
    nt = s // tg
    return pl.pallas_call(
        _gate_kernel,
        out_shape=jax.ShapeDtypeStruct((t, LANES), F32),
        grid=(batch, nt),
        in_specs=[
            pl.BlockSpec((tg, LANES), lambda b, j: (b * nt + j, 0)),
            pl.BlockSpec((1, LANES), lambda b, j: (0, 0)),
        ],
        out_specs=pl.BlockSpec((tg, LANES), lambda b, j: (b * nt + j, 0)),
        scratch_shapes=[pltpu.VMEM((1, LANES), F32)],
        compiler_params=_cparams("parallel", "arbitrary"),
        name="fox_gate",
    )(fg, b_pad)


def _lane_col(x, h):
    lane = lax.broadcasted_iota(jnp.int32, x.shape, 1)
    return jnp.sum(jnp.where(lane == h, x, 0.0), axis=-1, keepdims=True)


def _decay_lanes(x, h, base):
    hi = x.astype(BF16)
    r = x - hi.astype(F32)
    mid = r.astype(BF16)
    lo = (r - mid.astype(F32)).astype(BF16)
    src = lax.broadcasted_iota(jnp.int32, (3 * LANES, LANES), 0)
    dst = lax.broadcasted_iota(jnp.int32, (3 * LANES, LANES), 1)
    place = jnp.where(jnp.logical_and(src % LANES == h, dst == base + src // LANES), 1.0, 0.0)
    moved = _dot(jnp.concatenate([hi, mid, lo], axis=1), place.astype(BF16))
    lane = lax.broadcasted_iota(jnp.int32, (1, LANES), 1)
    ones = jnp.where(jnp.logical_and(lane >= 3 - base, lane < 6 - base), 1.0, 0.0)
    return (moved + ones).astype(BF16)


def _fox_attn_kernel(q_ref, k_ref, v_ref, cum_ref, o_ref, ka_ref, vt_ref, qa_ref, kstat_ref,
                     sa_ref, sb_ref, pa_ref, pb_ref, m_ref, ala_ref, alb_ref, acc_ref,
                     *, tk, qscale):
    h = pl.program_id(1)
    i = pl.program_id(2)
    tq, dh = q_ref.shape
    nkv = vt_ref.shape[0]
    lane_row = lax.broadcasted_iota(jnp.int32, (1, LANES), 1)

    @pl.when(i == 0)
    def _():
        kn2 = jnp.zeros((1, 1), F32)
        for c in range(nkv):
            rows = slice(c * tk, (c + 1) * tk)
            kc = k_ref[rows, :]
            kf = kc.astype(F32)
            kn2 = jnp.maximum(kn2, jnp.max(jnp.sum(kf * kf, axis=1, keepdims=True),
                                           axis=0, keepdims=True))
            ka_ref[rows, :dh] = kc
            ka_ref[rows, dh:] = _decay_lanes(cum_ref[rows, :] * (-LOG2E), h, 0)
            vt_ref[c, :dh, :] = v_ref[rows, :].T
            vt_ref[c, dh:, :] = jnp.ones((vt_ref.shape[1] - dh, tk), BF16)
        stat = jnp.where(lane_row == LANES - 1, kn2, 0.0)
        for k in range(nkv // 2):
            last = (k + 1) * tq - 1
            stat = jnp.where(lane_row == k, _lane_col(cum_ref[last:last + 1, :], h) * LOG2E, stat)
        kstat_ref[...] = stat

    qs = (q_ref[...].astype(F32) * qscale).astype(BF16)
    fq = cum_ref[pl.ds(pl.multiple_of(i * tq, tq), tq), :] * LOG2E
    qa_ref[:, :dh] = qs
    qa_ref[:, dh:] = _decay_lanes(fq, h, 3)
    qf = qs.astype(F32)
    qn2 = jnp.max(jnp.sum(qf * qf, axis=1, keepdims=True), axis=0, keepdims=True)
    fq_max = _lane_col(jnp.max(fq, axis=0, keepdims=True), h)
    acc_ref[...] = jnp.zeros(acc_ref.shape, F32)

    def qk(c, s_out, lo=0):
        ka = ka_ref[pl.ds(pl.multiple_of(c * tk, tk), tk), :]
        s_out[:, lo:] = _dot_nt(ka, qa_ref[lo:, :])

    def pv(c, p_in, al_in=None):
        acc = acc_ref[...] if al_in is None else al_in[...] * acc_ref[...]
        acc_ref[...] = acc + _dot(vt_ref[c], p_in[...])

    rc = SOFTMAX_ROWS
    nrc = tk // rc

    def chunk(s_in, r, key_offset, lo):
        blk = s_in[r * rc:(r + 1) * rc, lo:]
        if key_offset is not None:
            col = lax.broadcasted_iota(jnp.int32, blk.shape, 1) + lo
            row = lax.broadcasted_iota(jnp.int32, blk.shape, 0)
            blk = jnp.where(row + (r * rc + key_offset) <= col, blk, -jnp.inf)
        return blk

    def column_max(blocks, width):
        part = jnp.full((SUBLANES, width), -jnp.inf, F32)
        for blk in blocks:
            part = jnp.maximum(part, jnp.max(blk.reshape(rc // SUBLANES, SUBLANES, width), axis=0))
        return jnp.max(part, axis=0, keepdims=True)

    def softmax_fixed(s_in, p_out, ref, key_offset=None, lo=0):
        for r in range(nrc):
            p_out[r * rc:(r + 1) * rc, lo:] = jnp.exp2(chunk(s_in, r, key_offset, lo) - ref).astype(BF16)

    def softmax(s_in, p_out, al_out, key_offset=None, lo=0):
        m = m_ref[:, lo:]
        m_new = jnp.maximum(m, column_max((chunk(s_in, r, key_offset, lo) for r in range(nrc)),
                                          tq - lo))
        for r in range(nrc):
            p_out[r * rc:(r + 1) * rc, lo:] = jnp.exp2(chunk(s_in, r, key_offset, lo) - m_new).astype(BF16)
        m_ref[:, lo:] = m_new
        al_out[:, lo:] = jnp.exp2(m - m_new)

    def older_pairs(m_min):
        live = jnp.logical_and(lane_row < i, qk_bound + fq_max - stat >= m_min - SKIP_LOG2)
        first = jnp.min(jnp.where(live, lane_row, i).astype(F32), axis=1, keepdims=True)
        return i - first[0, 0].astype(jnp.int32)

    def finish():
        o_ref[...] = (acc_ref[:dh, :] / acc_ref[dh:dh + 1, :]).T.astype(o_ref.dtype)

    def diagonal_scores():
        qk(2 * i, sa_ref)
        qk(2 * i + 1, sb_ref, tk)
        pb_ref[:, :tk] = jnp.zeros((tk, tk), BF16)

    stat = kstat_ref[...]
    kn2 = jnp.sum(jnp.where(lane_row == LANES - 1, stat, 0.0), axis=1, keepdims=True)
    qk_bound = jnp.sqrt(qn2 * kn2)
    calm = (2.0 * qk_bound)[0, 0] < SAFE_LOG2

    @pl.when(calm)
    def _():
        diagonal_scores()
        softmax_fixed(sa_ref, pa_ref, qk_bound, 0)
        pv(2 * i, pa_ref)
        softmax_fixed(sb_ref, pb_ref, qk_bound, tk, tk)
        n_old = older_pairs(qk_bound)

        def body(j, carry):
            k = i - 1 - j
            qk(2 * k, sa_ref)
            qk(2 * k + 1, sb_ref)
            pv(2 * k + 3, pb_ref)
            softmax_fixed(sa_ref, pa_ref, qk_bound)
            pv(2 * k, pa_ref)
            softmax_fixed(sb_ref, pb_ref, qk_bound)
            return carry

        lax.fori_loop(0, n_old, body, 0)
        pv(2 * (i - n_old) + 1, pb_ref)
        finish()

    @pl.when(jnp.logical_not(calm))
    def _():
        diagonal_scores()
        m_ref[...] = jnp.full(m_ref.shape, -jnp.inf, F32)
        softmax(sa_ref, pa_ref, ala_ref, 0)
        pv(2 * i, pa_ref, ala_ref)
        alb_ref[:, :tk] = jnp.ones((1, tk), F32)
        softmax(sb_ref, pb_ref, alb_ref, tk, tk)
        n_old = older_pairs(jnp.min(m_ref[...], axis=1, keepdims=True))

        def body(j, carry):
            k = i - 1 - j
            qk(2 * k, sa_ref)
            qk(2 * k + 1, sb_ref)
            pv(2 * k + 3, pb_ref, alb_ref)
            softmax(sa_ref, pa_ref, ala_ref)
            pv(2 * k, pa_ref, ala_ref)
            softmax(sb_ref, pb_ref, alb_ref)
            return carry

        lax.fori_loop(0, n_old, body, 0)
        pv(2 * (i - n_old) + 1, pb_ref, alb_ref)
        finish()


def _fox_attn(qkv, cum, batch, heads, tq):
    t = qkv.shape[0]
    dh = qkv.shape[1] // (3 * heads)
    s = t // batch
    nq = s // tq
    tk = tq // 2
    return pl.pallas_call(
        functools.partial(_fox_attn_kernel, tk=tk, qscale=dh ** -0.5 * LOG2E),
        out_shape=jax.ShapeDtypeStruct((t, heads * dh), BF16),
        grid=(batch, heads, nq),
        in_specs=[
            pl.BlockSpec((tq, dh), lambda b, h, i: (b * nq + i, h)),
            pl.BlockSpec((s, dh), lambda b, h, i: (b, heads + h)),
            pl.BlockSpec((s, dh), lambda b, h, i: (b, 2 * heads + h)),
            pl.BlockSpec((s, LANES), lambda b, h, i: (b, 0)),
        ],
        out_specs=pl.BlockSpec((tq, dh), lambda b, h, i: (b * nq + i, h)),
        scratch_shapes=[
            pltpu.VMEM((s, 2 * dh), BF16),
            pltpu.VMEM((s // tk, dh + ONES_ROWS, tk), BF16),
            pltpu.VMEM((tq, 2 * dh), BF16),
            pltpu.VMEM((1, LANES), F32),
            pltpu.VMEM((tk, tq), F32), pltpu.VMEM((tk, tq), F32),
            pltpu.VMEM((tk, tq), BF16), pltpu.VMEM((tk, tq), BF16),
            pltpu.VMEM((1, tq), F32),
            pltpu.VMEM((1, tq), F32), pltpu.VMEM((1, tq), F32),
            pltpu.VMEM((dh + ONES_ROWS, tq), F32),
        ],
        compiler_params=_cparams("parallel", "parallel", "arbitrary"),
        name="fox_attn",
    )(qkv, qkv, qkv, cum)


def _sgu_kernel(x_ref, gpre_ref, sh_ref, sc_ref, win_ref, lng_ref, lnb_ref, ws_ref, bs_ref,
                wout_ref, gpost_ref, gate_ref, o_ref, u_ref, vn_ref, gated_ref):
    x = x_ref[...]
    tm = x.shape[0]
    width = u_ref.shape[1]
    groups, chunk, _ = ws_ref.shape
    gd = width // groups
    h = _prenorm(x, gpre_ref[...], sh_ref[0], sc_ref[0]).astype(BF16)
    zv = _dot(h, win_ref[:, width:])
    zu = _dot(h, win_ref[:, :width])
    v = jax.nn.gelu(zv)
    mu = jnp.mean(v, axis=-1, keepdims=True)
    vc = v - mu
    var = jnp.mean(vc * vc, axis=-1, keepdims=True)
    vn_ref[...] = (vc * lax.rsqrt(var + EPS) * lng_ref[...] + lnb_ref[...]).astype(BF16)
    u_ref[...] = jax.nn.gelu(zu)
    causal = (lax.broadcasted_iota(jnp.int32, (chunk, chunk), 1)
              <= lax.broadcasted_iota(jnp.int32, (chunk, chunk), 0))
    for g in range(groups):
        wg = jnp.where(causal, ws_ref[g], jnp.zeros((), BF16))
        bias = bs_ref[:, g:g + 1]
        cols = slice(g * gd, (g + 1) * gd)
        for c in range(tm // chunk):
            rows = slice(c * chunk, (c + 1) * chunk)
            f = _dot(wg, vn_ref[rows, cols]) + bias
            gated_ref[rows, cols] = (u_ref[rows, cols] * f).astype(BF16)
    y = _dot(gated_ref[...], wout_ref[...])
    o_ref[...] = _postnorm_residual(x, y, gpost_ref[...], gate_ref[0])


def _sgu(x, gpre, gpost, mod, w_in, ln_g, ln_b, w_s, b_st, w_out, tm):
    t, d = x.shape
    width = w_out.shape[0]
    row = lambda n: pl.BlockSpec((1, n), lambda i: (0, 0))
    return pl.pallas_call(
        _sgu_kernel,
        out_shape=jax.ShapeDtypeStruct((t, d), F32),
        grid=(t // tm,),
        in_specs=[
            pl.BlockSpec((tm, d), lambda i: (i, 0)),
            row(d), mod.spec(0, tm), mod.spec(1, tm),
            _const_spec(w_in.shape),
            row(width), row(width),
            _const_spec(w_s.shape),
            _const_spec(b_st.shape),
            _const_spec(w_out.shape),
            row(d), mod.spec(2, tm),
        ],
        out_specs=pl.BlockSpec((tm, d), lambda i: (i, 0)),
        scratch_shapes=[pltpu.VMEM((tm, width), F32), pltpu.VMEM((tm, width), BF16),
                        pltpu.VMEM((tm, width), BF16)],
        compiler_params=_cparams("parallel"),
        name="sgu",
    )(x, gpre, mod.mod, mod.mod, w_in, ln_g, ln_b, w_s, b_st, w_out, gpost, mod.mod)


def _swa_attn_kernel(sink_ref, q_ref, kc_ref, kp_ref, vc_ref, vp_ref, o_ref, *, nb, grp, scale):
    n = pl.program_id(0) % nb
    blk = q_ref.shape[0]
    half = LANES // 2
    kv_cols = kc_ref.shape[1] // LANES
    npair = grp // 2
    row = lax.broadcasted_iota(jnp.int32, (blk, blk), 0)
    col = lax.broadcasted_iota(jnp.int32, (blk, blk), 1)
    keep_p = jnp.logical_and(row > col, n > 0)
    keep_c = row <= col
    keep = jnp.concatenate([keep_p, keep_c], axis=0)
    keep = jnp.concatenate([keep] * npair, axis=1)
    lo = lax.broadcasted_iota(jnp.int32, (2 * blk, LANES), 1) < half
    scores, vts = {}, {}
    for j in range(kv_cols):
        csl = slice(j * LANES, (j + 1) * LANES)
        kf = jnp.concatenate([kp_ref[:, csl], kc_ref[:, csl]], axis=0).astype(F32) * scale
        kr = pltpu.roll(kf, half, 1)
        vt = jnp.concatenate([vp_ref[:, csl], vc_ref[:, csl]], axis=0).T
        for e in range(2):
            kh = 2 * j + e
            qcols = [q_ref[:, (kh * npair + a) * LANES:(kh * npair + a + 1) * LANES]
                     for a in range(npair)]
            rhs = jnp.concatenate(qcols, axis=0)
            vts[kh] = vt[e * half:(e + 1) * half, :]
            for p in range(2):
                src = kf if p == e else kr
                kz = jnp.where(lo if p == 0 else jnp.logical_not(lo), src, 0.0).astype(BF16)
                scores[kh, p] = _dot_nt(kz, rhs)
    probs = {}
    for (kh, p), st in scores.items():
        st = jnp.where(keep, st, -jnp.inf)
        sink = jnp.concatenate(
            [jnp.full((1, blk), sink_ref[kh * grp + 2 * a + p], F32) for a in range(npair)], axis=1)
        m = jnp.maximum(jnp.max(st, axis=0, keepdims=True), sink)
        pt = jnp.exp(st - m)
        den = jnp.sum(pt, axis=0, keepdims=True) + jnp.exp(sink - m)
        probs[kh, p] = (pt.astype(BF16), den)
    for kh in range(2 * kv_cols):
        outs = [_dot(vts[kh], probs[kh, p][0]) / probs[kh, p][1] for p in range(2)]
        for a in range(npair):
            ot = jnp.concatenate([o[:, a * blk:(a + 1) * blk] for o in outs], axis=0)
            c0 = (kh * npair + a) * LANES
            o_ref[:, c0:c0 + LANES] = ot.T.astype(o_ref.dtype)


def _swa_attn(qkv, sinks, batch, hq, hk, dh):
    t = qkv.shape[0]
    nb = t // batch // BLOCK
    grp = hq // hk
    assert 2 * dh == LANES and grp % 2 == 0 and hk % 2 == 0
    prev = lambda r: jnp.maximum(r - 1, 0)
    return pl.pallas_call(
        functools.partial(_swa_attn_kernel, nb=nb, grp=grp, scale=dh ** -0.5),
        out_shape=jax.ShapeDtypeStruct((t, hq * dh), BF16),
        grid=(t // BLOCK,),
        in_specs=[
            pl.BlockSpec(memory_space=pltpu.SMEM),
            pl.BlockSpec((BLOCK, hq * dh), lambda r: (r, 0)),
            pl.BlockSpec((BLOCK, hk * dh), lambda r: (r, grp)),
            pl.BlockSpec((BLOCK, hk * dh), lambda r: (prev(r), grp)),
            pl.BlockSpec((BLOCK, hk * dh), lambda r: (r, grp + 1)),
            pl.BlockSpec((BLOCK, hk * dh), lambda r: (prev(r), grp + 1)),
        ],
        out_specs=pl.BlockSpec((BLOCK, hq * dh), lambda r: (r, 0)),
        compiler_params=_cparams("parallel"),
        name="swa_attn",
    )(sinks, qkv, qkv, qkv, qkv, qkv)


def _pad_lanes(a, n=LANES):
    return jnp.pad(a, [(0, 0)] * (a.ndim - 1) + [(0, n - a.shape[-1])])


def kernel(x, c, positions, ada_w, ada_b, mix_pre_g, mix_post_g, ffn_pre_g, ffn_post_g, ffn_w_gu, ffn_w_down, fox_w_in, fox_b_f, fox_w_out, sgu_w_in, sgu_ln_g, sgu_ln_b, sgu_w_s, sgu_b_s, sgu_w_out, swa_w_in, swa_sinks, swa_w_out):
    batch, seq, d = x.shape
    depth = ada_w.shape[0]
    t = batch * seq
    assert batch <= SUBLANES and seq % BLOCK == 0 and d % LANES == 0

    tm = _pick(seq, 512)
    tm_proj = _pick(seq, 1024)
    xf = x.reshape(t, d)

    c_pad = jnp.pad(c, ((0, SUBLANES - batch), (0, 0)))
    mod_all = _adaln(c_pad, ada_w, ada_b).reshape(depth * SUBLANES * N_MOD, 1, d)

    dff = ffn_w_down.shape[1]
    tf = _pick(dff, 512)
    ffn_gu, ffn_down = ffn_w_gu.astype(BF16), ffn_w_down.astype(BF16)
    fox_in, fox_out = fox_w_in.astype(BF16), fox_w_out.astype(BF16)
    swa_out = swa_w_out.astype(BF16)

    for i in range(depth):
        kind, j = i % N_MIXERS, i // N_MIXERS
        mod = _Mod(mod_all, i, batch, seq)
        gpre, gpost = mix_pre_g[i].reshape(1, d), mix_post_g[i].reshape(1, d)
        if kind == 0:
            heads = fox_b_f.shape[1]
            nqkv = fox_w_in.shape[2] - heads
            qkv, fg = _fox_proj(xf, gpre, mod, fox_in, _pad_lanes(fox_in[j, :, nqkv:]), j, nqkv,
                                tm_proj, _pick(nqkv, min(1536, nqkv // 2)))
            cum = _gate_cumsum(fg, _pad_lanes(fox_b_f[j].reshape(1, heads)), batch, _pick(seq, 512))
            a = _fox_attn(qkv, cum, batch, heads, _pick(seq, 1024))
            xf = _out_proj(a, fox_out, j, xf, gpost, mod, tm)
        elif kind == 1:
            groups = sgu_w_s.shape[1]
            xf = _sgu(xf, gpre, gpost, mod, sgu_w_in[j].astype(BF16),
                      sgu_ln_g[j].reshape(1, -1), sgu_ln_b[j].reshape(1, -1),
                      sgu_w_s[j].astype(BF16), _pad_lanes(sgu_b_s[j].T),
                      sgu_w_out[j].astype(BF16), _pick(seq, 256))
        else:
            dh = SWA_HEAD_DIM
            hq = swa_sinks.shape[1]
            hk = (swa_w_in.shape[2] // dh - hq) // 2
            rope = dh // 4
            half = rope // 2
            inv = ROPE_THETA ** (-jnp.arange(0, rope, 2, dtype=F32) / rope)
            lane_d = jnp.arange(LANES) % dh
            inv_l = jnp.where(lane_d < rope, inv[lane_d % half], 0.0).reshape(1, LANES).astype(F32)
            m1 = jnp.where(lane_d < half, -1.0, 0.0).reshape(1, LANES).astype(F32)
            m2 = jnp.where((lane_d >= half) & (lane_d < rope), 1.0, 0.0).reshape(1, LANES).astype(F32)
            qkv = _swa_proj(xf, gpre, mod, swa_w_in[j].astype(BF16), positions.reshape(t, 1),
                            inv_l, m1, m2, tm_proj, _pick(hk * dh, 512), (hq + hk) * dh, half)
            a = _swa_attn(qkv, swa_sinks[j], batch, hq, hk, dh)
            xf = _out_proj(a, swa_out, j, xf, gpost, mod, tm)
        xf = _ffn(xf, ffn_pre_g[i].reshape(1, d), ffn_post_g[i].reshape(1, d), mod,
                  ffn_gu, ffn_down, i, tm, tf)
    return xf.reshape(batch, seq, d)
```

```python
import functools

import jax
import jax.numpy as jnp
from jax import lax
from jax.experimental import pallas as pl
from jax.experimental.pallas import tpu as pltpu

F32 = jnp.float32
BF16 = jnp.bfloat16

EPS = 1e-6
N_MIXERS = 3
BLOCK = 128
SWA_HEAD_DIM = 64
ROPE_THETA = 500000.0
LANES = 128
SUBLANES = 8
N_MOD = 6
LOG2E = 1.4426950408889634
ONES_ROWS = 16
SOFTMAX_ROWS = 32
SAFE_LOG2 = 100.0
SKIP_LOG2 = 160.0
VMEM_LIMIT = 56 * 1024 * 1024


def _cparams(*sem):
    return pltpu.CompilerParams(dimension_semantics=sem, vmem_limit_bytes=VMEM_LIMIT)


def _const_spec(shape):
    nd = len(shape)
    return pl.BlockSpec(shape, lambda *_: (0,) * nd, pipeline_mode=pl.Buffered(1))


def _pick(n, pref):
    t = min(n, pref)
    while n % t:
        t -= LANES
    return t


def _rms(x):
    return x * lax.rsqrt(jnp.mean(x * x, axis=-1, keepdims=True) + EPS)


def _prenorm(x, g, sh, sc):
    return _rms(x) * (g * (1.0 + sc)) + sh


def _postnorm_residual(x, y, g, gate):
    return x + _rms(y) * (gate * g)


def _dot(a, b):
    return jnp.dot(a, b, preferred_element_type=F32)


def _dot_nt(a, b):
    return lax.dot_general(a, b, (((1,), (1,)), ((), ())), preferred_element_type=F32)


def _adaln_kernel(c_ref, w_ref, b_ref, o_ref):
    c = c_ref[...]
    ca = (c * jax.nn.sigmoid(c)).astype(BF16)
    o_ref[0] = _dot(ca, w_ref[0].astype(BF16)) + b_ref[0]


def _adaln(c_pad, ada_w, ada_b):
    depth, d, n = ada_w.shape
    tn = _pick(n, 1024)
    return pl.pallas_call(
        _adaln_kernel,
        out_shape=jax.ShapeDtypeStruct((depth, SUBLANES, n), F32),
        grid=(depth, n // tn),
        in_specs=[
            pl.BlockSpec((SUBLANES, d), lambda i, j: (0, 0)),
            pl.BlockSpec((1, d, tn), lambda i, j: (i, 0, j)),
            pl.BlockSpec((1, 1, tn), lambda i, j: (i, 0, j)),
        ],
        out_specs=pl.BlockSpec((1, SUBLANES, tn), lambda i, j: (i, 0, j)),
        compiler_params=_cparams("parallel", "parallel"),
        name="adaln",
    )(c_pad, ada_w, ada_b.reshape(depth, 1, n))


class _Mod:
    def __init__(self, mod, layer, batch, seq):
        self.mod = mod
        self.layer = layer
        self.batch = batch
        self.seq = seq
        self.d = mod.shape[-1]

    def spec(self, k, tm, ahead=0):
        layer, tpb = self.layer, self.seq // tm
        last = self.batch * tpb - 1
        return pl.BlockSpec(
            (1, 1, self.d),
            lambda i, *_: ((layer * SUBLANES + jnp.clip(i + ahead, 0, last) // tpb) * N_MOD + k, 0, 0))


def _x_ahead_spec(tm, d, ntiles, nj):
    first_ahead = nj - 2 if nj >= 3 else nj - 1
    return pl.BlockSpec(
        (tm, d), lambda i, j: (jnp.minimum(jnp.where(j >= first_ahead, i + 1, i), ntiles - 1), 0))


def _fox_proj_kernel(x_ref, g_ref, sh_ref, sc_ref, shn_ref, scn_ref, w_ref, wf_ref, o_ref, fg_ref,
                     h_ref):
    i, j = pl.program_id(0), pl.program_id(1)
    cur = i % 2

    @pl.when(jnp.logical_and(i == 0, j == 0))
    def _():
        h_ref[0] = _prenorm(x_ref[...], g_ref[...], sh_ref[0], sc_ref[0]).astype(BF16)

    @pl.when(j == 0)
    def _():
        h = h_ref[cur]
        fg_ref[...] = _dot(h, wf_ref[...])
        o_ref[...] = _dot(h, w_ref[...]).astype(o_ref.dtype)

    @pl.when(jnp.logical_and(j > 0, j < pl.num_programs(1) - 1))
    def _():
        o_ref[...] = _dot(h_ref[cur], w_ref[...]).astype(o_ref.dtype)

    @pl.when(j == pl.num_programs(1) - 1)
    def _():
        h_ref[1 - cur] = _prenorm(x_ref[...], g_ref[...], shn_ref[0], scn_ref[0]).astype(BF16)
        o_ref[...] = _dot(h_ref[cur], w_ref[...]).astype(o_ref.dtype)


def _fox_proj(x, gain, mod, w, wf, layer, n, tm, tn):
    t, d = x.shape
    nj = n // tn
    assert nj >= 2
    return pl.pallas_call(
        _fox_proj_kernel,
        out_shape=(jax.ShapeDtypeStruct((t, n), BF16), jax.ShapeDtypeStruct((t, LANES), F32)),
        grid=(t // tm, nj),
        in_specs=[
            _x_ahead_spec(tm, d, t // tm, nj),
            pl.BlockSpec((1, d), lambda i, j: (0, 0)),
            mod.spec(0, tm), mod.spec(1, tm), mod.spec(0, tm, 1), mod.spec(1, tm, 1),
            pl.BlockSpec((None, d, tn), lambda i, j: (layer, 0, j)),
            _const_spec((d, LANES)),
        ],
        out_specs=(pl.BlockSpec((tm, tn), lambda i, j: (i, j)),
                   pl.BlockSpec((tm, LANES), lambda i, j: (i, 0))),
        scratch_shapes=[pltpu.VMEM((2, tm, d), BF16)],
        compiler_params=_cparams("arbitrary", "arbitrary"),
        name="fox_proj",
    )(x, gain, mod.mod, mod.mod, mod.mod, mod.mod, w, wf)


def _swa_proj_kernel(x_ref, g_ref, sh_ref, sc_ref, shn_ref, scn_ref, w_ref, pos_ref, inv_ref,
                     m1_ref, m2_ref, o_ref, h_ref, cos_ref, s1_ref, s2_ref, *, n_rope_tiles, shift):
    i, j = pl.program_id(0), pl.program_id(1)
    cur = i % 2

    def stage(slot, sh, sc):
        h_ref[slot] = _prenorm(x_ref[...], g_ref[...], sh[0], sc[0]).astype(BF16)
        ang = pos_ref[...].astype(F32) * inv_ref[...]
        sn = jnp.sin(ang)
        cos_ref[slot] = jnp.cos(ang)
        s1_ref[slot] = sn * m1_ref[...]
        s2_ref[slot] = sn * m2_ref[...]

    def rope_tile():
        acc = _dot(h_ref[cur], w_ref[...])
        cs, s1, s2 = cos_ref[cur], s1_ref[cur], s2_ref[cur]
        for c in range(acc.shape[1] // LANES):
            a = acc[:, c * LANES:(c + 1) * LANES]
            r = a * cs + pltpu.roll(a, LANES - shift, 1) * s1 + pltpu.roll(a, shift, 1) * s2
            o_ref[:, c * LANES:(c + 1) * LANES] = r.astype(o_ref.dtype)

    @pl.when(jnp.logical_and(i == 0, j == 0))
    def _():
        stage(0, sh_ref, sc_ref)

    @pl.when(j < n_rope_tiles)
    def _():
        rope_tile()

    @pl.when(jnp.logical_and(j >= n_rope_tiles, j < pl.num_programs(1) - 1))
    def _():
        o_ref[...] = _dot(h_ref[cur], w_ref[...]).astype(o_ref.dtype)

    @pl.when(j == pl.num_programs(1) - 1)
    def _():
        stage(1 - cur, shn_ref, scn_ref)
        o_ref[...] = _dot(h_ref[cur], w_ref[...]).astype(o_ref.dtype)


def _swa_proj(x, gain, mod, w, pos, inv_l, m1, m2, tm, tn, n_rope_cols, shift):
    t, d = x.shape
    n = w.shape[1]
    nt, nj = t // tm, n // tn
    assert n_rope_cols // tn < nj
    row = lambda shape: pl.BlockSpec(shape, lambda i, j: (0, 0))
    ahead = lambda i, j: (jnp.where(j == nj - 1, jnp.minimum(i + 1, nt - 1), i), 0)
    return pl.pallas_call(
        functools.partial(_swa_proj_kernel, n_rope_tiles=n_rope_cols // tn, shift=shift),
        out_shape=jax.ShapeDtypeStruct((t, n), BF16),
        grid=(nt, nj),
        in_specs=[
            _x_ahead_spec(tm, d, nt, nj),
            row((1, d)),
            mod.spec(0, tm), mod.spec(1, tm), mod.spec(0, tm, 1), mod.spec(1, tm, 1),
            pl.BlockSpec((d, tn), lambda i, j: (0, j)),
            pl.BlockSpec((tm, 1), ahead),
            row((1, LANES)), row((1, LANES)), row((1, LANES)),
        ],
        out_specs=pl.BlockSpec((tm, tn), lambda i, j: (i, j)),
        scratch_shapes=[pltpu.VMEM((2, tm, d), BF16)] + [pltpu.VMEM((2, tm, LANES), F32)] * 3,
        compiler_params=_cparams("arbitrary", "arbitrary"),
        name="swa_proj",
    )(x, gain, mod.mod, mod.mod, mod.mod, mod.mod, w, pos, inv_l, m1, m2)


def _out_kernel(a_ref, w_ref, x_ref, g_ref, gate_ref, o_ref):
    y = _dot(a_ref[...], w_ref[...])
    o_ref[...] = _postnorm_residual(x_ref[...], y, g_ref[...], gate_ref[0])


def _out_proj(a, w, layer, x, gain, mod, tm):
    t, k = a.shape
    d = w.shape[2]
    return pl.pallas_call(
        _out_kernel,
        out_shape=jax.ShapeDtypeStruct((t, d), F32),
        grid=(t // tm,),
        in_specs=[
            pl.BlockSpec((tm, k), lambda i: (i, 0)),
            pl.BlockSpec((None, k, d), lambda i: (layer, 0, 0), pipeline_mode=pl.Buffered(1)),
            pl.BlockSpec((tm, d), lambda i: (i, 0)),
            pl.BlockSpec((1, d), lambda i: (0, 0)),
            mod.spec(2, tm),
        ],
        out_specs=pl.BlockSpec((tm, d), lambda i: (i, 0)),
        compiler_params=_cparams("parallel"),
        name="out_proj",
    )(a, w, x, gain, mod.mod)


def _ffn_kernel(x_ref, gpre_ref, sh_ref, sc_ref, shn_ref, scn_ref, wg_ref, wu_ref, wd_ref,
                gpost_ref, gate_ref, o_ref, h_ref, xs_ref, acc_ref):
    i, j = pl.program_id(0), pl.program_id(1)
    last = pl.num_programs(1) - 1
    cur = i % 2

    def stage(slot, sh, sc):
        x = x_ref[...]
        xs_ref[slot] = x
        h_ref[slot] = _prenorm(x, gpre_ref[...], sh[0], sc[0]).astype(BF16)

    def chunk():
        h = h_ref[cur]
        g = _dot(h, wg_ref[...])
        u = _dot(h, wu_ref[...])
        a = (g * jax.nn.sigmoid(g) * u).astype(BF16)
        return _dot(a, wd_ref[...])

    @pl.when(jnp.logical_and(i == 0, j == 0))
    def _():
        stage(0, sh_ref, sc_ref)

    @pl.when(j == 0)
    def _():
        acc_ref[...] = chunk()

    @pl.when(jnp.logical_and(j > 0, j < last))
    def _():
        acc_ref[...] += chunk()

    @pl.when(j == last)
    def _():
        stage(1 - cur, shn_ref, scn_ref)
        y = acc_ref[...] + chunk()
        o_ref[...] = _postnorm_residual(xs_ref[cur], y, gpost_ref[...], gate_ref[0])


def _ffn(x, gpre, gpost, mod, w_gu, w_down, layer, tm, tf):
    t, d = x.shape
    dff = w_down.shape[1]
    nf = dff // tf
    assert nf >= 2
    row = lambda: pl.BlockSpec((1, d), lambda i, j: (0, 0))
    return pl.pallas_call(
        _ffn_kernel,
        out_shape=jax.ShapeDtypeStruct((t, d), F32),
        grid=(t // tm, nf),
        in_specs=[
            _x_ahead_spec(tm, d, t // tm, nf),
            row(), mod.spec(3, tm), mod.spec(4, tm), mod.spec(3, tm, 1), mod.spec(4, tm, 1),
            pl.BlockSpec((None, d, tf), lambda i, j: (layer, 0, j)),
            pl.BlockSpec((None, d, tf), lambda i, j: (layer, 0, nf + j)),
            pl.BlockSpec((None, tf, d), lambda i, j: (layer, j, 0)),
            row(), mod.spec(5, tm),
        ],
        out_specs=pl.BlockSpec((tm, d), lambda i, j: (i, 0)),
        scratch_shapes=[pltpu.VMEM((2, tm, d), BF16), pltpu.VMEM((2, tm, d), F32),
                        pltpu.VMEM((tm, d), F32)],
        compiler_params=_cparams("arbitrary", "arbitrary"),
        name="ffn",
    )(x, gpre, mod.mod, mod.mod, mod.mod, mod.mod, w_gu, w_gu, w_down, gpost, mod.mod)


def _gate_kernel(fg_ref, b_ref, cum_ref, carry_ref):
    @pl.when(pl.program_id(1) == 0)
    def _():
        carry_ref[...] = jnp.zeros_like(carry_ref)

    z = fg_ref[...] + b_ref[...]
    lf = jnp.minimum(z, 0.0) - jnp.log1p(jnp.exp(-jnp.abs(z)))
    n = lf.shape[0]
    tril = (lax.broadcasted_iota(jnp.int32, (n, n), 1)
            <= lax.broadcasted_iota(jnp.int32, (n, n), 0)).astype(BF16)
    hi = lf.astype(BF16)
    r1 = lf - hi.astype(F32)
    mid = r1.astype(BF16)
    lo = (r1 - mid.astype(F32)).astype(BF16)
    cum = (_dot(tril, hi) + _dot(tril, mid)) + _dot(tril, lo) + carry_ref[...]
    cum_ref[...] = cum
    carry_ref[...] = cum[n - 1:n, :]


def _gate_cumsum(fg, b_pad, batch, tg):
    t = fg.shape[0]
    s = t // batch
    nt = s // tg
    return pl.pallas_call(
        _gate_kernel,
        out_shape=jax.ShapeDtypeStruct((t, LANES), F32),
        grid=(batch, nt),
        in_specs=[
            pl.BlockSpec((tg, LANES), lambda b, j: (b * nt + j, 0)),
            pl.BlockSpec((1, LANES), lambda b, j: (0, 0)),
        ],
        out_specs=pl.BlockSpec((tg, LANES), lambda b, j: (b * nt + j, 0)),
        scratch_shapes=[pltpu.VMEM((1, LANES), F32)],
        compiler_params=_cparams("parallel", "arbitrary"),
        name="fox_gate",
    )(fg, b_pad)


def _lane_col(x, h):
    lane = lax.broadcasted_iota(jnp.int32, x.shape, 1)
    return jnp.sum(jnp.where(lane == h, x, 0.0), axis=-1, keepdims=True)


def _decay_lanes(x, h, base):
    hi = x.astype(BF16)
    r = x - hi.astype(F32)
    mid = r.astype(BF16)
    lo = (r - mid.astype(F32)).astype(BF16)
    src = lax.broadcasted_iota(jnp.int32, (3 * LANES, LANES), 0)
    dst = lax.broadcasted_iota(jnp.int32, (3 * LANES, LANES), 1)
    place = jnp.where(jnp.logical_and(src % LANES == h, dst == base + src // LANES), 1.0, 0.0)
    moved = _dot(jnp.concatenate([hi, mid, lo], axis=1), place.astype(BF16))
    lane = lax.broadcasted_iota(jnp.int32, (1, LANES), 1)
    ones = jnp.where(jnp.logical_and(lane >= 3 - base, lane < 6 - base), 1.0, 0.0)
    return (moved + ones).astype(BF16)


def _fox_attn_kernel(*refs, tk, qscale, n_side):
    q_ref, k_ref, v_ref, cum_ref = refs[:4]
    side_in = refs[4:4 + n_side]
    o_ref = refs[4 + n_side]
    side_out = refs[5 + n_side:5 + 2 * n_side]
    (ka_ref, vt_ref, qa_ref, kstat_ref, sa_ref, sb_ref, pa_ref, pb_ref, m_ref, ala_ref, alb_ref,
     acc_ref) = refs[5 + 2 * n_side:]
    for src, dst in zip(side_in, side_out):
        dst[...] = src[...].astype(BF16)

    h = pl.program_id(1)
    i = pl.program_id(2)
    tq, dh = q_ref.shape
    nkv = vt_ref.shape[0]
    lane_row = lax.broadcasted_iota(jnp.int32, (1, LANES), 1)

    @pl.when(i == 0)
    def _():
        kn2 = jnp.zeros((1, 1), F32)
        for c in range(nkv):
            rows = slice(c * tk, (c + 1) * tk)
            kc = k_ref[rows, :]
            kf = kc.astype(F32)
            kn2 = jnp.maximum(kn2, jnp.max(jnp.sum(kf * kf, axis=1, keepdims=True),
                                           axis=0, keepdims=True))
            ka_ref[rows, :dh] = kc
            ka_ref[rows, dh:] = _decay_lanes(cum_ref[rows, :] * (-LOG2E), h, 0)
            vt_ref[c, :dh, :] = v_ref[rows, :].T
            vt_ref[c, dh:, :] = jnp.ones((vt_ref.shape[1] - dh, tk), BF16)
        stat = jnp.where(lane_row == LANES - 1, kn2, 0.0)
        for k in range(nkv // 2):
            last = (k + 1) * tq - 1
            stat = jnp.where(lane_row == k, _lane_col(cum_ref[last:last + 1, :], h) * LOG2E, stat)
        kstat_ref[...] = stat

    qs = (q_ref[...].astype(F32) * qscale).astype(BF16)
    fq = cum_ref[pl.ds(pl.multiple_of(i * tq, tq), tq), :] * LOG2E
    qa_ref[:, :dh] = qs
    qa_ref[:, dh:] = _decay_lanes(fq, h, 3)
    qf = qs.astype(F32)
    qn2 = jnp.max(jnp.sum(qf * qf, axis=1, keepdims=True), axis=0, keepdims=True)
    fq_max = _lane_col(jnp.max(fq, axis=0, keepdims=True), h)
    acc_ref[...] = jnp.zeros(acc_ref.shape, F32)

    def qk(c, s_out, lo=0):
        ka = ka_ref[pl.ds(pl.multiple_of(c * tk, tk), tk), :]
        s_out[:, lo:] = _dot_nt(ka, qa_ref[lo:, :])

    def pv(c, p_in, al_in=None):
        acc = acc_ref[...] if al_in is None else al_in[...] * acc_ref[...]
        acc_ref[...] = acc + _dot(vt_ref[c], p_in[...])

    rc = SOFTMAX_ROWS
    nrc = tk // rc

    def chunk(s_in, r, key_offset, lo):
        blk = s_in[r * rc:(r + 1) * rc, lo:]
        if key_offset is not None:
            col = lax.broadcasted_iota(jnp.int32, blk.shape, 1) + lo
            row = lax.broadcasted_iota(jnp.int32, blk.shape, 0)
            blk = jnp.where(row + (r * rc + key_offset) <= col, blk, -jnp.inf)
        return blk

    def column_max(blocks, width):
        part = jnp.full((SUBLANES, width), -jnp.inf, F32)
        for blk in blocks:
            part = jnp.maximum(part, jnp.max(blk.reshape(rc // SUBLANES, SUBLANES, width), axis=0))
        return jnp.max(part, axis=0, keepdims=True)

    def softmax_fixed(s_in, p_out, ref, key_offset=None, lo=0):
        for r in range(nrc):
            p_out[r * rc:(r + 1) * rc, lo:] = jnp.exp2(chunk(s_in, r, key_offset, lo) - ref).astype(BF16)

    def softmax(s_in, p_out, al_out, key_offset=None, lo=0):
        m = m_ref[:, lo:]
        m_new = jnp.maximum(m, column_max((chunk(s_in, r, key_offset, lo) for r in range(nrc)),
                                          tq - lo))
        for r in range(nrc):
            p_out[r * rc:(r + 1) * rc, lo:] = jnp.exp2(chunk(s_in, r, key_offset, lo) - m_new).astype(BF16)
        m_ref[:, lo:] = m_new
        al_out[:, lo:] = jnp.exp2(m - m_new)

    def older_pairs(m_min):
        live = jnp.logical_and(lane_row < i, qk_bound + fq_max - stat >= m_min - SKIP_LOG2)
        first = jnp.min(jnp.where(live, lane_row, i).astype(F32), axis=1, keepdims=True)
        return i - first[0, 0].astype(jnp.int32)

    def finish():
        o_ref[...] = (acc_ref[:dh, :] / acc_ref[dh:dh + 1, :]).T.astype(o_ref.dtype)

    def diagonal_scores():
        qk(2 * i, sa_ref)
        qk(2 * i + 1, sb_ref, tk)
        pb_ref[:, :tk] = jnp.zeros((tk, tk), BF16)

    stat = kstat_ref[...]
    kn2 = jnp.sum(jnp.where(lane_row == LANES - 1, stat, 0.0), axis=1, keepdims=True)
    qk_bound = jnp.sqrt(qn2 * kn2)
    calm = (2.0 * qk_bound)[0, 0] < SAFE_LOG2

    @pl.when(calm)
    def _():
        diagonal_scores()
        softmax_fixed(sa_ref, pa_ref, qk_bound, 0)
        pv(2 * i, pa_ref)
        softmax_fixed(sb_ref, pb_ref, qk_bound, tk, tk)
        n_old = older_pairs(qk_bound)

        def body(j, carry):
            k = i - 1 - j
            qk(2 * k, sa_ref)
            qk(2 * k + 1, sb_ref)
            pv(2 * k + 3, pb_ref)
            softmax_fixed(sa_ref, pa_ref, qk_bound)
            pv(2 * k, pa_ref)
            softmax_fixed(sb_ref, pb_ref, qk_bound)
            return carry

        lax.fori_loop(0, n_old, body, 0)
        pv(2 * (i - n_old) + 1, pb_ref)
        finish()

    @pl.when(jnp.logical_not(calm))
    def _():
        diagonal_scores()
        m_ref[...] = jnp.full(m_ref.shape, -jnp.inf, F32)
        softmax(sa_ref, pa_ref, ala_ref, 0)
        pv(2 * i, pa_ref, ala_ref)
        alb_ref[:, :tk] = jnp.ones((1, tk), F32)
        softmax(sb_ref, pb_ref, alb_ref, tk, tk)
        n_old = older_pairs(jnp.min(m_ref[...], axis=1, keepdims=True))

        def body(j, carry):
            k = i - 1 - j
            qk(2 * k, sa_ref)
            qk(2 * k + 1, sb_ref)
            pv(2 * k + 3, pb_ref, alb_ref)
            softmax(sa_ref, pa_ref, ala_ref)
            pv(2 * k, pa_ref, ala_ref)
            softmax(sb_ref, pb_ref, alb_ref)
            return carry

        lax.fori_loop(0, n_old, body, 0)
        pv(2 * (i - n_old) + 1, pb_ref, alb_ref)
        finish()


def _side_cast_ok(side, nsteps):
    return all(a.shape[0] % (nsteps * 2 * SUBLANES) == 0 and a.shape[1] % LANES == 0 for a in side)


def _fox_attn(qkv, cum, batch, heads, tq, side=()):
    t = qkv.shape[0]
    dh = qkv.shape[1] // (3 * heads)
    s = t // batch
    nq = s // tq
    tk = tq // 2
    nsteps = batch * heads * nq
    assert _side_cast_ok(side, nsteps)
    slab = lambda a: pl.BlockSpec((a.shape[0] // nsteps, a.shape[1]),
                                  lambda b, h, i: ((b * heads + h) * nq + i, 0))
    outs = pl.pallas_call(
        functools.partial(_fox_attn_kernel, tk=tk, qscale=dh ** -0.5 * LOG2E, n_side=len(side)),
        out_shape=[jax.ShapeDtypeStruct((t, heads * dh), BF16)]
        + [jax.ShapeDtypeStruct(a.shape, BF16) for a in side],
        grid=(batch, heads, nq),
        in_specs=[
            pl.BlockSpec((tq, dh), lambda b, h, i: (b * nq + i, h)),
            pl.BlockSpec((s, dh), lambda b, h, i: (b, heads + h)),
            pl.BlockSpec((s, dh), lambda b, h, i: (b, 2 * heads + h)),
            pl.BlockSpec((s, LANES), lambda b, h, i: (b, 0)),
        ] + [slab(a) for a in side],
        out_specs=[pl.BlockSpec((tq, dh), lambda b, h, i: (b * nq + i, h))] + [slab(a) for a in side],
        scratch_shapes=[
            pltpu.VMEM((s, 2 * dh), BF16),
            pltpu.VMEM((s // tk, dh + ONES_ROWS, tk), BF16),
            pltpu.VMEM((tq, 2 * dh), BF16),
            pltpu.VMEM((1, LANES), F32),
            pltpu.VMEM((tk, tq), F32), pltpu.VMEM((tk, tq), F32),
            pltpu.VMEM((tk, tq), BF16), pltpu.VMEM((tk, tq), BF16),
            pltpu.VMEM((1, tq), F32),
            pltpu.VMEM((1, tq), F32), pltpu.VMEM((1, tq), F32),
            pltpu.VMEM((dh + ONES_ROWS, tq), F32),
        ],
        compiler_params=_cparams("parallel", "parallel", "arbitrary"),
        name="fox_attn",
    )(qkv, qkv, qkv, cum, *side)
    return outs[0], outs[1:]


def _sgu_kernel(x_ref, gpre_ref, sh_ref, sc_ref, win_ref, lng_ref, lnb_ref, ws_ref, bs_ref,
                wout_ref, gpost_ref, gate_ref, o_ref, u_ref, vn_ref, gated_ref):
    x = x_ref[...]
    tm = x.shape[0]
    width = u_ref.shape[1]
    groups, chunk, _ = ws_ref.shape
    gd = width // groups
    h = _prenorm(x, gpre_ref[...], sh_ref[0], sc_ref[0]).astype(BF16)
    zv = _dot(h, win_ref[:, width:])
    zu = _dot(h, win_ref[:, :width])
    v = jax.nn.gelu(zv)
    mu = jnp.mean(v, axis=-1, keepdims=True)
    vc = v - mu
    var = jnp.mean(vc * vc, axis=-1, keepdims=True)
    vn_ref[...] = (vc * lax.rsqrt(var + EPS) * lng_ref[...] + lnb_ref[...]).astype(BF16)
    u_ref[...] = jax.nn.gelu(zu)
    causal = (lax.broadcasted_iota(jnp.int32, (chunk, chunk), 1)
              <= lax.broadcasted_iota(jnp.int32, (chunk, chunk), 0))
    for g in range(groups):
        wg = jnp.where(causal, ws_ref[g], jnp.zeros((), BF16))
        bias = bs_ref[:, g:g + 1]
        cols = slice(g * gd, (g + 1) * gd)
        for c in range(tm // chunk):
            rows = slice(c * chunk, (c + 1) * chunk)
            f = _dot(wg, vn_ref[rows, cols]) + bias
            gated_ref[rows, cols] = (u_ref[rows, cols] * f).astype(BF16)
    y = _dot(gated_ref[...], wout_ref[...])
    o_ref[...] = _postnorm_residual(x, y, gpost_ref[...], gate_ref[0])


def _sgu(x, gpre, gpost, mod, w_in, ln_g, ln_b, w_s, b_st, w_out, tm):
    t, d = x.shape
    width = w_out.shape[0]
    row = lambda n: pl.BlockSpec((1, n), lambda i: (0, 0))
    return pl.pallas_call(
        _sgu_kernel,
        out_shape=jax.ShapeDtypeStruct((t, d), F32),
        grid=(t // tm,),
        in_specs=[
            pl.BlockSpec((tm, d), lambda i: (i, 0)),
            row(d), mod.spec(0, tm), mod.spec(1, tm),
            _const_spec(w_in.shape),
            row(width), row(width),
            _const_spec(w_s.shape),
            _const_spec(b_st.shape),
            _const_spec(w_out.shape),
            row(d), mod.spec(2, tm),
        ],
        out_specs=pl.BlockSpec((tm, d), lambda i: (i, 0)),
        scratch_shapes=[pltpu.VMEM((tm, width), F32), pltpu.VMEM((tm, width), BF16),
                        pltpu.VMEM((tm, width), BF16)],
        compiler_params=_cparams("parallel"),
        name="sgu",
    )(x, gpre, mod.mod, mod.mod, w_in, ln_g, ln_b, w_s, b_st, w_out, gpost, mod.mod)


def _swa_attn_kernel(sink_ref, q_ref, kc_ref, kp_ref, vc_ref, vp_ref, o_ref, *, nb, grp, scale):
    n = pl.program_id(0) % nb
    blk = q_ref.shape[0]
    half = LANES // 2
    kv_cols = kc_ref.shape[1] // LANES
    npair = grp // 2
    row = lax.broadcasted_iota(jnp.int32, (blk, blk), 0)
    col = lax.broadcasted_iota(jnp.int32, (blk, blk), 1)
    keep_p = jnp.logical_and(row > col, n > 0)
    keep_c = row <= col
    keep = jnp.concatenate([keep_p, keep_c], axis=0)
    keep = jnp.concatenate([keep] * npair, axis=1)
    lo = lax.broadcasted_iota(jnp.int32, (2 * blk, LANES), 1) < half
    scores, vts = {}, {}
    for j in range(kv_cols):
        csl = slice(j * LANES, (j + 1) * LANES)
        kf = jnp.concatenate([kp_ref[:, csl], kc_ref[:, csl]], axis=0).astype(F32) * scale
        kr = pltpu.roll(kf, half, 1)
        vt = jnp.concatenate([vp_ref[:, csl], vc_ref[:, csl]], axis=0).T
        for e in range(2):
            kh = 2 * j + e
            qcols = [q_ref[:, (kh * npair + a) * LANES:(kh * npair + a + 1) * LANES]
                     for a in range(npair)]
            rhs = jnp.concatenate(qcols, axis=0)
            vts[kh] = vt[e * half:(e + 1) * half, :]
            for p in range(2):
                src = kf if p == e else kr
                kz = jnp.where(lo if p == 0 else jnp.logical_not(lo), src, 0.0).astype(BF16)
                scores[kh, p] = _dot_nt(kz, rhs)
    probs = {}
    for (kh, p), st in scores.items():
        st = jnp.where(keep, st, -jnp.inf)
        sink = jnp.concatenate(
            [jnp.full((1, blk), sink_ref[kh * grp + 2 * a + p], F32) for a in range(npair)], axis=1)
        m = jnp.maximum(jnp.max(st, axis=0, keepdims=True), sink)
        pt = jnp.exp(st - m)
        den = jnp.sum(pt, axis=0, keepdims=True) + jnp.exp(sink - m)
        probs[kh, p] = (pt.astype(BF16), den)
    for kh in range(2 * kv_cols):
        outs = [_dot(vts[kh], probs[kh, p][0]) / probs[kh, p][1] for p in range(2)]
        for a in range(npair):
            ot = jnp.concatenate([o[:, a * blk:(a + 1) * blk] for o in outs], axis=0)
            c0 = (kh * npair + a) * LANES
            o_ref[:, c0:c0 + LANES] = ot.T.astype(o_ref.dtype)


def _swa_attn(qkv, sinks, batch, hq, hk, dh):
    t = qkv.shape[0]
    nb = t // batch // BLOCK
    grp = hq // hk
    assert 2 * dh == LANES and grp % 2 == 0 and hk % 2 == 0
    prev = lambda r: jnp.maximum(r - 1, 0)
    return pl.pallas_call(
        functools.partial(_swa_attn_kernel, nb=nb, grp=grp, scale=dh ** -0.5),
        out_shape=jax.ShapeDtypeStruct((t, hq * dh), BF16),
        grid=(t // BLOCK,),
        in_specs=[
            pl.BlockSpec(memory_space=pltpu.SMEM),
            pl.BlockSpec((BLOCK, hq * dh), lambda r: (r, 0)),
            pl.BlockSpec((BLOCK, hk * dh), lambda r: (r, grp)),
            pl.BlockSpec((BLOCK, hk * dh), lambda r: (prev(r), grp)),
            pl.BlockSpec((BLOCK, hk * dh), lambda r: (r, grp + 1)),
            pl.BlockSpec((BLOCK, hk * dh), lambda r: (prev(r), grp + 1)),
        ],
        out_specs=pl.BlockSpec((BLOCK, hq * dh), lambda r: (r, 0)),
        compiler_params=_cparams("parallel"),
        name="swa_attn",
    )(sinks, qkv, qkv, qkv, qkv, qkv)


def _pad_lanes(a, n=LANES):
    return jnp.pad(a, [(0, 0)] * (a.ndim - 1) + [(0, n - a.shape[-1])])


def kernel(x, c, positions, ada_w, ada_b, mix_pre_g, mix_post_g, ffn_pre_g, ffn_post_g, ffn_w_gu, ffn_w_down, fox_w_in, fox_b_f, fox_w_out, sgu_w_in, sgu_ln_g, sgu_ln_b, sgu_w_s, sgu_b_s, sgu_w_out, swa_w_in, swa_sinks, swa_w_out):
    batch, seq, d = x.shape
    depth = ada_w.shape[0]
    t = batch * seq
    assert batch <= SUBLANES and seq % BLOCK == 0 and d % LANES == 0

    tm = _pick(seq, 512)
    tm_proj = _pick(seq, 1024)
    xf = x.reshape(t, d)

    c_pad = jnp.pad(c, ((0, SUBLANES - batch), (0, 0)))
    mod_all = _adaln(c_pad, ada_w, ada_b).reshape(depth * SUBLANES * N_MOD, 1, d)

    dff = ffn_w_down.shape[1]
    tf = _pick(dff, 512)
    fox_in, fox_out = fox_w_in.astype(BF16), fox_w_out.astype(BF16)
    swa_out = swa_w_out.astype(BF16)
    heads0, tq0 = fox_b_f.shape[1], _pick(seq, 1024)
    ffn_side = (ffn_w_gu.reshape(-1, 2 * dff), ffn_w_down.reshape(-1, 2 * dff))
    cast_in_attn = depth > 0 and _side_cast_ok(ffn_side, batch * heads0 * (seq // tq0))
    if not cast_in_attn:
        ffn_gu, ffn_down = ffn_w_gu.astype(BF16), ffn_w_down.astype(BF16)

    for i in range(depth):
        kind, j = i % N_MIXERS, i // N_MIXERS
        mod = _Mod(mod_all, i, batch, seq)
        gpre, gpost = mix_pre_g[i].reshape(1, d), mix_post_g[i].reshape(1, d)
        if kind == 0:
            heads = fox_b_f.shape[1]
            nqkv = fox_w_in.shape[2] - heads
            qkv, fg = _fox_proj(xf, gpre, mod, fox_in, _pad_lanes(fox_in[j, :, nqkv:]), j, nqkv,
                                tm_proj, _pick(nqkv, min(1536, nqkv // 2)))
            cum = _gate_cumsum(fg, _pad_lanes(fox_b_f[j].reshape(1, heads)), batch, _pick(seq, 512))
            if i == 0 and cast_in_attn:
                a, (gu2, down2) = _fox_attn(qkv, cum, batch, heads, tq0, ffn_side)
                ffn_gu, ffn_down = gu2.reshape(ffn_w_gu.shape), down2.reshape(ffn_w_down.shape)
            else:
                a, _ = _fox_attn(qkv, cum, batch, heads, tq0)
            xf = _out_proj(a, fox_out, j, xf, gpost, mod, tm)
        elif kind == 1:
            groups = sgu_w_s.shape[1]
            xf = _sgu(xf, gpre, gpost, mod, sgu_w_in[j].astype(BF16),
                      sgu_ln_g[j].reshape(1, -1), sgu_ln_b[j].reshape(1, -1),
                      sgu_w_s[j].astype(BF16), _pad_lanes(sgu_b_s[j].T),
                      sgu_w_out[j].astype(BF16), _pick(seq, 256))
        else:
            dh = SWA_HEAD_DIM
            hq = swa_sinks.shape[1]
            hk = (swa_w_in.shape[2] // dh - hq) // 2
            rope = dh // 4
            half = rope // 2
            inv = ROPE_THETA ** (-jnp.arange(0, rope, 2, dtype=F32) / rope)
            lane_d = jnp.arange(LANES) % dh
            inv_l = jnp.where(lane_d < rope, inv[lane_d % half], 0.0).reshape(1, LANES).astype(F32)
            m1 = jnp.where(lane_d < half, -1.0, 0.0).reshape(1, LANES).astype(F32)
            m2 = jnp.where((lane_d >= half) & (lane_d < rope), 1.0, 0.0).reshape(1, LANES).astype(F32)
            qkv = _swa_proj(xf, gpre, mod, swa_w_in[j].astype(BF16), positions.reshape(t, 1),
                            inv_l, m1, m2, tm_proj, _pick(hk * dh, 512), (hq + hk) * dh, half)
            a = _swa_attn(qkv, swa_sinks[j], batch, hq, hk, dh)
            xf = _out_proj(a, swa_out, j, xf, gpost, mod, tm)
        xf = _ffn(xf, ffn_pre_g[i].reshape(1, d), ffn_post_g[i].reshape(1, d), mod,
                  ffn_gu, ffn_down, i, tm, tf)
    return xf.reshape(batch, seq, d)
```

```python
import functools

import jax
import jax.numpy as jnp
from jax import lax
from jax.experimental import pallas as pl
from jax.experimental.pallas import tpu as pltpu

F32 = jnp.float32
BF16 = jnp.bfloat16

EPS = 1e-6
N_MIXERS = 3
BLOCK = 128
SWA_HEAD_DIM = 64
ROPE_THETA = 500000.0
LANES = 128
SUBLANES = 8
N_MOD = 6
LOG2E = 1.4426950408889634
ONES_ROWS = 16
SOFTMAX_ROWS = 32
SAFE_LOG2 = 100.0
SKIP_LOG2 = 160.0
VMEM_LIMIT = 56 * 1024 * 1024


def _cparams(*sem):
    return pltpu.CompilerParams(dimension_semantics=sem, vmem_limit_bytes=VMEM_LIMIT)


def _const_spec(shape):
    nd = len(shape)
    return pl.BlockSpec(shape, lambda *_: (0,) * nd, pipeline_mode=pl.Buffered(1))


def _pick(n, pref):
    t = min(n, pref)
    while n % t:
        t -= LANES
    return t


def _rms(x):
    return x * lax.rsqrt(jnp.mean(x * x, axis=-1, keepdims=True) + EPS)


def _prenorm(x, g, sh, sc):
    return _rms(x) * (g * (1.0 + sc)) + sh


def _postnorm_residual(x, y, g, gate):
    return x + _rms(y) * (gate * g)


def _dot(a, b):
    return jnp.dot(a, b, preferred_element_type=F32)


def _dot_nt(a, b):
    return lax.dot_general(a, b, (((1,), (1,)), ((), ())), preferred_element_type=F32)


def _adaln_kernel(c_ref, w_ref, b_ref, o_ref):
    c = c_ref[...]
    ca = (c * jax.nn.sigmoid(c)).astype(BF16)
    o_ref[0] = _dot(ca, w_ref[0].astype(BF16)) + b_ref[0]


def _adaln(c_pad, ada_w, ada_b):
    depth, d, n = ada_w.shape
    tn = _pick(n, 1024)
    return pl.pallas_call(
        _adaln_kernel,
        out_shape=jax.ShapeDtypeStruct((depth, SUBLANES, n), F32),
        grid=(depth, n // tn),
        in_specs=[
            pl.BlockSpec((SUBLANES, d), lambda i, j: (0, 0)),
            pl.BlockSpec((1, d, tn), lambda i, j: (i, 0, j)),
            pl.BlockSpec((1, 1, tn), lambda i, j: (i, 0, j)),
        ],
        out_specs=pl.BlockSpec((1, SUBLANES, tn), lambda i, j: (i, 0, j)),
        compiler_params=_cparams("parallel", "parallel"),
        name="adaln",
    )(c_pad, ada_w, ada_b.reshape(depth, 1, n))


class _Mod:
    def __init__(self, mod, layer, batch, seq):
        self.mod = mod
        self.layer = layer
        self.batch = batch
        self.seq = seq
        self.d = mod.shape[-1]

    def spec(self, k, tm, ahead=0):
        layer, tpb = self.layer, self.seq // tm
        last = self.batch * tpb - 1
        return pl.BlockSpec(
            (1, 1, self.d),
            lambda i, *_: ((layer * SUBLANES + jnp.clip(i + ahead, 0, last) // tpb) * N_MOD + k, 0, 0))


def _x_ahead_spec(tm, d, ntiles, nj):
    first_ahead = nj - 2 if nj >= 3 else nj - 1
    return pl.BlockSpec(
        (tm, d), lambda i, j: (jnp.minimum(jnp.where(j >= first_ahead, i + 1, i), ntiles - 1), 0))


def _fox_proj_kernel(x_ref, g_ref, sh_ref, sc_ref, shn_ref, scn_ref, w_ref, wf_ref, o_ref, fg_ref,
                     h_ref):
    i, j = pl.program_id(0), pl.program_id(1)
    cur = i % 2

    @pl.when(jnp.logical_and(i == 0, j == 0))
    def _():
        h_ref[0] = _prenorm(x_ref[...], g_ref[...], sh_ref[0], sc_ref[0]).astype(BF16)

    @pl.when(j == 0)
    def _():
        h = h_ref[cur]
        fg_ref[...] = _dot(h, wf_ref[...])
        o_ref[...] = _dot(h, w_ref[...]).astype(o_ref.dtype)

    @pl.when(jnp.logical_and(j > 0, j < pl.num_programs(1) - 1))
    def _():
        o_ref[...] = _dot(h_ref[cur], w_ref[...]).astype(o_ref.dtype)

    @pl.when(j == pl.num_programs(1) - 1)
    def _():
        h_ref[1 - cur] = _prenorm(x_ref[...], g_ref[...], shn_ref[0], scn_ref[0]).astype(BF16)
        o_ref[...] = _dot(h_ref[cur], w_ref[...]).astype(o_ref.dtype)


def _fox_proj(x, gain, mod, w, wf, layer, n, tm, tn):
    t, d = x.shape
    nj = n // tn
    assert nj >= 2
    return pl.pallas_call(
        _fox_proj_kernel,
        out_shape=(jax.ShapeDtypeStruct((t, n), BF16), jax.ShapeDtypeStruct((t, LANES), F32)),
        grid=(t // tm, nj),
        in_specs=[
            _x_ahead_spec(tm, d, t // tm, nj),
            pl.BlockSpec((1, d), lambda i, j: (0, 0)),
            mod.spec(0, tm), mod.spec(1, tm), mod.spec(0, tm, 1), mod.spec(1, tm, 1),
            pl.BlockSpec((None, d, tn), lambda i, j: (layer, 0, j)),
            _const_spec((d, LANES)),
        ],
        out_specs=(pl.BlockSpec((tm, tn), lambda i, j: (i, j)),
                   pl.BlockSpec((tm, LANES), lambda i, j: (i, 0))),
        scratch_shapes=[pltpu.VMEM((2, tm, d), BF16)],
        compiler_params=_cparams("arbitrary", "arbitrary"),
        name="fox_proj",
    )(x, gain, mod.mod, mod.mod, mod.mod, mod.mod, w, wf)


def _swa_proj_kernel(x_ref, g_ref, sh_ref, sc_ref, shn_ref, scn_ref, w_ref, pos_ref, inv_ref,
                     m1_ref, m2_ref, o_ref, h_ref, cos_ref, s1_ref, s2_ref, *, n_rope_tiles, shift):
    i, j = pl.program_id(0), pl.program_id(1)
    cur = i % 2

    def stage(slot, sh, sc):
        h_ref[slot] = _prenorm(x_ref[...], g_ref[...], sh[0], sc[0]).astype(BF16)
        ang = pos_ref[...].astype(F32) * inv_ref[...]
        sn = jnp.sin(ang)
        cos_ref[slot] = jnp.cos(ang)
        s1_ref[slot] = sn * m1_ref[...]
        s2_ref[slot] = sn * m2_ref[...]

    def rope_tile():
        acc = _dot(h_ref[cur], w_ref[...])
        cs, s1, s2 = cos_ref[cur], s1_ref[cur], s2_ref[cur]
        for c in range(acc.shape[1] // LANES):
            a = acc[:, c * LANES:(c + 1) * LANES]
            r = a * cs + pltpu.roll(a, LANES - shift, 1) * s1 + pltpu.roll(a, shift, 1) * s2
            o_ref[:, c * LANES:(c + 1) * LANES] = r.astype(o_ref.dtype)

    @pl.when(jnp.logical_and(i == 0, j == 0))
    def _():
        stage(0, sh_ref, sc_ref)

    @pl.when(j < n_rope_tiles)
    def _():
        rope_tile()

    @pl.when(jnp.logical_and(j >= n_rope_tiles, j < pl.num_programs(1) - 1))
    def _():
        o_ref[...] = _dot(h_ref[cur], w_ref[...]).astype(o_ref.dtype)

    @pl.when(j == pl.num_programs(1) - 1)
    def _():
        stage(1 - cur, shn_ref, scn_ref)
        o_ref[...] = _dot(h_ref[cur], w_ref[...]).astype(o_ref.dtype)


def _swa_proj(x, gain, mod, w, pos, inv_l, m1, m2, tm, tn, n_rope_cols, shift):
    t, d = x.shape
    n = w.shape[1]
    nt, nj = t // tm, n // tn
    assert n_rope_cols // tn < nj
    row = lambda shape: pl.BlockSpec(shape, lambda i, j: (0, 0))
    ahead = lambda i, j: (jnp.where(j == nj - 1, jnp.minimum(i + 1, nt - 1), i), 0)
    return pl.pallas_call(
        functools.partial(_swa_proj_kernel, n_rope_tiles=n_rope_cols // tn, shift=shift),
        out_shape=jax.ShapeDtypeStruct((t, n), BF16),
        grid=(nt, nj),
        in_specs=[
            _x_ahead_spec(tm, d, nt, nj),
            row((1, d)),
            mod.spec(0, tm), mod.spec(1, tm), mod.spec(0, tm, 1), mod.spec(1, tm, 1),
            pl.BlockSpec((d, tn), lambda i, j: (0, j)),
            pl.BlockSpec((tm, 1), ahead),
            row((1, LANES)), row((1, LANES)), row((1, LANES)),
        ],
        out_specs=pl.BlockSpec((tm, tn), lambda i, j: (i, j)),
        scratch_shapes=[pltpu.VMEM((2, tm, d), BF16)] + [pltpu.VMEM((2, tm, LANES), F32)] * 3,
        compiler_params=_cparams("arbitrary", "arbitrary"),
        name="swa_proj",
    )(x, gain, mod.mod, mod.mod, mod.mod, mod.mod, w, pos, inv_l, m1, m2)


def _out_kernel(a_ref, w_ref, x_ref, g_ref, gate_ref, o_ref):
    y = _dot(a_ref[...], w_ref[...])
    o_ref[...] = _postnorm_residual(x_ref[...], y, g_ref[...], gate_ref[0])


def _out_proj(a, w, layer, x, gain, mod, tm):
    t, k = a.shape
    d = w.shape[2]
    return pl.pallas_call(
        _out_kernel,
        out_shape=jax.ShapeDtypeStruct((t, d), F32),
        grid=(t // tm,),
        in_specs=[
            pl.BlockSpec((tm, k), lambda i: (i, 0)),
            pl.BlockSpec((None, k, d), lambda i: (layer, 0, 0), pipeline_mode=pl.Buffered(1)),
            pl.BlockSpec((tm, d), lambda i: (i, 0)),
            pl.BlockSpec((1, d), lambda i: (0, 0)),
            mod.spec(2, tm),
        ],
        out_specs=pl.BlockSpec((tm, d), lambda i: (i, 0)),
        compiler_params=_cparams("parallel"),
        name="out_proj",
    )(a, w, x, gain, mod.mod)


def _ffn_kernel(x_ref, gpre_ref, sh_ref, sc_ref, shn_ref, scn_ref, wg_ref, wu_ref, wd_ref,
                gpost_ref, gate_ref, o_ref, h_ref, xs_ref, acc_ref):
    i, j = pl.program_id(0), pl.program_id(1)
    last = pl.num_programs(1) - 1
    cur = i % 2

    def stage(slot, sh, sc):
        x = x_ref[...]
        xs_ref[slot] = x
        h_ref[slot] = _prenorm(x, gpre_ref[...], sh[0], sc[0]).astype(BF16)

    def chunk():
        h = h_ref[cur]
        g = _dot(h, wg_ref[...])
        u = _dot(h, wu_ref[...])
        a = (g * jax.nn.sigmoid(g) * u).astype(BF16)
        return _dot(a, wd_ref[...])

    @pl.when(jnp.logical_and(i == 0, j == 0))
    def _():
        stage(0, sh_ref, sc_ref)

    @pl.when(j == 0)
    def _():
        acc_ref[...] = chunk()

    @pl.when(jnp.logical_and(j > 0, j < last))
    def _():
        acc_ref[...] += chunk()

    @pl.when(j == last)
    def _():
        stage(1 - cur, shn_ref, scn_ref)
        y = acc_ref[...] + chunk()
        o_ref[...] = _postnorm_residual(xs_ref[cur], y, gpost_ref[...], gate_ref[0])


def _ffn(x, gpre, gpost, mod, w_gu, w_down, layer, tm, tf):
    t, d = x.shape
    dff = w_down.shape[1]
    nf = dff // tf
    assert nf >= 2
    row = lambda: pl.BlockSpec((1, d), lambda i, j: (0, 0))
    return pl.pallas_call(
        _ffn_kernel,
        out_shape=jax.ShapeDtypeStruct((t, d), F32),
        grid=(t // tm, nf),
        in_specs=[
            _x_ahead_spec(tm, d, t // tm, nf),
            row(), mod.spec(3, tm), mod.spec(4, tm), mod.spec(3, tm, 1), mod.spec(4, tm, 1),
            pl.BlockSpec((None, d, tf), lambda i, j: (layer, 0, j)),
            pl.BlockSpec((None, d, tf), lambda i, j: (layer, 0, nf + j)),
            pl.BlockSpec((None, tf, d), lambda i, j: (layer, j, 0)),
            row(), mod.spec(5, tm),
        ],
        out_specs=pl.BlockSpec((tm, d), lambda i, j: (i, 0)),
        scratch_shapes=[pltpu.VMEM((2, tm, d), BF16), pltpu.VMEM((2, tm, d), F32),
                        pltpu.VMEM((tm, d), F32)],
        compiler_params=_cparams("arbitrary", "arbitrary"),
        name="ffn",
    )(x, gpre, mod.mod, mod.mod, mod.mod, mod.mod, w_gu, w_gu, w_down, gpost, mod.mod)


def _gate_kernel(fg_ref, b_ref, cum_ref, carry_ref):
    @pl.when(pl.program_id(1) == 0)
    def _():
        carry_ref[...] = jnp.zeros_like(carry_ref)

    z = fg_ref[...] + b_ref[...]
    lf = jnp.minimum(z, 0.0) - jnp.log1p(jnp.exp(-jnp.abs(z)))
    n = lf.shape[0]
    tril = (lax.broadcasted_iota(jnp.int32, (n, n), 1)
            <= lax.broadcasted_iota(jnp.int32, (n, n), 0)).astype(BF16)
    hi = lf.astype(BF16)
    r1 = lf - hi.astype(F32)
    mid = r1.astype(BF16)
    lo = (r1 - mid.astype(F32)).astype(BF16)
    cum = (_dot(tril, hi) + _dot(tril, mid)) + _dot(tril, lo) + carry_ref[...]
    cum_ref[...] = cum
    carry_ref[...] = cum[n - 1:n, :]


def _gate_cumsum(fg, b_pad, batch, tg):
    t = fg.shape[0]
    s = t // batch
    nt = s // tg
    return pl.pallas_call(
        _gate_kernel,
        out_shape=jax.ShapeDtypeStruct((t, LANES), F32),
        grid=(batch, nt),
        in_specs=[
            pl.BlockSpec((tg, LANES), lambda b, j: (b * nt + j, 0)),
            pl.BlockSpec((1, LANES), lambda b, j: (0, 0)),
        ],
        out_specs=pl.BlockSpec((tg, LANES), lambda b, j: (b * nt + j, 0)),
        scratch_shapes=[pltpu.VMEM((1, LANES), F32)],
        compiler_params=_cparams("parallel", "arbitrary"),
        name="fox_gate",
    )(fg, b_pad)


def _lane_col(x, h):
    lane = lax.broadcasted_iota(jnp.int32, x.shape, 1)
    return jnp.sum(jnp.where(lane == h, x, 0.0), axis=-1, keepdims=True)


def _decay_lanes(x, h, base):
    hi = x.astype(BF16)
    r = x - hi.astype(F32)
    mid = r.astype(BF16)
    lo = (r - mid.astype(F32)).astype(BF16)
    src = lax.broadcasted_iota(jnp.int32, (3 * LANES, LANES), 0)
    dst = lax.broadcasted_iota(jnp.int32, (3 * LANES, LANES), 1)
    place = jnp.where(jnp.logical_and(src % LANES == h, dst == base + src // LANES), 1.0, 0.0)
    moved = _dot(jnp.concatenate([hi, mid, lo], axis=1), place.astype(BF16))
    lane = lax.broadcasted_iota(jnp.int32, (1, LANES), 1)
    ones = jnp.where(jnp.logical_and(lane >= 3 - base, lane < 6 - base), 1.0, 0.0)
    return (moved + ones).astype(BF16)


def _fox_attn_kernel(*refs, tk, qscale, n_side):
    q_ref, k_ref, v_ref, cum_ref = refs[:4]
    side_in = refs[4:4 + n_side]
    o_ref = refs[4 + n_side]
    side_out = refs[5 + n_side:5 + 2 * n_side]
    (ka_ref, vt_ref, qa_ref, kstat_ref, sa_ref, sb_ref, pa_ref, pb_ref, m_ref, ala_ref, alb_ref,
     acc_ref) = refs[5 + 2 * n_side:]
    for src, dst in zip(side_in, side_out):
        dst[...] = src[...].astype(BF16)

    h = pl.program_id(1)
    i = pl.program_id(2)
    tq, dh = q_ref.shape
    nkv = vt_ref.shape[0]
    lane_row = lax.broadcasted_iota(jnp.int32, (1, LANES), 1)

    @pl.when(i == 0)
    def _():
        kn2 = jnp.zeros((1, 1), F32)
        for c in range(nkv):
            rows = slice(c * tk, (c + 1) * tk)
            kc = k_ref[rows, :]
            kf = kc.astype(F32)
            kn2 = jnp.maximum(kn2, jnp.max(jnp.sum(kf * kf, axis=1, keepdims=True),
                                           axis=0, keepdims=True))
            ka_ref[rows, :dh] = kc
            ka_ref[rows, dh:] = _decay_lanes(cum_ref[rows, :] * (-LOG2E), h, 0)
            vt_ref[c, :dh, :] = v_ref[rows, :].T
            vt_ref[c, dh:, :] = jnp.ones((vt_ref.shape[1] - dh, tk), BF16)
        stat = jnp.where(lane_row == LANES - 1, kn2, 0.0)
        for k in range(nkv // 2):
            last = (k + 1) * tq - 1
            stat = jnp.where(lane_row == k, _lane_col(cum_ref[last:last + 1, :], h) * LOG2E, stat)
        kstat_ref[...] = stat

    qs = (q_ref[...].astype(F32) * qscale).astype(BF16)
    fq = cum_ref[pl.ds(pl.multiple_of(i * tq, tq), tq), :] * LOG2E
    qa_ref[:, :dh] = qs
    qa_ref[:, dh:] = _decay_lanes(fq, h, 3)
    qf = qs.astype(F32)
    qn2 = jnp.max(jnp.sum(qf * qf, axis=1, keepdims=True), axis=0, keepdims=True)
    fq_max = _lane_col(jnp.max(fq, axis=0, keepdims=True), h)
    acc_ref[...] = jnp.zeros(acc_ref.shape, F32)

    def qk(c, s_out, lo=0):
        ka = ka_ref[pl.ds(pl.multiple_of(c * tk, tk), tk), :]
        s_out[:, lo:] = _dot_nt(ka, qa_ref[lo:, :])

    def pv(c, p_in, al_in=None):
        acc = acc_ref[...] if al_in is None else al_in[...] * acc_ref[...]
        acc_ref[...] = acc + _dot(vt_ref[c], p_in[...])

    rc = SOFTMAX_ROWS
    nrc = tk // rc

    def chunk(s_in, r, key_offset, lo):
        blk = s_in[r * rc:(r + 1) * rc, lo:]
        if key_offset is not None:
            col = lax.broadcasted_iota(jnp.int32, blk.shape, 1) + lo
            row = lax.broadcasted_iota(jnp.int32, blk.shape, 0)
            blk = jnp.where(row + (r * rc + key_offset) <= col, blk, -jnp.inf)
        return blk

    def column_max(blocks, width):
        part = jnp.full((SUBLANES, width), -jnp.inf, F32)
        for blk in blocks:
            part = jnp.maximum(part, jnp.max(blk.reshape(rc // SUBLANES, SUBLANES, width), axis=0))
        return jnp.max(part, axis=0, keepdims=True)

    def softmax_fixed(s_in, p_out, ref, key_offset=None, lo=0):
        for r in range(nrc):
            p_out[r * rc:(r + 1) * rc, lo:] = jnp.exp2(chunk(s_in, r, key_offset, lo) - ref).astype(BF16)

    def softmax(s_in, p_out, al_out, key_offset=None, lo=0):
        m = m_ref[:, lo:]
        m_new = jnp.maximum(m, column_max((chunk(s_in, r, key_offset, lo) for r in range(nrc)),
                                          tq - lo))
        for r in range(nrc):
            p_out[r * rc:(r + 1) * rc, lo:] = jnp.exp2(chunk(s_in, r, key_offset, lo) - m_new).astype(BF16)
        m_ref[:, lo:] = m_new
        al_out[:, lo:] = jnp.exp2(m - m_new)

    def older_pairs(m_min):
        live = jnp.logical_and(lane_row < i, qk_bound + fq_max - stat >= m_min - SKIP_LOG2)
        first = jnp.min(jnp.where(live, lane_row, i).astype(F32), axis=1, keepdims=True)
        return i - first[0, 0].astype(jnp.int32)

    def finish():
        o_ref[...] = (acc_ref[:dh, :] / acc_ref[dh:dh + 1, :]).T.astype(o_ref.dtype)

    def diagonal_scores():
        qk(2 * i, sa_ref)
        qk(2 * i + 1, sb_ref, tk)
        pb_ref[:, :tk] = jnp.zeros((tk, tk), BF16)

    stat = kstat_ref[...]
    kn2 = jnp.sum(jnp.where(lane_row == LANES - 1, stat, 0.0), axis=1, keepdims=True)
    qk_bound = jnp.sqrt(qn2 * kn2)
    calm = (2.0 * qk_bound)[0, 0] < SAFE_LOG2

    @pl.when(calm)
    def _():
        diagonal_scores()
        softmax_fixed(sa_ref, pa_ref, qk_bound, 0)
        pv(2 * i, pa_ref)
        softmax_fixed(sb_ref, pb_ref, qk_bound, tk, tk)
        n_old = older_pairs(qk_bound)

        def body(j, carry):
            k = i - 1 - j
            qk(2 * k, sa_ref)
            qk(2 * k + 1, sb_ref)
            pv(2 * k + 3, pb_ref)
            softmax_fixed(sa_ref, pa_ref, qk_bound)
            pv(2 * k, pa_ref)
            softmax_fixed(sb_ref, pb_ref, qk_bound)
            return carry

        lax.fori_loop(0, n_old, body, 0)
        pv(2 * (i - n_old) + 1, pb_ref)
        finish()

    @pl.when(jnp.logical_not(calm))
    def _():
        diagonal_scores()
        m_ref[...] = jnp.full(m_ref.shape, -jnp.inf, F32)
        softmax(sa_ref, pa_ref, ala_ref, 0)
        pv(2 * i, pa_ref, ala_ref)
        alb_ref[:, :tk] = jnp.ones((1, tk), F32)
        softmax(sb_ref, pb_ref, alb_ref, tk, tk)
        n_old = older_pairs(jnp.min(m_ref[...], axis=1, keepdims=True))

        def body(j, carry):
            k = i - 1 - j
            qk(2 * k, sa_ref)
            qk(2 * k + 1, sb_ref)
            pv(2 * k + 3, pb_ref, alb_ref)
            softmax(sa_ref, pa_ref, ala_ref)
            pv(2 * k, pa_ref, ala_ref)
            softmax(sb_ref, pb_ref, alb_ref)
            return carry

        lax.fori_loop(0, n_old, body, 0)
        pv(2 * (i - n_old) + 1, pb_ref, alb_ref)
        finish()


def _slab_rows(a, nsteps):
    tile = 2 * SUBLANES
    if a.shape[0] % tile or a.shape[1] % LANES:
        return None
    units = a.shape[0] // tile
    for k in range(-(-units // nsteps), units + 1):
        if units % k == 0:
            return k * tile
    return None


def _fox_attn(qkv, cum, batch, heads, tq, side=()):
    t = qkv.shape[0]
    dh = qkv.shape[1] // (3 * heads)
    s = t // batch
    nq = s // tq
    tk = tq // 2
    nsteps = batch * heads * nq

    def slab(a):
        rows = _slab_rows(a, nsteps)
        last = a.shape[0] // rows - 1
        return pl.BlockSpec((rows, a.shape[1]),
                            lambda b, h, i: (jnp.minimum((b * heads + h) * nq + i, last), 0))

    outs = pl.pallas_call(
        functools.partial(_fox_attn_kernel, tk=tk, qscale=dh ** -0.5 * LOG2E, n_side=len(side)),
        out_shape=[jax.ShapeDtypeStruct((t, heads * dh), BF16)]
        + [jax.ShapeDtypeStruct(a.shape, BF16) for a in side],
        grid=(batch, heads, nq),
        in_specs=[
            pl.BlockSpec((tq, dh), lambda b, h, i: (b * nq + i, h)),
            pl.BlockSpec((s, dh), lambda b, h, i: (b, heads + h)),
            pl.BlockSpec((s, dh), lambda b, h, i: (b, 2 * heads + h)),
            pl.BlockSpec((s, LANES), lambda b, h, i: (b, 0)),
        ] + [slab(a) for a in side],
        out_specs=[pl.BlockSpec((tq, dh), lambda b, h, i: (b * nq + i, h))] + [slab(a) for a in side],
        scratch_shapes=[
            pltpu.VMEM((s, 2 * dh), BF16),
            pltpu.VMEM((s // tk, dh + ONES_ROWS, tk), BF16),
            pltpu.VMEM((tq, 2 * dh), BF16),
            pltpu.VMEM((1, LANES), F32),
            pltpu.VMEM((tk, tq), F32), pltpu.VMEM((tk, tq), F32),
            pltpu.VMEM((tk, tq), BF16), pltpu.VMEM((tk, tq), BF16),
            pltpu.VMEM((1, tq), F32),
            pltpu.VMEM((1, tq), F32), pltpu.VMEM((1, tq), F32),
            pltpu.VMEM((dh + ONES_ROWS, tq), F32),
        ],
        compiler_params=_cparams("parallel", "parallel", "arbitrary"),
        name="fox_attn",
    )(qkv, qkv, qkv, cum, *side)
    return outs[0], outs[1:]


def _sgu_kernel(x_ref, gpre_ref, sh_ref, sc_ref, win_ref, lng_ref, lnb_ref, ws_ref, bs_ref,
                wout_ref, gpost_ref, gate_ref, o_ref, u_ref, vn_ref, gated_ref):
    x = x_ref[...]
    tm = x.shape[0]
    width = u_ref.shape[1]
    groups, chunk, _ = ws_ref.shape
    gd = width // groups
    h = _prenorm(x, gpre_ref[...], sh_ref[0], sc_ref[0]).astype(BF16)
    zv = _dot(h, win_ref[:, width:])
    zu = _dot(h, win_ref[:, :width])
    v = jax.nn.gelu(zv)
    mu = jnp.mean(v, axis=-1, keepdims=True)
    vc = v - mu
    var = jnp.mean(vc * vc, axis=-1, keepdims=True)
    vn_ref[...] = (vc * lax.rsqrt(var + EPS) * lng_ref[...] + lnb_ref[...]).astype(BF16)
    u_ref[...] = jax.nn.gelu(zu)
    causal = (lax.broadcasted_iota(jnp.int32, (chunk, chunk), 1)
              <= lax.broadcasted_iota(jnp.int32, (chunk, chunk), 0))
    for g in range(groups):
        wg = jnp.where(causal, ws_ref[g], jnp.zeros((), BF16))
        bias = bs_ref[:, g:g + 1]
        cols = slice(g * gd, (g + 1) * gd)
        for c in range(tm // chunk):
            rows = slice(c * chunk, (c + 1) * chunk)
            f = _dot(wg, vn_ref[rows, cols]) + bias
            gated_ref[rows, cols] = (u_ref[rows, cols] * f).astype(BF16)
    y = _dot(gated_ref[...], wout_ref[...])
    o_ref[...] = _postnorm_residual(x, y, gpost_ref[...], gate_ref[0])


def _sgu(x, gpre, gpost, mod, w_in, ln_g, ln_b, w_s, b_st, w_out, tm):
    t, d = x.shape
    width = w_out.shape[0]
    row = lambda n: pl.BlockSpec((1, n), lambda i: (0, 0))
    return pl.pallas_call(
        _sgu_kernel,
        out_shape=jax.ShapeDtypeStruct((t, d), F32),
        grid=(t // tm,),
        in_specs=[
            pl.BlockSpec((tm, d), lambda i: (i, 0)),
            row(d), mod.spec(0, tm), mod.spec(1, tm),
            _const_spec(w_in.shape),
            row(width), row(width),
            _const_spec(w_s.shape),
            _const_spec(b_st.shape),
            _const_spec(w_out.shape),
            row(d), mod.spec(2, tm),
        ],
        out_specs=pl.BlockSpec((tm, d), lambda i: (i, 0)),
        scratch_shapes=[pltpu.VMEM((tm, width), F32), pltpu.VMEM((tm, width), BF16),
                        pltpu.VMEM((tm, width), BF16)],
        compiler_params=_cparams("parallel"),
        name="sgu",
    )(x, gpre, mod.mod, mod.mod, w_in, ln_g, ln_b, w_s, b_st, w_out, gpost, mod.mod)


def _swa_attn_kernel(sink_ref, q_ref, kc_ref, kp_ref, vc_ref, vp_ref, o_ref, *, nb, grp, scale):
    n = pl.program_id(0) % nb
    blk = q_ref.shape[0]
    half = LANES // 2
    kv_cols = kc_ref.shape[1] // LANES
    npair = grp // 2
    row = lax.broadcasted_iota(jnp.int32, (blk, blk), 0)
    col = lax.broadcasted_iota(jnp.int32, (blk, blk), 1)
    keep_p = jnp.logical_and(row > col, n > 0)
    keep_c = row <= col
    keep = jnp.concatenate([keep_p, keep_c], axis=0)
    keep = jnp.concatenate([keep] * npair, axis=1)
    lo = lax.broadcasted_iota(jnp.int32, (2 * blk, LANES), 1) < half
    scores, vts = {}, {}
    for j in range(kv_cols):
        csl = slice(j * LANES, (j + 1) * LANES)
        kf = jnp.concatenate([kp_ref[:, csl], kc_ref[:, csl]], axis=0).astype(F32) * scale
        kr = pltpu.roll(kf, half, 1)
        vt = jnp.concatenate([vp_ref[:, csl], vc_ref[:, csl]], axis=0).T
        for e in range(2):
            kh = 2 * j + e
            qcols = [q_ref[:, (kh * npair + a) * LANES:(kh * npair + a + 1) * LANES]
                     for a in range(npair)]
            rhs = jnp.concatenate(qcols, axis=0)
            vts[kh] = vt[e * half:(e + 1) * half, :]
            for p in range(2):
                src = kf if p == e else kr
                kz = jnp.where(lo if p == 0 else jnp.logical_not(lo), src, 0.0).astype(BF16)
                scores[kh, p] = _dot_nt(kz, rhs)
    probs = {}
    for (kh, p), st in scores.items():
        st = jnp.where(keep, st, -jnp.inf)
        sink = jnp.concatenate(
            [jnp.full((1, blk), sink_ref[kh * grp + 2 * a + p], F32) for a in range(npair)], axis=1)
        m = jnp.maximum(jnp.max(st, axis=0, keepdims=True), sink)
        pt = jnp.exp(st - m)
        den = jnp.sum(pt, axis=0, keepdims=True) + jnp.exp(sink - m)
        probs[kh, p] = (pt.astype(BF16), den)
    for kh in range(2 * kv_cols):
        outs = [_dot(vts[kh], probs[kh, p][0]) / probs[kh, p][1] for p in range(2)]
        for a in range(npair):
            ot = jnp.concatenate([o[:, a * blk:(a + 1) * blk] for o in outs], axis=0)
            c0 = (kh * npair + a) * LANES
            o_ref[:, c0:c0 + LANES] = ot.T.astype(o_ref.dtype)


def _swa_attn(qkv, sinks, batch, hq, hk, dh):
    t = qkv.shape[0]
    nb = t // batch // BLOCK
    grp = hq // hk
    assert 2 * dh == LANES and grp % 2 == 0 and hk % 2 == 0
    prev = lambda r: jnp.maximum(r - 1, 0)
    return pl.pallas_call(
        functools.partial(_swa_attn_kernel, nb=nb, grp=grp, scale=dh ** -0.5),
        out_shape=jax.ShapeDtypeStruct((t, hq * dh), BF16),
        grid=(t // BLOCK,),
        in_specs=[
            pl.BlockSpec(memory_space=pltpu.SMEM),
            pl.BlockSpec((BLOCK, hq * dh), lambda r: (r, 0)),
            pl.BlockSpec((BLOCK, hk * dh), lambda r: (r, grp)),
            pl.BlockSpec((BLOCK, hk * dh), lambda r: (prev(r), grp)),
            pl.BlockSpec((BLOCK, hk * dh), lambda r: (r, grp + 1)),
            pl.BlockSpec((BLOCK, hk * dh), lambda r: (prev(r), grp + 1)),
        ],
        out_specs=pl.BlockSpec((BLOCK, hq * dh), lambda r: (r, 0)),
        compiler_params=_cparams("parallel"),
        name="swa_attn",
    )(sinks, qkv, qkv, qkv, qkv, qkv)


def _pad_lanes(a, n=LANES):
    return jnp.pad(a, [(0, 0)] * (a.ndim - 1) + [(0, n - a.shape[-1])])


def kernel(x, c, positions, ada_w, ada_b, mix_pre_g, mix_post_g, ffn_pre_g, ffn_post_g, ffn_w_gu, ffn_w_down, fox_w_in, fox_b_f, fox_w_out, sgu_w_in, sgu_ln_g, sgu_ln_b, sgu_w_s, sgu_b_s, sgu_w_out, swa_w_in, swa_sinks, swa_w_out):
    batch, seq, d = x.shape
    depth = ada_w.shape[0]
    t = batch * seq
    assert batch <= SUBLANES and seq % BLOCK == 0 and d % LANES == 0

    tm = _pick(seq, 512)
    tm_proj = _pick(seq, 1024)
    xf = x.reshape(t, d)

    c_pad = jnp.pad(c, ((0, SUBLANES - batch), (0, 0)))
    mod_all = _adaln(c_pad, ada_w, ada_b).reshape(depth * SUBLANES * N_MOD, 1, d)

    dff = ffn_w_down.shape[1]
    tf = _pick(dff, 512)
    fox_in, fox_out = fox_w_in.astype(BF16), fox_w_out.astype(BF16)
    swa_out = swa_w_out.astype(BF16)
    heads0, tq0 = fox_b_f.shape[1], _pick(seq, 1024)
    ffn_side = (ffn_w_gu.reshape(-1, 2 * dff), ffn_w_down.reshape(-1, d))
    cast_in_attn = depth > 0 and all(
        _slab_rows(a, batch * heads0 * (seq // tq0)) is not None for a in ffn_side)
    if not cast_in_attn:
        ffn_gu, ffn_down = ffn_w_gu.astype(BF16), ffn_w_down.astype(BF16)

    for i in range(depth):
        kind, j = i % N_MIXERS, i // N_MIXERS
        mod = _Mod(mod_all, i, batch, seq)
        gpre, gpost = mix_pre_g[i].reshape(1, d), mix_post_g[i].reshape(1, d)
        if kind == 0:
            heads = fox_b_f.shape[1]
            nqkv = fox_w_in.shape[2] - heads
            qkv, fg = _fox_proj(xf, gpre, mod, fox_in, _pad_lanes(fox_in[j, :, nqkv:]), j, nqkv,
                                tm_proj, _pick(nqkv, min(1536, nqkv // 2)))
            cum = _gate_cumsum(fg, _pad_lanes(fox_b_f[j].reshape(1, heads)), batch, _pick(seq, 512))
            if i == 0 and cast_in_attn:
                a, (gu2, down2) = _fox_attn(qkv, cum, batch, heads, tq0, ffn_side)
                ffn_gu, ffn_down = gu2.reshape(ffn_w_gu.shape), down2.reshape(ffn_w_down.shape)
            else:
                a, _ = _fox_attn(qkv, cum, batch, heads, tq0)
            xf = _out_proj(a, fox_out, j, xf, gpost, mod, tm)
        elif kind == 1:
            groups = sgu_w_s.shape[1]
            xf = _sgu(xf, gpre, gpost, mod, sgu_w_in[j].astype(BF16),
                      sgu_ln_g[j].reshape(1, -1), sgu_ln_b[j].reshape(1, -1),
                      sgu_w_s[j].astype(BF16), _pad_lanes(sgu_b_s[j].T),
                      sgu_w_out[j].astype(BF16), _pick(seq, 256))
        else:
            dh = SWA_HEAD_DIM
            hq = swa_sinks.shape[1]
            hk = (swa_w_in.shape[2] // dh - hq) // 2
            rope = dh // 4
            half = rope // 2
            inv = ROPE_THETA ** (-jnp.arange(0, rope, 2, dtype=F32) / rope)
            lane_d = jnp.arange(LANES) % dh
            inv_l = jnp.where(lane_d < rope, inv[lane_d % half], 0.0).reshape(1, LANES).astype(F32)
            m1 = jnp.where(lane_d < half, -1.0, 0.0).reshape(1, LANES).astype(F32)
            m2 = jnp.where((lane_d >= half) & (lane_d < rope), 1.0, 0.0).reshape(1, LANES).astype(F32)
            qkv = _swa_proj(xf, gpre, mod, swa_w_in[j].astype(BF16), positions.reshape(t, 1),
                            inv_l, m1, m2, tm_proj, _pick(hk * dh, 512), (hq + hk) * dh, half)
            a = _swa_attn(qkv, swa_sinks[j], batch, hq, hk, dh)
            xf = _out_proj(a, swa_out, j, xf, gpost, mod, tm)
        xf = _ffn(xf, ffn_pre_g[i].reshape(1, d), ffn_post_g[i].reshape(1, d), mod,
                  ffn_gu, ffn_down, i, tm, tf)
    return xf.reshape(batch, seq, d)
```

```python
import functools

import jax
import jax.numpy as jnp
from jax import lax
from jax.experimental import pallas as pl
from jax.experimental.pallas import tpu as pltpu

F32 = jnp.float32
BF16 = jnp.bfloat16

EPS = 1e-6
N_MIXERS = 3
BLOCK = 128
SWA_HEAD_DIM = 64
ROPE_THETA = 500000.0
LANES = 128
SUBLANES = 8
N_MOD = 6
LOG2E = 1.4426950408889634
ONES_ROWS = 16
SOFTMAX_ROWS = 32
SAFE_LOG2 = 100.0
SKIP_LOG2 = 160.0
VMEM_LIMIT = 56 * 1024 * 1024


def _cparams(*sem):
    return pltpu.CompilerParams(dimension_semantics=sem, vmem_limit_bytes=VMEM_LIMIT)


def _const_spec(shape):
    nd = len(shape)
    return pl.BlockSpec(shape, lambda *_: (0,) * nd, pipeline_mode=pl.Buffered(1))


def _pick(n, pref):
    t = min(n, pref)
    while n % t:
        t -= LANES
    return t


def _rms(x):
    return x * lax.rsqrt(jnp.mean(x * x, axis=-1, keepdims=True) + EPS)


def _prenorm(x, g, sh, sc):
    return _rms(x) * (g * (1.0 + sc)) + sh


def _postnorm_residual(x, y, g, gate):
    return x + _rms(y) * (gate * g)


def _dot(a, b):
    return jnp.dot(a, b, preferred_element_type=F32)


def _dot_nt(a, b):
    return lax.dot_general(a, b, (((1,), (1,)), ((), ())), preferred_element_type=F32)


def _adaln_kernel(c_ref, w_ref, b_ref, o_ref):
    c = c_ref[...]
    ca = (c * jax.nn.sigmoid(c)).astype(BF16)
    o_ref[0] = _dot(ca, w_ref[0].astype(BF16)) + b_ref[0]


def _adaln(c_pad, ada_w, ada_b):
    depth, d, n = ada_w.shape
    tn = _pick(n, 1024)
    return pl.pallas_call(
        _adaln_kernel,
        out_shape=jax.ShapeDtypeStruct((depth, SUBLANES, n), F32),
        grid=(depth, n // tn),
        in_specs=[
            pl.BlockSpec((SUBLANES, d), lambda i, j: (0, 0)),
            pl.BlockSpec((1, d, tn), lambda i, j: (i, 0, j)),
            pl.BlockSpec((1, 1, tn), lambda i, j: (i, 0, j)),
        ],
        out_specs=pl.BlockSpec((1, SUBLANES, tn), lambda i, j: (i, 0, j)),
        compiler_params=_cparams("parallel", "parallel"),
        name="adaln",
    )(c_pad, ada_w, ada_b.reshape(depth, 1, n))


class _Mod:
    def __init__(self, mod, layer, batch, seq):
        self.mod = mod
        self.layer = layer
        self.batch = batch
        self.seq = seq
        self.d = mod.shape[-1]

    def spec(self, k, tm, ahead=0):
        layer, tpb = self.layer, self.seq // tm
        last = self.batch * tpb - 1
        return pl.BlockSpec(
            (1, 1, self.d),
            lambda i, *_: ((layer * SUBLANES + jnp.clip(i + ahead, 0, last) // tpb) * N_MOD + k, 0, 0))


def _x_ahead_spec(tm, d, ntiles, nj):
    first_ahead = nj - 2 if nj >= 3 else nj - 1
    return pl.BlockSpec(
        (tm, d), lambda i, j: (jnp.minimum(jnp.where(j >= first_ahead, i + 1, i), ntiles - 1), 0))


def _fox_proj_kernel(x_ref, g_ref, sh_ref, sc_ref, shn_ref, scn_ref, w_ref, wf_ref, o_ref, fg_ref,
                     h_ref):
    i, j = pl.program_id(0), pl.program_id(1)
    cur = i % 2

    @pl.when(jnp.logical_and(i == 0, j == 0))
    def _():
        h_ref[0] = _prenorm(x_ref[...], g_ref[...], sh_ref[0], sc_ref[0]).astype(BF16)

    @pl.when(j == 0)
    def _():
        h = h_ref[cur]
        fg_ref[...] = _dot(h, wf_ref[...])
        o_ref[...] = _dot(h, w_ref[...]).astype(o_ref.dtype)

    @pl.when(jnp.logical_and(j > 0, j < pl.num_programs(1) - 1))
    def _():
        o_ref[...] = _dot(h_ref[cur], w_ref[...]).astype(o_ref.dtype)

    @pl.when(j == pl.num_programs(1) - 1)
    def _():
        h_ref[1 - cur] = _prenorm(x_ref[...], g_ref[...], shn_ref[0], scn_ref[0]).astype(BF16)
        o_ref[...] = _dot(h_ref[cur], w_ref[...]).astype(o_ref.dtype)


def _fox_proj(x, gain, mod, w, wf, layer, n, tm, tn):
    t, d = x.shape
    nj = n // tn
    assert nj >= 2
    return pl.pallas_call(
        _fox_proj_kernel,
        out_shape=(jax.ShapeDtypeStruct((t, n), BF16), jax.ShapeDtypeStruct((t, LANES), F32)),
        grid=(t // tm, nj),
        in_specs=[
            _x_ahead_spec(tm, d, t // tm, nj),
            pl.BlockSpec((1, d), lambda i, j: (0, 0)),
            mod.spec(0, tm), mod.spec(1, tm), mod.spec(0, tm, 1), mod.spec(1, tm, 1),
            pl.BlockSpec((None, d, tn), lambda i, j: (layer, 0, j)),
            _const_spec((d, LANES)),
        ],
        out_specs=(pl.BlockSpec((tm, tn), lambda i, j: (i, j)),
                   pl.BlockSpec((tm, LANES), lambda i, j: (i, 0))),
        scratch_shapes=[pltpu.VMEM((2, tm, d), BF16)],
        compiler_params=_cparams("arbitrary", "arbitrary"),
        name="fox_proj",
    )(x, gain, mod.mod, mod.mod, mod.mod, mod.mod, w, wf)


def _swa_proj_kernel(x_ref, g_ref, sh_ref, sc_ref, shn_ref, scn_ref, w_ref, pos_ref, inv_ref,
                     m1_ref, m2_ref, o_ref, h_ref, cos_ref, s1_ref, s2_ref, *, n_rope_tiles, shift):
    i, j = pl.program_id(0), pl.program_id(1)
    cur = i % 2

    def stage(slot, sh, sc):
        h_ref[slot] = _prenorm(x_ref[...], g_ref[...], sh[0], sc[0]).astype(BF16)
        ang = pos_ref[...].astype(F32) * inv_ref[...]
        sn = jnp.sin(ang)
        cos_ref[slot] = jnp.cos(ang)
        s1_ref[slot] = sn * m1_ref[...]
        s2_ref[slot] = sn * m2_ref[...]

    def rope_tile():
        acc = _dot(h_ref[cur], w_ref[...])
        cs, s1, s2 = cos_ref[cur], s1_ref[cur], s2_ref[cur]
        for c in range(acc.shape[1] // LANES):
            a = acc[:, c * LANES:(c + 1) * LANES]
            r = a * cs + pltpu.roll(a, LANES - shift, 1) * s1 + pltpu.roll(a, shift, 1) * s2
            o_ref[:, c * LANES:(c + 1) * LANES] = r.astype(o_ref.dtype)

    @pl.when(jnp.logical_and(i == 0, j == 0))
    def _():
        stage(0, sh_ref, sc_ref)

    @pl.when(j < n_rope_tiles)
    def _():
        rope_tile()

    @pl.when(jnp.logical_and(j >= n_rope_tiles, j < pl.num_programs(1) - 1))
    def _():
        o_ref[...] = _dot(h_ref[cur], w_ref[...]).astype(o_ref.dtype)

    @pl.when(j == pl.num_programs(1) - 1)
    def _():
        stage(1 - cur, shn_ref, scn_ref)
        o_ref[...] = _dot(h_ref[cur], w_ref[...]).astype(o_ref.dtype)


def _swa_proj(x, gain, mod, w, pos, inv_l, m1, m2, tm, tn, n_rope_cols, shift):
    t, d = x.shape
    n = w.shape[1]
    nt, nj = t // tm, n // tn
    assert n_rope_cols // tn < nj
    row = lambda shape: pl.BlockSpec(shape, lambda i, j: (0, 0))
    ahead = lambda i, j: (jnp.where(j == nj - 1, jnp.minimum(i + 1, nt - 1), i), 0)
    return pl.pallas_call(
        functools.partial(_swa_proj_kernel, n_rope_tiles=n_rope_cols // tn, shift=shift),
        out_shape=jax.ShapeDtypeStruct((t, n), BF16),
        grid=(nt, nj),
        in_specs=[
            _x_ahead_spec(tm, d, nt, nj),
            row((1, d)),
            mod.spec(0, tm), mod.spec(1, tm), mod.spec(0, tm, 1), mod.spec(1, tm, 1),
            pl.BlockSpec((d, tn), lambda i, j: (0, j)),
            pl.BlockSpec((tm, 1), ahead),
            row((1, LANES)), row((1, LANES)), row((1, LANES)),
        ],
        out_specs=pl.BlockSpec((tm, tn), lambda i, j: (i, j)),
        scratch_shapes=[pltpu.VMEM((2, tm, d), BF16)] + [pltpu.VMEM((2, tm, LANES), F32)] * 3,
        compiler_params=_cparams("arbitrary", "arbitrary"),
        name="swa_proj",
    )(x, gain, mod.mod, mod.mod, mod.mod, mod.mod, w, pos, inv_l, m1, m2)


def _out_kernel(a_ref, w_ref, x_ref, g_ref, gate_ref, o_ref):
    y = _dot(a_ref[...], w_ref[...])
    o_ref[...] = _postnorm_residual(x_ref[...], y, g_ref[...], gate_ref[0])


def _out_proj(a, w, layer, x, gain, mod, tm):
    t, k = a.shape
    d = w.shape[2]
    return pl.pallas_call(
        _out_kernel,
        out_shape=jax.ShapeDtypeStruct((t, d), F32),
        grid=(t // tm,),
        in_specs=[
            pl.BlockSpec((tm, k), lambda i: (i, 0)),
            pl.BlockSpec((None, k, d), lambda i: (layer, 0, 0), pipeline_mode=pl.Buffered(1)),
            pl.BlockSpec((tm, d), lambda i: (i, 0)),
            pl.BlockSpec((1, d), lambda i: (0, 0)),
            mod.spec(2, tm),
        ],
        out_specs=pl.BlockSpec((tm, d), lambda i: (i, 0)),
        compiler_params=_cparams("parallel"),
        name="out_proj",
    )(a, w, x, gain, mod.mod)


def _ffn_kernel(x_ref, gpre_ref, sh_ref, sc_ref, shn_ref, scn_ref, wg_ref, wu_ref, wd_ref,
                gpost_ref, gate_ref, o_ref, h_ref, xs_ref, acc_ref):
    i, j = pl.program_id(0), pl.program_id(1)
    last = pl.num_programs(1) - 1
    cur = i % 2

    def stage(slot, sh, sc):
        x = x_ref[...]
        xs_ref[slot] = x
        h_ref[slot] = _prenorm(x, gpre_ref[...], sh[0], sc[0]).astype(BF16)

    def chunk():
        h = h_ref[cur]
        g = _dot(h, wg_ref[...])
        u = _dot(h, wu_ref[...])
        a = (g * jax.nn.sigmoid(g) * u).astype(BF16)
        return _dot(a, wd_ref[...])

    @pl.when(jnp.logical_and(i == 0, j == 0))
    def _():
        stage(0, sh_ref, sc_ref)

    @pl.when(j == 0)
    def _():
        acc_ref[...] = chunk()

    @pl.when(jnp.logical_and(j > 0, j < last))
    def _():
        acc_ref[...] += chunk()

    @pl.when(j == last)
    def _():
        stage(1 - cur, shn_ref, scn_ref)
        y = acc_ref[...] + chunk()
        o_ref[...] = _postnorm_residual(xs_ref[cur], y, gpost_ref[...], gate_ref[0])


def _ffn(x, gpre, gpost, mod, w_gu, w_down, layer, tm, tf):
    t, d = x.shape
    dff = w_down.shape[1]
    nf = dff // tf
    assert nf >= 2
    row = lambda: pl.BlockSpec((1, d), lambda i, j: (0, 0))
    return pl.pallas_call(
        _ffn_kernel,
        out_shape=jax.ShapeDtypeStruct((t, d), F32),
        grid=(t // tm, nf),
        in_specs=[
            _x_ahead_spec(tm, d, t // tm, nf),
            row(), mod.spec(3, tm), mod.spec(4, tm), mod.spec(3, tm, 1), mod.spec(4, tm, 1),
            pl.BlockSpec((None, d, tf), lambda i, j: (layer, 0, j)),
            pl.BlockSpec((None, d, tf), lambda i, j: (layer, 0, nf + j)),
            pl.BlockSpec((None, tf, d), lambda i, j: (layer, j, 0)),
            row(), mod.spec(5, tm),
        ],
        out_specs=pl.BlockSpec((tm, d), lambda i, j: (i, 0)),
        scratch_shapes=[pltpu.VMEM((2, tm, d), BF16), pltpu.VMEM((2, tm, d), F32),
                        pltpu.VMEM((tm, d), F32)],
        compiler_params=_cparams("arbitrary", "arbitrary"),
        name="ffn",
    )(x, gpre, mod.mod, mod.mod, mod.mod, mod.mod, w_gu, w_gu, w_down, gpost, mod.mod)


def _ffn_stream_kernel(x_ref, xn_ref, gpre_ref, sh_ref, sc_ref, shn_ref, scn_ref, wgu_hbm, wd_hbm,
                       gpost_ref, gate_ref, o_ref, wg_buf, wu_buf, wd_buf, sem, h_ref, acc_ref,
                       *, layer, nf, tf):
    i = pl.program_id(0)
    nt = pl.num_programs(0)
    cur = i % 2
    dff = nf * tf

    def copies(c, slot):
        col = pl.multiple_of(c * tf, tf)
        return (
            pltpu.make_async_copy(wgu_hbm.at[layer, :, pl.ds(col, tf)], wg_buf.at[slot], sem.at[slot, 0]),
            pltpu.make_async_copy(wgu_hbm.at[layer, :, pl.ds(dff + col, tf)], wu_buf.at[slot],
                                  sem.at[slot, 1]),
            pltpu.make_async_copy(wd_hbm.at[layer, pl.ds(col, tf), :], wd_buf.at[slot], sem.at[slot, 2]),
        )

    def start(c, slot):
        for cp in copies(c, slot):
            cp.start()

    def wait(c, slot):
        for cp in copies(c, slot):
            cp.wait()

    def chunk(slot):
        h = h_ref[cur]
        g = _dot(h, wg_buf[slot])
        u = _dot(h, wu_buf[slot])
        a = (g * jax.nn.sigmoid(g) * u).astype(BF16)
        return _dot(a, wd_buf[slot])

    @pl.when(i == 0)
    def _():
        start(0, 0)
        h_ref[0] = _prenorm(x_ref[...], gpre_ref[...], sh_ref[0], sc_ref[0]).astype(BF16)

    base = i * nf
    acc_ref[...] = jnp.zeros(acc_ref.shape, F32)

    def body(c, carry):
        slot = (base + c) % 2
        start(c + 1, 1 - slot)
        wait(c, slot)
        acc_ref[...] += chunk(slot)
        return carry

    lax.fori_loop(0, nf - 1, body, 0)
    slot = (base + nf - 1) % 2

    @pl.when(i + 1 < nt)
    def _():
        start(0, 1 - slot)

    wait(nf - 1, slot)
    h_ref[1 - cur] = _prenorm(xn_ref[...], gpre_ref[...], shn_ref[0], scn_ref[0]).astype(BF16)
    y = acc_ref[...] + chunk(slot)
    o_ref[...] = _postnorm_residual(x_ref[...], y, gpost_ref[...], gate_ref[0])


def _ffn_stream(x, gpre, gpost, mod, w_gu, w_down, layer, tm, tf):
    t, d = x.shape
    dff = w_down.shape[1]
    nt, nf = t // tm, dff // tf
    assert nf >= 2
    row = lambda: pl.BlockSpec((1, d), lambda i: (0, 0))
    return pl.pallas_call(
        functools.partial(_ffn_stream_kernel, layer=layer, nf=nf, tf=tf),
        out_shape=jax.ShapeDtypeStruct((t, d), F32),
        grid=(nt,),
        in_specs=[
            pl.BlockSpec((tm, d), lambda i: (i, 0)),
            pl.BlockSpec((tm, d), lambda i: (jnp.minimum(i + 1, nt - 1), 0)),
            row(), mod.spec(3, tm), mod.spec(4, tm), mod.spec(3, tm, 1), mod.spec(4, tm, 1),
            pl.BlockSpec(memory_space=pl.ANY), pl.BlockSpec(memory_space=pl.ANY),
            row(), mod.spec(5, tm),
        ],
        out_specs=pl.BlockSpec((tm, d), lambda i: (i, 0)),
        scratch_shapes=[pltpu.VMEM((2, d, tf), BF16), pltpu.VMEM((2, d, tf), BF16),
                        pltpu.VMEM((2, tf, d), BF16), pltpu.SemaphoreType.DMA((2, 3)),
                        pltpu.VMEM((2, tm, d), BF16), pltpu.VMEM((tm, d), F32)],
        compiler_params=_cparams("arbitrary"),
        name="ffn",
    )(x, x, gpre, mod.mod, mod.mod, mod.mod, mod.mod, w_gu, w_down, gpost, mod.mod)


def _gate_kernel(fg_ref, b_ref, cum_ref, carry_ref):
    @pl.when(pl.program_id(1) == 0)
    def _():
        carry_ref[...] = jnp.zeros_like(carry_ref)

    z = fg_ref[...] + b_ref[...]
    lf = jnp.minimum(z, 0.0) - jnp.log1p(jnp.exp(-jnp.abs(z)))
    n = lf.shape[0]
    tril = (lax.broadcasted_iota(jnp.int32, (n, n), 1)
            <= lax.broadcasted_iota(jnp.int32, (n, n), 0)).astype(BF16)
    hi = lf.astype(BF16)
    r1 = lf - hi.astype(F32)
    mid = r1.astype(BF16)
    lo = (r1 - mid.astype(F32)).astype(BF16)
    cum = (_dot(tril, hi) + _dot(tril, mid)) + _dot(tril, lo) + carry_ref[...]
    cum_ref[...] = cum
    carry_ref[...] = cum[n - 1:n, :]


def _gate_cumsum(fg, b_pad, batch, tg):
    t = fg.shape[0]
    s = t // batch
    nt = s // tg
    return pl.pallas_call(
        _gate_kernel,
        out_shape=jax.ShapeDtypeStruct((t, LANES), F32),
        grid=(batch, nt),
        in_specs=[
            pl.BlockSpec((tg, LANES), lambda b, j: (b * nt + j, 0)),
            pl.BlockSpec((1, LANES), lambda b, j: (0, 0)),
        ],
        out_specs=pl.BlockSpec((tg, LANES), lambda b, j: (b * nt + j, 0)),
        scratch_shapes=[pltpu.VMEM((1, LANES), F32)],
        compiler_params=_cparams("parallel", "arbitrary"),
        name="fox_gate",
    )(fg, b_pad)


def _lane_col(x, h):
    lane = lax.broadcasted_iota(jnp.int32, x.shape, 1)
    return jnp.sum(jnp.where(lane == h, x, 0.0), axis=-1, keepdims=True)


def _decay_lanes(x, h, base):
    hi = x.astype(BF16)
    r = x - hi.astype(F32)
    mid = r.astype(BF16)
    lo = (r - mid.astype(F32)).astype(BF16)
    src = lax.broadcasted_iota(jnp.int32, (3 * LANES, LANES), 0)
    dst = lax.broadcasted_iota(jnp.int32, (3 * LANES, LANES), 1)
    place = jnp.where(jnp.logical_and(src % LANES == h, dst == base + src // LANES), 1.0, 0.0)
    moved = _dot(jnp.concatenate([hi, mid, lo], axis=1), place.astype(BF16))
    lane = lax.broadcasted_iota(jnp.int32, (1, LANES), 1)
    ones = jnp.where(jnp.logical_and(lane >= 3 - base, lane < 6 - base), 1.0, 0.0)
    return (moved + ones).astype(BF16)


def _fox_attn_kernel(*refs, tk, qscale, n_side):
    q_ref, k_ref, v_ref, cum_ref = refs[:4]
    side_in = refs[4:4 + n_side]
    o_ref = refs[4 + n_side]
    side_out = refs[5 + n_side:5 + 2 * n_side]
    (ka_ref, vt_ref, qa_ref, kstat_ref, sa_ref, sb_ref, pa_ref, pb_ref, m_ref, ala_ref, alb_ref,
     acc_ref) = refs[5 + 2 * n_side:]
    for src, dst in zip(side_in, side_out):
        dst[...] = src[...].astype(BF16)

    h = pl.program_id(1)
    i = pl.program_id(2)
    tq, dh = q_ref.shape
    nkv = vt_ref.shape[0]
    lane_row = lax.broadcasted_iota(jnp.int32, (1, LANES), 1)

    @pl.when(i == 0)
    def _():
        kn2 = jnp.zeros((1, 1), F32)
        for c in range(nkv):
            rows = slice(c * tk, (c + 1) * tk)
            kc = k_ref[rows, :]
            kf = kc.astype(F32)
            kn2 = jnp.maximum(kn2, jnp.max(jnp.sum(kf * kf, axis=1, keepdims=True),
                                           axis=0, keepdims=True))
            ka_ref[rows, :dh] = kc
            ka_ref[rows, dh:] = _decay_lanes(cum_ref[rows, :] * (-LOG2E), h, 0)
            vt_ref[c, :dh, :] = v_ref[rows, :].T
            vt_ref[c, dh:, :] = jnp.ones((vt_ref.shape[1] - dh, tk), BF16)
        stat = jnp.where(lane_row == LANES - 1, kn2, 0.0)
        for k in range(nkv // 2):
            last = (k + 1) * tq - 1
            stat = jnp.where(lane_row == k, _lane_col(cum_ref[last:last + 1, :], h) * LOG2E, stat)
        kstat_ref[...] = stat

    qs = (q_ref[...].astype(F32) * qscale).astype(BF16)
    fq = cum_ref[pl.ds(pl.multiple_of(i * tq, tq), tq), :] * LOG2E
    qa_ref[:, :dh] = qs
    qa_ref[:, dh:] = _decay_lanes(fq, h, 3)
    qf = qs.astype(F32)
    qn2 = jnp.max(jnp.sum(qf * qf, axis=1, keepdims=True), axis=0, keepdims=True)
    fq_max = _lane_col(jnp.max(fq, axis=0, keepdims=True), h)
    acc_ref[...] = jnp.zeros(acc_ref.shape, F32)

    def qk(c, s_out, lo=0):
        ka = ka_ref[pl.ds(pl.multiple_of(c * tk, tk), tk), :]
        s_out[:, lo:] = _dot_nt(ka, qa_ref[lo:, :])

    def pv(c, p_in, al_in=None):
        acc = acc_ref[...] if al_in is None else al_in[...] * acc_ref[...]
        acc_ref[...] = acc + _dot(vt_ref[c], p_in[...])

    rc = SOFTMAX_ROWS
    nrc = tk // rc

    def chunk(s_in, r, key_offset, lo):
        blk = s_in[r * rc:(r + 1) * rc, lo:]
        if key_offset is not None:
            col = lax.broadcasted_iota(jnp.int32, blk.shape, 1) + lo
            row = lax.broadcasted_iota(jnp.int32, blk.shape, 0)
            blk = jnp.where(row + (r * rc + key_offset) <= col, blk, -jnp.inf)
        return blk

    def column_max(blocks, width):
        part = jnp.full((SUBLANES, width), -jnp.inf, F32)
        for blk in blocks:
            part = jnp.maximum(part, jnp.max(blk.reshape(rc // SUBLANES, SUBLANES, width), axis=0))
        return jnp.max(part, axis=0, keepdims=True)

    def softmax_fixed(s_in, p_out, ref, key_offset=None, lo=0):
        for r in range(nrc):
            p_out[r * rc:(r + 1) * rc, lo:] = jnp.exp2(chunk(s_in, r, key_offset, lo) - ref).astype(BF16)

    def softmax(s_in, p_out, al_out, key_offset=None, lo=0):
        m = m_ref[:, lo:]
        m_new = jnp.maximum(m, column_max((chunk(s_in, r, key_offset, lo) for r in range(nrc)),
                                          tq - lo))
        for r in range(nrc):
            p_out[r * rc:(r + 1) * rc, lo:] = jnp.exp2(chunk(s_in, r, key_offset, lo) - m_new).astype(BF16)
        m_ref[:, lo:] = m_new
        al_out[:, lo:] = jnp.exp2(m - m_new)

    def older_pairs(m_min):
        live = jnp.logical_and(lane_row < i, qk_bound + fq_max - stat >= m_min - SKIP_LOG2)
        first = jnp.min(jnp.where(live, lane_row, i).astype(F32), axis=1, keepdims=True)
        return i - first[0, 0].astype(jnp.int32)

    def finish():
        o_ref[...] = (acc_ref[:dh, :] / acc_ref[dh:dh + 1, :]).T.astype(o_ref.dtype)

    def diagonal_scores():
        qk(2 * i, sa_ref)
        qk(2 * i + 1, sb_ref, tk)
        pb_ref[:, :tk] = jnp.zeros((tk, tk), BF16)

    stat = kstat_ref[...]
    kn2 = jnp.sum(jnp.where(lane_row == LANES - 1, stat, 0.0), axis=1, keepdims=True)
    qk_bound = jnp.sqrt(qn2 * kn2)
    calm = (2.0 * qk_bound)[0, 0] < SAFE_LOG2

    @pl.when(calm)
    def _():
        diagonal_scores()
        softmax_fixed(sa_ref, pa_ref, qk_bound, 0)
        pv(2 * i, pa_ref)
        softmax_fixed(sb_ref, pb_ref, qk_bound, tk, tk)
        n_old = older_pairs(qk_bound)

        def body(j, carry):
            k = i - 1 - j
            qk(2 * k, sa_ref)
            qk(2 * k + 1, sb_ref)
            pv(2 * k + 3, pb_ref)
            softmax_fixed(sa_ref, pa_ref, qk_bound)
            pv(2 * k, pa_ref)
            softmax_fixed(sb_ref, pb_ref, qk_bound)
            return carry

        lax.fori_loop(0, n_old, body, 0)
        pv(2 * (i - n_old) + 1, pb_ref)
        finish()

    @pl.when(jnp.logical_not(calm))
    def _():
        diagonal_scores()
        m_ref[...] = jnp.full(m_ref.shape, -jnp.inf, F32)
        softmax(sa_ref, pa_ref, ala_ref, 0)
        pv(2 * i, pa_ref, ala_ref)
        alb_ref[:, :tk] = jnp.ones((1, tk), F32)
        softmax(sb_ref, pb_ref, alb_ref, tk, tk)
        n_old = older_pairs(jnp.min(m_ref[...], axis=1, keepdims=True))

        def body(j, carry):
            k = i - 1 - j
            qk(2 * k, sa_ref)
            qk(2 * k + 1, sb_ref)
            pv(2 * k + 3, pb_ref, alb_ref)
            softmax(sa_ref, pa_ref, ala_ref)
            pv(2 * k, pa_ref, ala_ref)
            softmax(sb_ref, pb_ref, alb_ref)
            return carry

        lax.fori_loop(0, n_old, body, 0)
        pv(2 * (i - n_old) + 1, pb_ref, alb_ref)
        finish()


def _slab_rows(a, nsteps):
    tile = 2 * SUBLANES
    if a.shape[0] % tile or a.shape[1] % LANES:
        return None
    units = a.shape[0] // tile
    for k in range(-(-units // nsteps), units + 1):
        if units % k == 0:
            return k * tile
    return None


def _fox_attn(qkv, cum, batch, heads, tq, side=()):
    t = qkv.shape[0]
    dh = qkv.shape[1] // (3 * heads)
    s = t // batch
    nq = s // tq
    tk = tq // 2
    nsteps = batch * heads * nq

    def slab(a):
        rows = _slab_rows(a, nsteps)
        last = a.shape[0] // rows - 1
        return pl.BlockSpec((rows, a.shape[1]),
                            lambda b, h, i: (jnp.minimum((b * heads + h) * nq + i, last), 0))

    outs = pl.pallas_call(
        functools.partial(_fox_attn_kernel, tk=tk, qscale=dh ** -0.5 * LOG2E, n_side=len(side)),
        out_shape=[jax.ShapeDtypeStruct((t, heads * dh), BF16)]
        + [jax.ShapeDtypeStruct(a.shape, BF16) for a in side],
        grid=(batch, heads, nq),
        in_specs=[
            pl.BlockSpec((tq, dh), lambda b, h, i: (b * nq + i, h)),
            pl.BlockSpec((s, dh), lambda b, h, i: (b, heads + h)),
            pl.BlockSpec((s, dh), lambda b, h, i: (b, 2 * heads + h)),
            pl.BlockSpec((s, LANES), lambda b, h, i: (b, 0)),
        ] + [slab(a) for a in side],
        out_specs=[pl.BlockSpec((tq, dh), lambda b, h, i: (b * nq + i, h))] + [slab(a) for a in side],
        scratch_shapes=[
            pltpu.VMEM((s, 2 * dh), BF16),
            pltpu.VMEM((s // tk, dh + ONES_ROWS, tk), BF16),
            pltpu.VMEM((tq, 2 * dh), BF16),
            pltpu.VMEM((1, LANES), F32),
            pltpu.VMEM((tk, tq), F32), pltpu.VMEM((tk, tq), F32),
            pltpu.VMEM((tk, tq), BF16), pltpu.VMEM((tk, tq), BF16),
            pltpu.VMEM((1, tq), F32),
            pltpu.VMEM((1, tq), F32), pltpu.VMEM((1, tq), F32),
            pltpu.VMEM((dh + ONES_ROWS, tq), F32),
        ],
        compiler_params=_cparams("parallel", "parallel", "arbitrary"),
        name="fox_attn",
    )(qkv, qkv, qkv, cum, *side)
    return outs[0], outs[1:]


def _sgu_kernel(x_ref, gpre_ref, sh_ref, sc_ref, win_ref, lng_ref, lnb_ref, ws_ref, bs_ref,
                wout_ref, gpost_ref, gate_ref, o_ref, u_ref, vn_ref, gated_ref):
    x = x_ref[...]
    tm = x.shape[0]
    width = u_ref.shape[1]
    groups, chunk, _ = ws_ref.shape
    gd = width // groups
    h = _prenorm(x, gpre_ref[...], sh_ref[0], sc_ref[0]).astype(BF16)
    zv = _dot(h, win_ref[:, width:])
    zu = _dot(h, win_ref[:, :width])
    v = jax.nn.gelu(zv)
    mu = jnp.mean(v, axis=-1, keepdims=True)
    vc = v - mu
    var = jnp.mean(vc * vc, axis=-1, keepdims=True)
    vn_ref[...] = (vc * lax.rsqrt(var + EPS) * lng_ref[...] + lnb_ref[...]).astype(BF16)
    u_ref[...] = jax.nn.gelu(zu)
    causal = (lax.broadcasted_iota(jnp.int32, (chunk, chunk), 1)
              <= lax.broadcasted_iota(jnp.int32, (chunk, chunk), 0))
    for g in range(groups):
        wg = jnp.where(causal, ws_ref[g], jnp.zeros((), BF16))
        bias = bs_ref[:, g:g + 1]
        cols = slice(g * gd, (g + 1) * gd)
        for c in range(tm // chunk):
            rows = slice(c * chunk, (c + 1) * chunk)
            f = _dot(wg, vn_ref[rows, cols]) + bias
            gated_ref[rows, cols] = (u_ref[rows, cols] * f).astype(BF16)
    y = _dot(gated_ref[...], wout_ref[...])
    o_ref[...] = _postnorm_residual(x, y, gpost_ref[...], gate_ref[0])


def _sgu(x, gpre, gpost, mod, w_in, ln_g, ln_b, w_s, b_st, w_out, tm):
    t, d = x.shape
    width = w_out.shape[0]
    row = lambda n: pl.BlockSpec((1, n), lambda i: (0, 0))
    return pl.pallas_call(
        _sgu_kernel,
        out_shape=jax.ShapeDtypeStruct((t, d), F32),
        grid=(t // tm,),
        in_specs=[
            pl.BlockSpec((tm, d), lambda i: (i, 0)),
            row(d), mod.spec(0, tm), mod.spec(1, tm),
            _const_spec(w_in.shape),
            row(width), row(width),
            _const_spec(w_s.shape),
            _const_spec(b_st.shape),
            _const_spec(w_out.shape),
            row(d), mod.spec(2, tm),
        ],
        out_specs=pl.BlockSpec((tm, d), lambda i: (i, 0)),
        scratch_shapes=[pltpu.VMEM((tm, width), F32), pltpu.VMEM((tm, width), BF16),
                        pltpu.VMEM((tm, width), BF16)],
        compiler_params=_cparams("parallel"),
        name="sgu",
    )(x, gpre, mod.mod, mod.mod, w_in, ln_g, ln_b, w_s, b_st, w_out, gpost, mod.mod)


def _swa_attn_kernel(sink_ref, q_ref, kc_ref, kp_ref, vc_ref, vp_ref, o_ref, *, nb, grp, scale):
    n = pl.program_id(0) % nb
    blk = q_ref.shape[0]
    half = LANES // 2
    kv_cols = kc_ref.shape[1] // LANES
    npair = grp // 2
    row = lax.broadcasted_iota(jnp.int32, (blk, blk), 0)
    col = lax.broadcasted_iota(jnp.int32, (blk, blk), 1)
    keep_p = jnp.logical_and(row > col, n > 0)
    keep_c = row <= col
    keep = jnp.concatenate([keep_p, keep_c], axis=0)
    keep = jnp.concatenate([keep] * npair, axis=1)
    lo = lax.broadcasted_iota(jnp.int32, (2 * blk, LANES), 1) < half
    scores, vts = {}, {}
    for j in range(kv_cols):
        csl = slice(j * LANES, (j + 1) * LANES)
        kf = jnp.concatenate([kp_ref[:, csl], kc_ref[:, csl]], axis=0).astype(F32) * scale
        kr = pltpu.roll(kf, half, 1)
        vt = jnp.concatenate([vp_ref[:, csl], vc_ref[:, csl]], axis=0).T
        for e in range(2):
            kh = 2 * j + e
            qcols = [q_ref[:, (kh * npair + a) * LANES:(kh * npair + a + 1) * LANES]
                     for a in range(npair)]
            rhs = jnp.concatenate(qcols, axis=0)
            vts[kh] = vt[e * half:(e + 1) * half, :]
            for p in range(2):
                src = kf if p == e else kr
                kz = jnp.where(lo if p == 0 else jnp.logical_not(lo), src, 0.0).astype(BF16)
                scores[kh, p] = _dot_nt(kz, rhs)
    probs = {}
    for (kh, p), st in scores.items():
        st = jnp.where(keep, st, -jnp.inf)
        sink = jnp.concatenate(
            [jnp.full((1, blk), sink_ref[kh * grp + 2 * a + p], F32) for a in range(npair)], axis=1)
        m = jnp.maximum(jnp.max(st, axis=0, keepdims=True), sink)
        pt = jnp.exp(st - m)
        den = jnp.sum(pt, axis=0, keepdims=True) + jnp.exp(sink - m)
        probs[kh, p] = (pt.astype(BF16), den)
    for kh in range(2 * kv_cols):
        outs = [_dot(vts[kh], probs[kh, p][0]) / probs[kh, p][1] for p in range(2)]
        for a in range(npair):
            ot = jnp.concatenate([o[:, a * blk:(a + 1) * blk] for o in outs], axis=0)
            c0 = (kh * npair + a) * LANES
            o_ref[:, c0:c0 + LANES] = ot.T.astype(o_ref.dtype)


def _swa_attn(qkv, sinks, batch, hq, hk, dh):
    t = qkv.shape[0]
    nb = t // batch // BLOCK
    grp = hq // hk
    assert 2 * dh == LANES and grp % 2 == 0 and hk % 2 == 0
    prev = lambda r: jnp.maximum(r - 1, 0)
    return pl.pallas_call(
        functools.partial(_swa_attn_kernel, nb=nb, grp=grp, scale=dh ** -0.5),
        out_shape=jax.ShapeDtypeStruct((t, hq * dh), BF16),
        grid=(t // BLOCK,),
        in_specs=[
            pl.BlockSpec(memory_space=pltpu.SMEM),
            pl.BlockSpec((BLOCK, hq * dh), lambda r: (r, 0)),
            pl.BlockSpec((BLOCK, hk * dh), lambda r: (r, grp)),
            pl.BlockSpec((BLOCK, hk * dh), lambda r: (prev(r), grp)),
            pl.BlockSpec((BLOCK, hk * dh), lambda r: (r, grp + 1)),
            pl.BlockSpec((BLOCK, hk * dh), lambda r: (prev(r), grp + 1)),
        ],
        out_specs=pl.BlockSpec((BLOCK, hq * dh), lambda r: (r, 0)),
        compiler_params=_cparams("parallel"),
        name="swa_attn",
    )(sinks, qkv, qkv, qkv, qkv, qkv)


def _pad_lanes(a, n=LANES):
    return jnp.pad(a, [(0, 0)] * (a.ndim - 1) + [(0, n - a.shape[-1])])


def kernel(x, c, positions, ada_w, ada_b, mix_pre_g, mix_post_g, ffn_pre_g, ffn_post_g, ffn_w_gu, ffn_w_down, fox_w_in, fox_b_f, fox_w_out, sgu_w_in, sgu_ln_g, sgu_ln_b, sgu_w_s, sgu_b_s, sgu_w_out, swa_w_in, swa_sinks, swa_w_out):
    batch, seq, d = x.shape
    depth = ada_w.shape[0]
    t = batch * seq
    assert batch <= SUBLANES and seq % BLOCK == 0 and d % LANES == 0

    tm = _pick(seq, 512)
    tm_proj = _pick(seq, 1024)
    xf = x.reshape(t, d)

    c_pad = jnp.pad(c, ((0, SUBLANES - batch), (0, 0)))
    mod_all = _adaln(c_pad, ada_w, ada_b).reshape(depth * SUBLANES * N_MOD, 1, d)

    dff = ffn_w_down.shape[1]
    tf = _pick(dff, 512)
    fox_in, fox_out = fox_w_in.astype(BF16), fox_w_out.astype(BF16)
    swa_out = swa_w_out.astype(BF16)
    heads0, tq0 = fox_b_f.shape[1], _pick(seq, 1024)
    ffn_side = (ffn_w_gu.reshape(-1, 2 * dff), ffn_w_down.reshape(-1, d))
    cast_in_attn = depth > 0 and all(
        _slab_rows(a, batch * heads0 * (seq // tq0)) is not None for a in ffn_side)
    if not cast_in_attn:
        ffn_gu, ffn_down = ffn_w_gu.astype(BF16), ffn_w_down.astype(BF16)

    for i in range(depth):
        kind, j = i % N_MIXERS, i // N_MIXERS
        mod = _Mod(mod_all, i, batch, seq)
        gpre, gpost = mix_pre_g[i].reshape(1, d), mix_post_g[i].reshape(1, d)
        if kind == 0:
            heads = fox_b_f.shape[1]
            nqkv = fox_w_in.shape[2] - heads
            qkv, fg = _fox_proj(xf, gpre, mod, fox_in, _pad_lanes(fox_in[j, :, nqkv:]), j, nqkv,
                                tm_proj, _pick(nqkv, min(1536, nqkv // 2)))
            cum = _gate_cumsum(fg, _pad_lanes(fox_b_f[j].reshape(1, heads)), batch, _pick(seq, 512))
            if i == 0 and cast_in_attn:
                a, (gu2, down2) = _fox_attn(qkv, cum, batch, heads, tq0, ffn_side)
                ffn_gu, ffn_down = gu2.reshape(ffn_w_gu.shape), down2.reshape(ffn_w_down.shape)
            else:
                a, _ = _fox_attn(qkv, cum, batch, heads, tq0)
            xf = _out_proj(a, fox_out, j, xf, gpost, mod, tm)
        elif kind == 1:
            groups = sgu_w_s.shape[1]
            xf = _sgu(xf, gpre, gpost, mod, sgu_w_in[j].astype(BF16),
                      sgu_ln_g[j].reshape(1, -1), sgu_ln_b[j].reshape(1, -1),
                      sgu_w_s[j].astype(BF16), _pad_lanes(sgu_b_s[j].T),
                      sgu_w_out[j].astype(BF16), _pick(seq, 256))
        else:
            dh = SWA_HEAD_DIM
            hq = swa_sinks.shape[1]
            hk = (swa_w_in.shape[2] // dh - hq) // 2
            rope = dh // 4
            half = rope // 2
            inv = ROPE_THETA ** (-jnp.arange(0, rope, 2, dtype=F32) / rope)
            lane_d = jnp.arange(LANES) % dh
            inv_l = jnp.where(lane_d < rope, inv[lane_d % half], 0.0).reshape(1, LANES).astype(F32)
            m1 = jnp.where(lane_d < half, -1.0, 0.0).reshape(1, LANES).astype(F32)
            m2 = jnp.where((lane_d >= half) & (lane_d < rope), 1.0, 0.0).reshape(1, LANES).astype(F32)
            qkv = _swa_proj(xf, gpre, mod, swa_w_in[j].astype(BF16), positions.reshape(t, 1),
                            inv_l, m1, m2, tm_proj, _pick(hk * dh, 512), (hq + hk) * dh, half)
            a = _swa_attn(qkv, swa_sinks[j], batch, hq, hk, dh)
            xf = _out_proj(a, swa_out, j, xf, gpost, mod, tm)
        xf = _ffn_stream(xf, ffn_pre_g[i].reshape(1, d), ffn_post_g[i].reshape(1, d), mod,
                         ffn_gu, ffn_down, i, tm, tf)
    return xf.reshape(batch, seq, d)
```

```python
import functools

import jax
import jax.numpy as jnp
from jax import lax
from jax.experimental import pallas as pl
from jax.experimental.pallas import tpu as pltpu

F32 = jnp.float32
BF16 = jnp.bfloat16

EPS = 1e-6
N_MIXERS = 3
BLOCK = 128
SWA_HEAD_DIM = 64
SWA_BLOCKS_PER_STEP = 2
ROPE_THETA = 500000.0
LANES = 128
SUBLANES = 8
N_MOD = 6
LOG2E = 1.4426950408889634
ONES_ROWS = 16
SOFTMAX_ROWS = 32
SAFE_LOG2 = 100.0
SKIP_LOG2 = 160.0
VMEM_LIMIT = 56 * 1024 * 1024


def _cparams(*sem):
    return pltpu.CompilerParams(dimension_semantics=sem, vmem_limit_bytes=VMEM_LIMIT)


def _const_spec(shape):
    nd = len(shape)
    return pl.BlockSpec(shape, lambda *_: (0,) * nd, pipeline_mode=pl.Buffered(1))


def _pick(n, pref):
    t = min(n, pref)
    while n % t:
        t -= LANES
    return t


def _rms(x):
    return x * lax.rsqrt(jnp.mean(x * x, axis=-1, keepdims=True) + EPS)


def _prenorm(x, g, sh, sc):
    return _rms(x) * (g * (1.0 + sc)) + sh


def _postnorm_residual(x, y, g, gate):
    return x + _rms(y) * (gate * g)


def _dot(a, b):
    return jnp.dot(a, b, preferred_element_type=F32)


def _dot_nt(a, b):
    return lax.dot_general(a, b, (((1,), (1,)), ((), ())), preferred_element_type=F32)


def _adaln_kernel(c_ref, w_ref, b_ref, o_ref):
    c = c_ref[...]
    ca = (c * jax.nn.sigmoid(c)).astype(BF16)
    o_ref[0] = _dot(ca, w_ref[0].astype(BF16)) + b_ref[0]


def _adaln(c_pad, ada_w, ada_b):
    depth, d, n = ada_w.shape
    tn = _pick(n, 1024)
    return pl.pallas_call(
        _adaln_kernel,
        out_shape=jax.ShapeDtypeStruct((depth, SUBLANES, n), F32),
        grid=(depth, n // tn),
        in_specs=[
            pl.BlockSpec((SUBLANES, d), lambda i, j: (0, 0)),
            pl.BlockSpec((1, d, tn), lambda i, j: (i, 0, j)),
            pl.BlockSpec((1, 1, tn), lambda i, j: (i, 0, j)),
        ],
        out_specs=pl.BlockSpec((1, SUBLANES, tn), lambda i, j: (i, 0, j)),
        compiler_params=_cparams("parallel", "parallel"),
        name="adaln",
    )(c_pad, ada_w, ada_b.reshape(depth, 1, n))


class _Mod:
    def __init__(self, mod, layer, batch, seq):
        self.mod = mod
        self.layer = layer
        self.batch = batch
        self.seq = seq
        self.d = mod.shape[-1]

    def spec(self, k, tm, ahead=0):
        layer, tpb = self.layer, self.seq // tm
        last = self.batch * tpb - 1
        return pl.BlockSpec(
            (1, 1, self.d),
            lambda i, *_: ((layer * SUBLANES + jnp.clip(i + ahead, 0, last) // tpb) * N_MOD + k, 0, 0))


def _x_ahead_spec(tm, d, ntiles, nj):
    first_ahead = nj - 2 if nj >= 3 else nj - 1
    return pl.BlockSpec(
        (tm, d), lambda i, j: (jnp.minimum(jnp.where(j >= first_ahead, i + 1, i), ntiles - 1), 0))


def _fox_proj_kernel(x_ref, g_ref, sh_ref, sc_ref, shn_ref, scn_ref, w_ref, wf_ref, o_ref, fg_ref,
                     h_ref):
    i, j = pl.program_id(0), pl.program_id(1)
    cur = i % 2

    @pl.when(jnp.logical_and(i == 0, j == 0))
    def _():
        h_ref[0] = _prenorm(x_ref[...], g_ref[...], sh_ref[0], sc_ref[0]).astype(BF16)

    @pl.when(j == 0)
    def _():
        h = h_ref[cur]
        fg_ref[...] = _dot(h, wf_ref[...])
        o_ref[...] = _dot(h, w_ref[...]).astype(o_ref.dtype)

    @pl.when(jnp.logical_and(j > 0, j < pl.num_programs(1) - 1))
    def _():
        o_ref[...] = _dot(h_ref[cur], w_ref[...]).astype(o_ref.dtype)

    @pl.when(j == pl.num_programs(1) - 1)
    def _():
        h_ref[1 - cur] = _prenorm(x_ref[...], g_ref[...], shn_ref[0], scn_ref[0]).astype(BF16)
        o_ref[...] = _dot(h_ref[cur], w_ref[...]).astype(o_ref.dtype)


def _fox_proj(x, gain, mod, w, wf, layer, n, tm, tn):
    t, d = x.shape
    nj = n // tn
    assert nj >= 2
    return pl.pallas_call(
        _fox_proj_kernel,
        out_shape=(jax.ShapeDtypeStruct((t, n), BF16), jax.ShapeDtypeStruct((t, LANES), F32)),
        grid=(t // tm, nj),
        in_specs=[
            _x_ahead_spec(tm, d, t // tm, nj),
            pl.BlockSpec((1, d), lambda i, j: (0, 0)),
            mod.spec(0, tm), mod.spec(1, tm), mod.spec(0, tm, 1), mod.spec(1, tm, 1),
            pl.BlockSpec((None, d, tn), lambda i, j: (layer, 0, j)),
            _const_spec((d, LANES)),
        ],
        out_specs=(pl.BlockSpec((tm, tn), lambda i, j: (i, j)),
                   pl.BlockSpec((tm, LANES), lambda i, j: (i, 0))),
        scratch_shapes=[pltpu.VMEM((2, tm, d), BF16)],
        compiler_params=_cparams("arbitrary", "arbitrary"),
        name="fox_proj",
    )(x, gain, mod.mod, mod.mod, mod.mod, mod.mod, w, wf)


def _swa_proj_kernel(x_ref, g_ref, sh_ref, sc_ref, shn_ref, scn_ref, w_ref, pos_ref, inv_ref,
                     m1_ref, m2_ref, o_ref, h_ref, cos_ref, s1_ref, s2_ref, *, n_rope_tiles, shift):
    i, j = pl.program_id(0), pl.program_id(1)
    cur = i % 2

    def stage(slot, sh, sc):
        h_ref[slot] = _prenorm(x_ref[...], g_ref[...], sh[0], sc[0]).astype(BF16)
        ang = pos_ref[...].astype(F32) * inv_ref[...]
        sn = jnp.sin(ang)
        cos_ref[slot] = jnp.cos(ang)
        s1_ref[slot] = sn * m1_ref[...]
        s2_ref[slot] = sn * m2_ref[...]

    def rope_tile():
        acc = _dot(h_ref[cur], w_ref[...])
        cs, s1, s2 = cos_ref[cur], s1_ref[cur], s2_ref[cur]
        for c in range(acc.shape[1] // LANES):
            a = acc[:, c * LANES:(c + 1) * LANES]
            r = a * cs + pltpu.roll(a, LANES - shift, 1) * s1 + pltpu.roll(a, shift, 1) * s2
            o_ref[:, c * LANES:(c + 1) * LANES] = r.astype(o_ref.dtype)

    @pl.when(jnp.logical_and(i == 0, j == 0))
    def _():
        stage(0, sh_ref, sc_ref)

    @pl.when(j < n_rope_tiles)
    def _():
        rope_tile()

    @pl.when(jnp.logical_and(j >= n_rope_tiles, j < pl.num_programs(1) - 1))
    def _():
        o_ref[...] = _dot(h_ref[cur], w_ref[...]).astype(o_ref.dtype)

    @pl.when(j == pl.num_programs(1) - 1)
    def _():
        stage(1 - cur, shn_ref, scn_ref)
        o_ref[...] = _dot(h_ref[cur], w_ref[...]).astype(o_ref.dtype)


def _swa_proj(x, gain, mod, w, pos, inv_l, m1, m2, tm, tn, n_rope_cols, shift):
    t, d = x.shape
    n = w.shape[1]
    nt, nj = t // tm, n // tn
    assert n_rope_cols // tn < nj
    row = lambda shape: pl.BlockSpec(shape, lambda i, j: (0, 0))
    ahead = lambda i, j: (jnp.where(j == nj - 1, jnp.minimum(i + 1, nt - 1), i), 0)
    return pl.pallas_call(
        functools.partial(_swa_proj_kernel, n_rope_tiles=n_rope_cols // tn, shift=shift),
        out_shape=jax.ShapeDtypeStruct((t, n), BF16),
        grid=(nt, nj),
        in_specs=[
            _x_ahead_spec(tm, d, nt, nj),
            row((1, d)),
            mod.spec(0, tm), mod.spec(1, tm), mod.spec(0, tm, 1), mod.spec(1, tm, 1),
            pl.BlockSpec((d, tn), lambda i, j: (0, j)),
            pl.BlockSpec((tm, 1), ahead),
            row((1, LANES)), row((1, LANES)), row((1, LANES)),
        ],
        out_specs=pl.BlockSpec((tm, tn), lambda i, j: (i, j)),
        scratch_shapes=[pltpu.VMEM((2, tm, d), BF16)] + [pltpu.VMEM((2, tm, LANES), F32)] * 3,
        compiler_params=_cparams("arbitrary", "arbitrary"),
        name="swa_proj",
    )(x, gain, mod.mod, mod.mod, mod.mod, mod.mod, w, pos, inv_l, m1, m2)


def _out_kernel(a_ref, w_ref, x_ref, g_ref, gate_ref, o_ref):
    y = _dot(a_ref[...], w_ref[...])
    o_ref[...] = _postnorm_residual(x_ref[...], y, g_ref[...], gate_ref[0])


def _out_proj(a, w, layer, x, gain, mod, tm):
    t, k = a.shape
    d = w.shape[2]
    return pl.pallas_call(
        _out_kernel,
        out_shape=jax.ShapeDtypeStruct((t, d), F32),
        grid=(t // tm,),
        in_specs=[
            pl.BlockSpec((tm, k), lambda i: (i, 0)),
            pl.BlockSpec((None, k, d), lambda i: (layer, 0, 0), pipeline_mode=pl.Buffered(1)),
            pl.BlockSpec((tm, d), lambda i: (i, 0)),
            pl.BlockSpec((1, d), lambda i: (0, 0)),
            mod.spec(2, tm),
        ],
        out_specs=pl.BlockSpec((tm, d), lambda i: (i, 0)),
        compiler_params=_cparams("parallel"),
        name="out_proj",
    )(a, w, x, gain, mod.mod)


def _ffn_kernel(x_ref, gpre_ref, sh_ref, sc_ref, shn_ref, scn_ref, wg_ref, wu_ref, wd_ref,
                gpost_ref, gate_ref, o_ref, h_ref, xs_ref, acc_ref):
    i, j = pl.program_id(0), pl.program_id(1)
    last = pl.num_programs(1) - 1
    cur = i % 2

    def stage(slot, sh, sc):
        x = x_ref[...]
        xs_ref[slot] = x
        h_ref[slot] = _prenorm(x, gpre_ref[...], sh[0], sc[0]).astype(BF16)

    def chunk():
        h = h_ref[cur]
        g = _dot(h, wg_ref[...])
        u = _dot(h, wu_ref[...])
        a = (g * jax.nn.sigmoid(g) * u).astype(BF16)
        return _dot(a, wd_ref[...])

    @pl.when(jnp.logical_and(i == 0, j == 0))
    def _():
        stage(0, sh_ref, sc_ref)

    @pl.when(j == 0)
    def _():
        acc_ref[...] = chunk()

    @pl.when(jnp.logical_and(j > 0, j < last))
    def _():
        acc_ref[...] += chunk()

    @pl.when(j == last)
    def _():
        stage(1 - cur, shn_ref, scn_ref)
        y = acc_ref[...] + chunk()
        o_ref[...] = _postnorm_residual(xs_ref[cur], y, gpost_ref[...], gate_ref[0])


def _ffn(x, gpre, gpost, mod, w_gu, w_down, layer, tm, tf):
    t, d = x.shape
    dff = w_down.shape[1]
    nf = dff // tf
    assert nf >= 2
    row = lambda: pl.BlockSpec((1, d), lambda i, j: (0, 0))
    return pl.pallas_call(
        _ffn_kernel,
        out_shape=jax.ShapeDtypeStruct((t, d), F32),
        grid=(t // tm, nf),
        in_specs=[
            _x_ahead_spec(tm, d, t // tm, nf),
            row(), mod.spec(3, tm), mod.spec(4, tm), mod.spec(3, tm, 1), mod.spec(4, tm, 1),
            pl.BlockSpec((None, d, tf), lambda i, j: (layer, 0, j)),
            pl.BlockSpec((None, d, tf), lambda i, j: (layer, 0, nf + j)),
            pl.BlockSpec((None, tf, d), lambda i, j: (layer, j, 0)),
            row(), mod.spec(5, tm),
        ],
        out_specs=pl.BlockSpec((tm, d), lambda i, j: (i, 0)),
        scratch_shapes=[pltpu.VMEM((2, tm, d), BF16), pltpu.VMEM((2, tm, d), F32),
                        pltpu.VMEM((tm, d), F32)],
        compiler_params=_cparams("arbitrary", "arbitrary"),
        name="ffn",
    )(x, gpre, mod.mod, mod.mod, mod.mod, mod.mod, w_gu, w_gu, w_down, gpost, mod.mod)


def _gate_kernel(fg_ref, b_ref, cum_ref, carry_ref):
    @pl.when(pl.program_id(1) == 0)
    def _():
        carry_ref[...] = jnp.zeros_like(carry_ref)

    z = fg_ref[...] + b_ref[...]
    lf = jnp.minimum(z, 0.0) - jnp.log1p(jnp.exp(-jnp.abs(z)))
    n = lf.shape[0]
    tril = (lax.broadcasted_iota(jnp.int32, (n, n), 1)
            <= lax.broadcasted_iota(jnp.int32, (n, n), 0)).astype(BF16)
    hi = lf.astype(BF16)
    r1 = lf - hi.astype(F32)
    mid = r1.astype(BF16)
    lo = (r1 - mid.astype(F32)).astype(BF16)
    cum = (_dot(tril, hi) + _dot(tril, mid)) + _dot(tril, lo) + carry_ref[...]
    cum_ref[...] = cum
    carry_ref[...] = cum[n - 1:n, :]


def _gate_cumsum(fg, b_pad, batch, tg):
    t = fg.shape[0]
    s = t // batch
    nt = s // tg
    return pl.pallas_call(
        _gate_kernel,
        out_shape=jax.ShapeDtypeStruct((t, LANES), F32),
        grid=(batch, nt),
        in_specs=[
            pl.BlockSpec((tg, LANES), lambda b, j: (b * nt + j, 0)),
            pl.BlockSpec((1, LANES), lambda b, j: (0, 0)),
        ],
        out_specs=pl.BlockSpec((tg, LANES), lambda b, j: (b * nt + j, 0)),
        scratch_shapes=[pltpu.VMEM((1, LANES), F32)],
        compiler_params=_cparams("parallel", "arbitrary"),
        name="fox_gate",
    )(fg, b_pad)


def _lane_col(x, h):
    lane = lax.broadcasted_iota(jnp.int32, x.shape, 1)
    return jnp.sum(jnp.where(lane == h, x, 0.0), axis=-1, keepdims=True)


def _decay_lanes(x, h, base):
    hi = x.astype(BF16)
    r = x - hi.astype(F32)
    mid = r.astype(BF16)
    lo = (r - mid.astype(F32)).astype(BF16)
    src = lax.broadcasted_iota(jnp.int32, (3 * LANES, LANES), 0)
    dst = lax.broadcasted_iota(jnp.int32, (3 * LANES, LANES), 1)
    place = jnp.where(jnp.logical_and(src % LANES == h, dst == base + src // LANES), 1.0, 0.0)
    moved = _dot(jnp.concatenate([hi, mid, lo], axis=1), place.astype(BF16))
    lane = lax.broadcasted_iota(jnp.int32, (1, LANES), 1)
    ones = jnp.where(jnp.logical_and(lane >= 3 - base, lane < 6 - base), 1.0, 0.0)
    return (moved + ones).astype(BF16)


def _fox_attn_kernel(*refs, tk, qscale, n_side):
    q_ref, k_ref, v_ref, cum_ref = refs[:4]
    side_in = refs[4:4 + n_side]
    o_ref = refs[4 + n_side]
    side_out = refs[5 + n_side:5 + 2 * n_side]
    (ka_ref, vt_ref, qa_ref, kstat_ref, sa_ref, sb_ref, pa_ref, pb_ref, m_ref, ala_ref, alb_ref,
     acc_ref) = refs[5 + 2 * n_side:]
    for src, dst in zip(side_in, side_out):
        dst[...] = src[...].astype(BF16)

    h = pl.program_id(1)
    i = pl.program_id(2)
    tq, dh = q_ref.shape
    nkv = vt_ref.shape[0]
    lane_row = lax.broadcasted_iota(jnp.int32, (1, LANES), 1)

    @pl.when(i == 0)
    def _():
        kn2 = jnp.zeros((1, 1), F32)
        for c in range(nkv):
            rows = slice(c * tk, (c + 1) * tk)
            kc = k_ref[rows, :]
            kf = kc.astype(F32)
            kn2 = jnp.maximum(kn2, jnp.max(jnp.sum(kf * kf, axis=1, keepdims=True),
                                           axis=0, keepdims=True))
            ka_ref[rows, :dh] = kc
            ka_ref[rows, dh:] = _decay_lanes(cum_ref[rows, :] * (-LOG2E), h, 0)
            vt_ref[c, :dh, :] = v_ref[rows, :].T
            vt_ref[c, dh:, :] = jnp.ones((vt_ref.shape[1] - dh, tk), BF16)
        stat = jnp.where(lane_row == LANES - 1, kn2, 0.0)
        for k in range(nkv // 2):
            last = (k + 1) * tq - 1
            stat = jnp.where(lane_row == k, _lane_col(cum_ref[last:last + 1, :], h) * LOG2E, stat)
        kstat_ref[...] = stat

    qs = (q_ref[...].astype(F32) * qscale).astype(BF16)
    fq = cum_ref[pl.ds(pl.multiple_of(i * tq, tq), tq), :] * LOG2E
    qa_ref[:, :dh] = qs
    qa_ref[:, dh:] = _decay_lanes(fq, h, 3)
    qf = qs.astype(F32)
    qn2 = jnp.max(jnp.sum(qf * qf, axis=1, keepdims=True), axis=0, keepdims=True)
    fq_max = _lane_col(jnp.max(fq, axis=0, keepdims=True), h)
    acc_ref[...] = jnp.zeros(acc_ref.shape, F32)

    def qk(c, s_out, lo=0):
        ka = ka_ref[pl.ds(pl.multiple_of(c * tk, tk), tk), :]
        s_out[:, lo:] = _dot_nt(ka, qa_ref[lo:, :])

    def pv(c, p_in, al_in=None):
        acc = acc_ref[...] if al_in is None else al_in[...] * acc_ref[...]
        acc_ref[...] = acc + _dot(vt_ref[c], p_in[...])

    rc = SOFTMAX_ROWS
    nrc = tk // rc

    def chunk(s_in, r, key_offset, lo):
        blk = s_in[r * rc:(r + 1) * rc, lo:]
        if key_offset is not None:
            col = lax.broadcasted_iota(jnp.int32, blk.shape, 1) + lo
            row = lax.broadcasted_iota(jnp.int32, blk.shape, 0)
            blk = jnp.where(row + (r * rc + key_offset) <= col, blk, -jnp.inf)
        return blk

    def column_max(blocks, width):
        part = jnp.full((SUBLANES, width), -jnp.inf, F32)
        for blk in blocks:
            part = jnp.maximum(part, jnp.max(blk.reshape(rc // SUBLANES, SUBLANES, width), axis=0))
        return jnp.max(part, axis=0, keepdims=True)

    def softmax_fixed(s_in, p_out, ref, key_offset=None, lo=0):
        for r in range(nrc):
            p_out[r * rc:(r + 1) * rc, lo:] = jnp.exp2(chunk(s_in, r, key_offset, lo) - ref).astype(BF16)

    def softmax(s_in, p_out, al_out, key_offset=None, lo=0):
        m = m_ref[:, lo:]
        m_new = jnp.maximum(m, column_max((chunk(s_in, r, key_offset, lo) for r in range(nrc)),
                                          tq - lo))
        for r in range(nrc):
            p_out[r * rc:(r + 1) * rc, lo:] = jnp.exp2(chunk(s_in, r, key_offset, lo) - m_new).astype(BF16)
        m_ref[:, lo:] = m_new
        al_out[:, lo:] = jnp.exp2(m - m_new)

    def older_pairs(m_min):
        live = jnp.logical_and(lane_row < i, qk_bound + fq_max - stat >= m_min - SKIP_LOG2)
        first = jnp.min(jnp.where(live, lane_row, i).astype(F32), axis=1, keepdims=True)
        return i - first[0, 0].astype(jnp.int32)

    def finish():
        o_ref[...] = (acc_ref[:dh, :] / acc_ref[dh:dh + 1, :]).T.astype(o_ref.dtype)

    def diagonal_scores():
        qk(2 * i, sa_ref)
        qk(2 * i + 1, sb_ref, tk)
        pb_ref[:, :tk] = jnp.zeros((tk, tk), BF16)

    stat = kstat_ref[...]
    kn2 = jnp.sum(jnp.where(lane_row == LANES - 1, stat, 0.0), axis=1, keepdims=True)
    qk_bound = jnp.sqrt(qn2 * kn2)
    calm = (2.0 * qk_bound)[0, 0] < SAFE_LOG2

    @pl.when(calm)
    def _():
        diagonal_scores()
        softmax_fixed(sa_ref, pa_ref, qk_bound, 0)
        pv(2 * i, pa_ref)
        softmax_fixed(sb_ref, pb_ref, qk_bound, tk, tk)
        n_old = older_pairs(qk_bound)

        def body(j, carry):
            k = i - 1 - j
            qk(2 * k, sa_ref)
            qk(2 * k + 1, sb_ref)
            pv(2 * k + 3, pb_ref)
            softmax_fixed(sa_ref, pa_ref, qk_bound)
            pv(2 * k, pa_ref)
            softmax_fixed(sb_ref, pb_ref, qk_bound)
            return carry

        lax.fori_loop(0, n_old, body, 0)
        pv(2 * (i - n_old) + 1, pb_ref)
        finish()

    @pl.when(jnp.logical_not(calm))
    def _():
        diagonal_scores()
        m_ref[...] = jnp.full(m_ref.shape, -jnp.inf, F32)
        softmax(sa_ref, pa_ref, ala_ref, 0)
        pv(2 * i, pa_ref, ala_ref)
        alb_ref[:, :tk] = jnp.ones((1, tk), F32)
        softmax(sb_ref, pb_ref, alb_ref, tk, tk)
        n_old = older_pairs(jnp.min(m_ref[...], axis=1, keepdims=True))

        def body(j, carry):
            k = i - 1 - j
            qk(2 * k, sa_ref)
            qk(2 * k + 1, sb_ref)
            pv(2 * k + 3, pb_ref, alb_ref)
            softmax(sa_ref, pa_ref, ala_ref)
            pv(2 * k, pa_ref, ala_ref)
            softmax(sb_ref, pb_ref, alb_ref)
            return carry

        lax.fori_loop(0, n_old, body, 0)
        pv(2 * (i - n_old) + 1, pb_ref, alb_ref)
        finish()


def _slab_rows(a, nsteps):
    tile = 2 * SUBLANES
    if a.shape[0] % tile or a.shape[1] % LANES:
        return None
    units = a.shape[0] // tile
    for k in range(-(-units // nsteps), units + 1):
        if units % k == 0:
            return k * tile
    return None


def _fox_attn(qkv, cum, batch, heads, tq, side=()):
    t = qkv.shape[0]
    dh = qkv.shape[1] // (3 * heads)
    s = t // batch
    nq = s // tq
    tk = tq // 2
    nsteps = batch * heads * nq

    def slab(a):
        rows = _slab_rows(a, nsteps)
        last = a.shape[0] // rows - 1
        return pl.BlockSpec((rows, a.shape[1]),
                            lambda b, h, i: (jnp.minimum((b * heads + h) * nq + i, last), 0))

    outs = pl.pallas_call(
        functools.partial(_fox_attn_kernel, tk=tk, qscale=dh ** -0.5 * LOG2E, n_side=len(side)),
        out_shape=[jax.ShapeDtypeStruct((t, heads * dh), BF16)]
        + [jax.ShapeDtypeStruct(a.shape, BF16) for a in side],
        grid=(batch, heads, nq),
        in_specs=[
            pl.BlockSpec((tq, dh), lambda b, h, i: (b * nq + i, h)),
            pl.BlockSpec((s, dh), lambda b, h, i: (b, heads + h)),
            pl.BlockSpec((s, dh), lambda b, h, i: (b, 2 * heads + h)),
            pl.BlockSpec((s, LANES), lambda b, h, i: (b, 0)),
        ] + [slab(a) for a in side],
        out_specs=[pl.BlockSpec((tq, dh), lambda b, h, i: (b * nq + i, h))] + [slab(a) for a in side],
        scratch_shapes=[
            pltpu.VMEM((s, 2 * dh), BF16),
            pltpu.VMEM((s // tk, dh + ONES_ROWS, tk), BF16),
            pltpu.VMEM((tq, 2 * dh), BF16),
            pltpu.VMEM((1, LANES), F32),
            pltpu.VMEM((tk, tq), F32), pltpu.VMEM((tk, tq), F32),
            pltpu.VMEM((tk, tq), BF16), pltpu.VMEM((tk, tq), BF16),
            pltpu.VMEM((1, tq), F32),
            pltpu.VMEM((1, tq), F32), pltpu.VMEM((1, tq), F32),
            pltpu.VMEM((dh + ONES_ROWS, tq), F32),
        ],
        compiler_params=_cparams("parallel", "parallel", "arbitrary"),
        name="fox_attn",
    )(qkv, qkv, qkv, cum, *side)
    return outs[0], outs[1:]


def _sgu_kernel(x_ref, gpre_ref, sh_ref, sc_ref, win_ref, lng_ref, lnb_ref, ws_ref, bs_ref,
                wout_ref, gpost_ref, gate_ref, o_ref, u_ref, vn_ref, gated_ref):
    x = x_ref[...]
    tm = x.shape[0]
    width = u_ref.shape[1]
    groups, chunk, _ = ws_ref.shape
    gd = width // groups
    h = _prenorm(x, gpre_ref[...], sh_ref[0], sc_ref[0]).astype(BF16)
    zv = _dot(h, win_ref[:, width:])
    zu = _dot(h, win_ref[:, :width])
    v = jax.nn.gelu(zv)
    mu = jnp.mean(v, axis=-1, keepdims=True)
    vc = v - mu
    var = jnp.mean(vc * vc, axis=-1, keepdims=True)
    vn_ref[...] = (vc * lax.rsqrt(var + EPS) * lng_ref[...] + lnb_ref[...]).astype(BF16)
    u_ref[...] = jax.nn.gelu(zu)
    causal = (lax.broadcasted_iota(jnp.int32, (chunk, chunk), 1)
              <= lax.broadcasted_iota(jnp.int32, (chunk, chunk), 0))
    for g in range(groups):
        wg = jnp.where(causal, ws_ref[g], jnp.zeros((), BF16))
        bias = bs_ref[:, g:g + 1]
        cols = slice(g * gd, (g + 1) * gd)
        for c in range(tm // chunk):
            rows = slice(c * chunk, (c + 1) * chunk)
            f = _dot(wg, vn_ref[rows, cols]) + bias
            gated_ref[rows, cols] = (u_ref[rows, cols] * f).astype(BF16)
    y = _dot(gated_ref[...], wout_ref[...])
    o_ref[...] = _postnorm_residual(x, y, gpost_ref[...], gate_ref[0])


def _sgu(x, gpre, gpost, mod, w_in, ln_g, ln_b, w_s, b_st, w_out, tm):
    t, d = x.shape
    width = w_out.shape[0]
    row = lambda n: pl.BlockSpec((1, n), lambda i: (0, 0))
    return pl.pallas_call(
        _sgu_kernel,
        out_shape=jax.ShapeDtypeStruct((t, d), F32),
        grid=(t // tm,),
        in_specs=[
            pl.BlockSpec((tm, d), lambda i: (i, 0)),
            row(d), mod.spec(0, tm), mod.spec(1, tm),
            _const_spec(w_in.shape),
            row(width), row(width),
            _const_spec(w_s.shape),
            _const_spec(b_st.shape),
            _const_spec(w_out.shape),
            row(d), mod.spec(2, tm),
        ],
        out_specs=pl.BlockSpec((tm, d), lambda i: (i, 0)),
        scratch_shapes=[pltpu.VMEM((tm, width), F32), pltpu.VMEM((tm, width), BF16),
                        pltpu.VMEM((tm, width), BF16)],
        compiler_params=_cparams("parallel"),
        name="sgu",
    )(x, gpre, mod.mod, mod.mod, w_in, ln_g, ln_b, w_s, b_st, w_out, gpost, mod.mod)


def _swa_attn_kernel(sink_ref, q_ref, kc_ref, kp_ref, vc_ref, vp_ref, o_ref, *, nb, grp, scale):
    blk = kp_ref.shape[0]
    nsub = q_ref.shape[0] // blk
    half = LANES // 2
    kv_cols = kc_ref.shape[1] // LANES
    npair = grp // 2
    row = lax.broadcasted_iota(jnp.int32, (blk, blk), 0)
    col = lax.broadcasted_iota(jnp.int32, (blk, blk), 1)
    keep_c = row <= col
    lo = lax.broadcasted_iota(jnp.int32, (2 * blk, LANES), 1) < half
    scores, vts, keeps = {}, {}, {}
    for s in range(nsub):
        rows = slice(s * blk, (s + 1) * blk)
        n = (pl.program_id(0) * nsub + s) % nb
        keep_p = jnp.logical_and(row > col, n > 0)
        keep = jnp.concatenate([keep_p, keep_c], axis=0)
        keeps[s] = jnp.concatenate([keep] * npair, axis=1)
        for j in range(kv_cols):
            csl = slice(j * LANES, (j + 1) * LANES)
            k_prev = kp_ref[:, csl] if s == 0 else kc_ref[(s - 1) * blk:s * blk, csl]
            v_prev = vp_ref[:, csl] if s == 0 else vc_ref[(s - 1) * blk:s * blk, csl]
            kf = jnp.concatenate([k_prev, kc_ref[rows, csl]], axis=0).astype(F32) * scale
            kr = pltpu.roll(kf, half, 1)
            vt = jnp.concatenate([v_prev, vc_ref[rows, csl]], axis=0).T
            for e in range(2):
                kh = 2 * j + e
                qcols = [q_ref[rows, (kh * npair + a) * LANES:(kh * npair + a + 1) * LANES]
                         for a in range(npair)]
                rhs = jnp.concatenate(qcols, axis=0)
                vts[s, kh] = vt[e * half:(e + 1) * half, :]
                for p in range(2):
                    src = kf if p == e else kr
                    kz = jnp.where(lo if p == 0 else jnp.logical_not(lo), src, 0.0).astype(BF16)
                    scores[s, kh, p] = _dot_nt(kz, rhs)
    probs = {}
    for (s, kh, p), st in scores.items():
        st = jnp.where(keeps[s], st, -jnp.inf)
        sink = jnp.concatenate(
            [jnp.full((1, blk), sink_ref[kh * grp + 2 * a + p], F32) for a in range(npair)], axis=1)
        m = jnp.maximum(jnp.max(st, axis=0, keepdims=True), sink)
        pt = jnp.exp(st - m)
        den = jnp.sum(pt, axis=0, keepdims=True) + jnp.exp(sink - m)
        probs[s, kh, p] = (pt.astype(BF16), den)
    for s in range(nsub):
        for kh in range(2 * kv_cols):
            outs = [_dot(vts[s, kh], probs[s, kh, p][0]) / probs[s, kh, p][1] for p in range(2)]
            for a in range(npair):
                ot = jnp.concatenate([o[:, a * blk:(a + 1) * blk] for o in outs], axis=0)
                c0 = (kh * npair + a) * LANES
                o_ref[s * blk:(s + 1) * blk, c0:c0 + LANES] = ot.T.astype(o_ref.dtype)


def _swa_attn(qkv, sinks, batch, hq, hk, dh):
    t = qkv.shape[0]
    nb = t // batch // BLOCK
    grp = hq // hk
    assert 2 * dh == LANES and grp % 2 == 0 and hk % 2 == 0
    nsub = SWA_BLOCKS_PER_STEP if nb % SWA_BLOCKS_PER_STEP == 0 else 1
    rows = nsub * BLOCK
    prev = lambda r: jnp.maximum(r * nsub - 1, 0)
    return pl.pallas_call(
        functools.partial(_swa_attn_kernel, nb=nb, grp=grp, scale=dh ** -0.5),
        out_shape=jax.ShapeDtypeStruct((t, hq * dh), BF16),
        grid=(t // rows,),
        in_specs=[
            pl.BlockSpec(memory_space=pltpu.SMEM),
            pl.BlockSpec((rows, hq * dh), lambda r: (r, 0)),
            pl.BlockSpec((rows, hk * dh), lambda r: (r, grp)),
            pl.BlockSpec((BLOCK, hk * dh), lambda r: (prev(r), grp)),
            pl.BlockSpec((rows, hk * dh), lambda r: (r, grp + 1)),
            pl.BlockSpec((BLOCK, hk * dh), lambda r: (prev(r), grp + 1)),
        ],
        out_specs=pl.BlockSpec((rows, hq * dh), lambda r: (r, 0)),
        compiler_params=_cparams("parallel"),
        name="swa_attn",
    )(sinks, qkv, qkv, qkv, qkv, qkv)


def _pad_lanes(a, n=LANES):
    return jnp.pad(a, [(0, 0)] * (a.ndim - 1) + [(0, n - a.shape[-1])])


def kernel(x, c, positions, ada_w, ada_b, mix_pre_g, mix_post_g, ffn_pre_g, ffn_post_g, ffn_w_gu, ffn_w_down, fox_w_in, fox_b_f, fox_w_out, sgu_w_in, sgu_ln_g, sgu_ln_b, sgu_w_s, sgu_b_s, sgu_w_out, swa_w_in, swa_sinks, swa_w_out):
    batch, seq, d = x.shape
    depth = ada_w.shape[0]
    t = batch * seq
    assert batch <= SUBLANES and seq % BLOCK == 0 and d % LANES == 0

    tm = _pick(seq, 512)
    tm_proj = _pick(seq, 1024)
    xf = x.reshape(t, d)

    c_pad = jnp.pad(c, ((0, SUBLANES - batch), (0, 0)))
    mod_all = _adaln(c_pad, ada_w, ada_b).reshape(depth * SUBLANES * N_MOD, 1, d)

    dff = ffn_w_down.shape[1]
    tf = _pick(dff, 512)
    fox_in, fox_out = fox_w_in.astype(BF16), fox_w_out.astype(BF16)
    swa_out = swa_w_out.astype(BF16)
    heads0, tq0 = fox_b_f.shape[1], _pick(seq, 1024)
    ffn_side = (ffn_w_gu.reshape(-1, 2 * dff), ffn_w_down.reshape(-1, d))
    cast_in_attn = depth > 0 and all(
        _slab_rows(a, batch * heads0 * (seq // tq0)) is not None for a in ffn_side)
    if not cast_in_attn:
        ffn_gu, ffn_down = ffn_w_gu.astype(BF16), ffn_w_down.astype(BF16)

    for i in range(depth):
        kind, j = i % N_MIXERS, i // N_MIXERS
        mod = _Mod(mod_all, i, batch, seq)
        gpre, gpost = mix_pre_g[i].reshape(1, d), mix_post_g[i].reshape(1, d)
        if kind == 0:
            heads = fox_b_f.shape[1]
            nqkv = fox_w_in.shape[2] - heads
            qkv, fg = _fox_proj(xf, gpre, mod, fox_in, _pad_lanes(fox_in[j, :, nqkv:]), j, nqkv,
                                tm_proj, _pick(nqkv, min(1536, nqkv // 2)))
            cum = _gate_cumsum(fg, _pad_lanes(fox_b_f[j].reshape(1, heads)), batch, _pick(seq, 512))
            if i == 0 and cast_in_attn:
                a, (gu2, down2) = _fox_attn(qkv, cum, batch, heads, tq0, ffn_side)
                ffn_gu, ffn_down = gu2.reshape(ffn_w_gu.shape), down2.reshape(ffn_w_down.shape)
            else:
                a, _ = _fox_attn(qkv, cum, batch, heads, tq0)
            xf = _out_proj(a, fox_out, j, xf, gpost, mod, tm)
        elif kind == 1:
            groups = sgu_w_s.shape[1]
            xf = _sgu(xf, gpre, gpost, mod, sgu_w_in[j].astype(BF16),
                      sgu_ln_g[j].reshape(1, -1), sgu_ln_b[j].reshape(1, -1),
                      sgu_w_s[j].astype(BF16), _pad_lanes(sgu_b_s[j].T),
                      sgu_w_out[j].astype(BF16), _pick(seq, 256))
        else:
            dh = SWA_HEAD_DIM
            hq = swa_sinks.shape[1]
            hk = (swa_w_in.shape[2] // dh - hq) // 2
            rope = dh // 4
            half = rope // 2
            inv = ROPE_THETA ** (-jnp.arange(0, rope, 2, dtype=F32) / rope)
            lane_d = jnp.arange(LANES) % dh
            inv_l = jnp.where(lane_d < rope, inv[lane_d % half], 0.0).reshape(1, LANES).astype(F32)
            m1 = jnp.where(lane_d < half, -1.0, 0.0).reshape(1, LANES).astype(F32)
            m2 = jnp.where((lane_d >= half) & (lane_d < rope), 1.0, 0.0).reshape(1, LANES).astype(F32)
            qkv = _swa_proj(xf, gpre, mod, swa_w_in[j].astype(BF16), positions.reshape(t, 1),
                            inv_l, m1, m2, tm_proj, _pick(hk * dh, 512), (hq + hk) * dh, half)
            a = _swa_attn(qkv, swa_sinks[j], batch, hq, hk, dh)
            xf = _out_proj(a, swa_out, j, xf, gpost, mod, tm)
        xf = _ffn(xf, ffn_pre_g[i].reshape(1, d), ffn_post_g[i].reshape(1, d), mod,
                  ffn_gu, ffn_down, i, tm, tf)
    return xf.reshape(batch, seq, d)
```
